```python
import math
import jax, jax.numpy as jnp
from jax import lax
import numpy as np

D_MODEL = 2048
BATCH = 1
SEQ = 8192
DEPTH = 4

N_MIXERS = 2
HEAD_DIM = 128
GDN_QK_HEADS = D_MODEL // HEAD_DIM
GDN_V_HEADS = 2 * GDN_QK_HEADS
GDN_CONV = 4
GDN_CHUNK = 64
GDN_DQK = GDN_QK_HEADS * HEAD_DIM
GDN_DV = GDN_V_HEADS * HEAD_DIM
GDN_CONV_CH = 2 * GDN_DQK + GDN_DV
GDN_IN = GDN_CONV_CH + GDN_DV + 2 * GDN_V_HEADS
NSA_Q_HEADS = D_MODEL // HEAD_DIM
NSA_KV_HEADS = NSA_Q_HEADS // 4
NSA_GROUP = NSA_Q_HEADS // NSA_KV_HEADS
N_BRANCH = 3
CMP_LEN = 32
CMP_STRIDE = 16
SLC_LEN = 64
SLC_TOP = 16
WINDOW = 512
NSA_QBLOCK = 64
NSA_DQ = NSA_Q_HEADS * HEAD_DIM
NSA_DKV = NSA_KV_HEADS * HEAD_DIM
NSA_IN = NSA_DQ + 6 * NSA_DKV + NSA_Q_HEADS * N_BRANCH
_FF_RAW = -(-8 * D_MODEL // 3)
D_FF = -(-_FF_RAW // 256) * 256
ROPE_THETA = 10000.0
POS_OFFSET_MAX = 32768
LN_EPS = 1e-5
NORM_EPS = 1e-6
ALPHA_DN = (2 * DEPTH) ** 0.25
BETA_DN = (8 * DEPTH) ** -0.25

kernel_name = 'hybrid_gdn_nsa_deepnorm_adaln'


def layer_norm(x, g, b):
    xf = x.astype(jnp.float32)
    mu = jnp.mean(xf, -1, keepdims=True)
    var = jnp.mean(jnp.square(xf - mu), -1, keepdims=True)
    return ((xf - mu) * lax.rsqrt(var + LN_EPS)).astype(x.dtype) * g + b


def l2norm(t):
    tf = t.astype(jnp.float32)
    return tf * lax.rsqrt(jnp.sum(tf * tf, -1, keepdims=True) + NORM_EPS)


def rope_tables(positions):
    inv = ROPE_THETA ** (-jnp.arange(0, HEAD_DIM, 2, dtype=jnp.float32) / HEAD_DIM)
    ang = positions.astype(jnp.float32)[..., None] * inv
    return jnp.cos(ang)[:, :, None, :], jnp.sin(ang)[:, :, None, :]


def apply_rope(t, cos, sin):
    t1, t2 = jnp.split(t.astype(jnp.float32), 2, axis=-1)
    return jnp.concatenate([t1 * cos - t2 * sin, t2 * cos + t1 * sin], -1).astype(t.dtype)


def masked_softmax(s, mask):
    s = jnp.where(mask, s.astype(jnp.float32), -jnp.inf)
    m = jnp.max(s, -1, keepdims=True)
    m = jnp.where(jnp.isfinite(m), m, 0.0)
    p = jnp.exp(s - m)
    den = jnp.sum(p, -1, keepdims=True)
    return p / jnp.where(den > 0, den, 1.0)


def causal_depthwise_conv(t, w):
    return lax.conv_general_dilated(t, w[:, None, :].astype(t.dtype), window_strides=(1,),
                                    padding=[(GDN_CONV - 1, 0)],
                                    dimension_numbers=('NWC', 'WIO', 'NWC'),
                                    feature_group_count=t.shape[-1])


def gated_delta_rule(q, k, v, g, beta):
    B, S, H, dk = k.shape
    dv = v.shape[-1]
    C = GDN_CHUNK
    N = S // C

    def chunks(t):
        return jnp.moveaxis(t.astype(jnp.float32).reshape(B, N, C, H, *t.shape[3:]), 3, 1)

    q, k, v, g, beta = [chunks(t) for t in (q, k, v, g, beta)]
    gc = jnp.cumsum(g, -1)
    idx = jnp.arange(C)
    lower = idx[:, None] >= idx[None, :]
    strict = idx[:, None] > idx[None, :]
    diff = gc[..., :, None] - gc[..., None, :]
    decay = jnp.where(lower, jnp.exp(jnp.where(lower, diff, 0.0)), 0.0)
    kb = k * beta[..., None]
    L = jnp.where(strict, jnp.einsum('bhnid,bhnjd->bhnij', kb, k) * decay, 0.0)
    rhs = jnp.concatenate([v * beta[..., None], kb * jnp.exp(gc)[..., None]], -1)
    sol = lax.linalg.triangular_solve(L + jnp.eye(C, dtype=L.dtype), rhs, left_side=True,
                                      lower=True, unit_diagonal=True)
    u, w = sol[..., :dv], sol[..., dv:]
    a_qk = jnp.where(lower, jnp.einsum('bhnid,bhnjd->bhnij', q, k) * decay, 0.0)
    q_g = q * jnp.exp(gc)[..., None]
    g_last = gc[..., -1]
    k_g = k * jnp.exp(g_last[..., None] - gc)[..., None]
    xs = [jnp.moveaxis(t, 2, 0) for t in (q_g, k_g, u, w, a_qk, g_last)]

    def step(state, inp):
        qg, kg, uu, ww, aa, gl = inp
        v_new = uu - jnp.einsum('bhck,bhkv->bhcv', ww, state)
        o = jnp.einsum('bhck,bhkv->bhcv', qg, state) + jnp.einsum('bhij,bhjv->bhiv', aa, v_new)
        state = state * jnp.exp(gl)[..., None, None] + jnp.einsum('bhck,bhcv->bhkv', kg, v_new)
        return state, o

    state0 = jnp.zeros((B, H, dk, dv), jnp.float32)
    _, o = lax.scan(step, state0, xs)
    return jnp.transpose(o, (1, 0, 3, 2, 4)).reshape(B, S, H, dv)


def gated_deltanet(h, w_in, conv_w, a_log, dt_bias, norm_w, w_out):
    B, S, _ = h.shape
    proj = h @ w_in
    qkv, z, b, a = jnp.split(proj, [GDN_CONV_CH, GDN_CONV_CH + GDN_DV,
                                    GDN_CONV_CH + GDN_DV + GDN_V_HEADS], axis=-1)
    qkv = jax.nn.silu(causal_depthwise_conv(qkv, conv_w))
    q, k, v = jnp.split(qkv, [GDN_DQK, 2 * GDN_DQK], axis=-1)
    rep = GDN_V_HEADS // GDN_QK_HEADS
    q = jnp.repeat(l2norm(q.reshape(B, S, GDN_QK_HEADS, HEAD_DIM)) * HEAD_DIM ** -0.5, rep, axis=2)
    k = jnp.repeat(l2norm(k.reshape(B, S, GDN_QK_HEADS, HEAD_DIM)), rep, axis=2)
    v = v.reshape(B, S, GDN_V_HEADS, HEAD_DIM)
    beta = jax.nn.sigmoid(b.astype(jnp.float32))
    g = -jnp.exp(a_log.astype(jnp.float32)) * jax.nn.softplus(a.astype(jnp.float32) + dt_bias.astype(jnp.float32))
    o = gated_delta_rule(q, k, v, g, beta)
    o = o * lax.rsqrt(jnp.mean(o * o, -1, keepdims=True) + NORM_EPS)
    o = (o.astype(h.dtype) * norm_w) * jax.nn.silu(z.reshape(B, S, GDN_V_HEADS, HEAD_DIM))
    return o.reshape(B, S, GDN_DV) @ w_out


def nsa_attention(h, cos, sin, w_in, cmp_pe, cmp_w1, cmp_w2, w_out):
    B, S, _ = h.shape
    Hq, Hk, G, dh, QB = NSA_Q_HEADS, NSA_KV_HEADS, NSA_GROUP, HEAD_DIM, NSA_QBLOCK
    proj = h @ w_in
    q, kc, vc, ks, vs, kw, vw, gates = jnp.split(proj, [NSA_DQ + i * NSA_DKV for i in range(7)], axis=-1)
    q = apply_rope(q.reshape(B, S, Hq, dh), cos, sin).reshape(B, S, Hk, G, dh) * dh ** -0.5
    kc, ks, kw = [apply_rope(t.reshape(B, S, Hk, dh), cos, sin) for t in (kc, ks, kw)]
    vc, vs, vw = [t.reshape(B, S, Hk, dh) for t in (vc, vs, vw)]
    gates = jax.nn.sigmoid(gates.astype(jnp.float32)).reshape(B, S, Hk, G, N_BRANCH).astype(h.dtype)

    n_cmp = (S - CMP_LEN) // CMP_STRIDE + 1
    c_start = CMP_STRIDE * jnp.arange(n_cmp)
    cmp_idx = c_start[:, None] + jnp.arange(CMP_LEN)[None, :]
    cmp_end = c_start + CMP_LEN - 1

    def compress(t, pe, w1, w2):
        blk = t[:, cmp_idx] + pe[:, None, :]
        blk = jnp.moveaxis(blk, 3, 2).reshape(B, n_cmp, Hk, CMP_LEN * dh)
        return jax.nn.silu(blk @ w1) @ w2

    kcmp = compress(kc, cmp_pe[0], cmp_w1[0], cmp_w2[0])
    vcmp = compress(vc, cmp_pe[1], cmp_w1[1], cmp_w2[1])

    n_slc = S // SLC_LEN
    n_sel = min(SLC_TOP, n_slc)
    ks_blk = jnp.moveaxis(ks.reshape(B, n_slc, SLC_LEN, Hk, dh), 3, 1)
    vs_blk = jnp.moveaxis(vs.reshape(B, n_slc, SLC_LEN, Hk, dh), 3, 1)
    s_start = SLC_LEN * jnp.arange(n_slc)
    overlap = jnp.clip(jnp.minimum(c_start[:, None] + CMP_LEN, s_start[None, :] + SLC_LEN)
                       - jnp.maximum(c_start[:, None], s_start[None, :]), 0, None).astype(jnp.float32) / CMP_LEN
    b_ix = jnp.arange(B)[:, None, None, None]
    h_ix = jnp.arange(Hk)[None, None, :, None]

    kw_pad = jnp.pad(kw, ((0, 0), (WINDOW, 0), (0, 0), (0, 0)))
    vw_pad = jnp.pad(vw, ((0, 0), (WINDOW, 0), (0, 0), (0, 0)))

    def block(i):
        t0 = i * QB
        tpos = t0 + jnp.arange(QB)
        qb = lax.dynamic_slice_in_dim(q, t0, QB, 1)
        gb = lax.dynamic_slice_in_dim(gates, t0, QB, 1)
        s_c = jnp.einsum('bqhgd,bnhd->bqhgn', qb, kcmp)
        p_c = masked_softmax(s_c, (cmp_end[None, :] <= tpos[:, None])[None, :, None, None, :])
        o_c = jnp.einsum('bqhgn,bnhd->bqhgd', p_c.astype(vcmp.dtype), vcmp)
        imp = jnp.einsum('bqhgn,nj->bqhj', p_c, overlap)
        cur = (tpos // SLC_LEN)[None, :, None, None]
        j = jnp.arange(n_slc)
        forced = (j == 0) | (j == cur) | (j == cur - 1)
        imp = jnp.where(j <= cur, jnp.where(forced, jnp.inf, imp), -jnp.inf)
        _, sel = lax.top_k(imp, n_sel)
        k_sel = ks_blk[b_ix, h_ix, sel].reshape(B, QB, Hk, n_sel * SLC_LEN, dh)
        v_sel = vs_blk[b_ix, h_ix, sel].reshape(B, QB, Hk, n_sel * SLC_LEN, dh)
        key_pos = sel[..., None] * SLC_LEN + jnp.arange(SLC_LEN)
        m_s = (sel <= cur)[..., None] & (key_pos <= tpos[None, :, None, None, None])
        s_s = jnp.einsum('bqhgd,bqhmd->bqhgm', qb, k_sel)
        p_s = masked_softmax(s_s, m_s.reshape(B, QB, Hk, 1, n_sel * SLC_LEN))
        o_s = jnp.einsum('bqhgm,bqhmd->bqhgd', p_s.astype(v_sel.dtype), v_sel)
        kwb = lax.dynamic_slice_in_dim(kw_pad, t0, WINDOW + QB, 1)
        vwb = lax.dynamic_slice_in_dim(vw_pad, t0, WINDOW + QB, 1)
        kpos = t0 - WINDOW + jnp.arange(WINDOW + QB)
        m_w = (kpos[None, :] <= tpos[:, None]) & (kpos[None, :] > tpos[:, None] - WINDOW) & (kpos[None, :] >= 0)
        s_w = jnp.einsum('bqhgd,bkhd->bqhgk', qb, kwb)
        p_w = masked_softmax(s_w, m_w[None, :, None, None, :])
        o_w = jnp.einsum('bqhgk,bkhd->bqhgd', p_w.astype(vwb.dtype), vwb)
        return gb[..., 0:1] * o_c + gb[..., 1:2] * o_s + gb[..., 2:3] * o_w

    o = lax.map(block, jnp.arange(S // QB))
    o = jnp.moveaxis(o, 0, 1).reshape(B, S, NSA_DQ)
    return o @ w_out


def swiglu(h, w_gu, w_down):
    gate, up = jnp.split(h @ w_gu, 2, axis=-1)
    return (jax.nn.silu(gate) * up) @ w_down


def setup_inputs(seed: int = 0) -> dict:
    key = jax.random.key(seed)
    k = jax.random.split(key, 24)
    n_gdn = (DEPTH + N_MIXERS - 1) // N_MIXERS
    n_nsa = DEPTH // N_MIXERS

    def w(kk, shape, fan_in, gain=1.0):
        return jax.random.normal(kk, shape, jnp.float32) * (gain * fan_in ** -0.5)

    x = jax.random.normal(k[0], (BATCH, SEQ, D_MODEL), jnp.float32)
    c = jax.random.normal(k[1], (BATCH, D_MODEL), jnp.float32)
    start = jax.random.randint(k[2], (BATCH, 1), 0, POS_OFFSET_MAX, dtype=jnp.int32)
    positions = start + jnp.arange(SEQ, dtype=jnp.int32)[None, :]
    mod_w = w(k[3], (DEPTH, D_MODEL, 6 * D_MODEL), D_MODEL, 0.2)
    mod_b = 0.01 * jax.random.normal(k[4], (DEPTH, 6 * D_MODEL), jnp.float32)
    ln_g = 1.0 + 0.05 * jax.random.normal(k[5], (DEPTH, 2, D_MODEL), jnp.float32)
    ln_b = 0.01 * jax.random.normal(k[6], (DEPTH, 2, D_MODEL), jnp.float32)
    ffn_w_gu = w(k[7], (DEPTH, D_MODEL, 2 * D_FF), D_MODEL)
    ffn_w_down = w(k[8], (DEPTH, D_FF, D_MODEL), D_FF, BETA_DN)
    gdn_w_in = w(k[9], (n_gdn, D_MODEL, GDN_IN), D_MODEL)
    gdn_conv_w = w(k[10], (n_gdn, GDN_CONV, GDN_CONV_CH), GDN_CONV)
    gdn_a_log = jnp.log(jax.random.uniform(k[11], (n_gdn, GDN_V_HEADS), jnp.float32, 1.0, 16.0))
    dt = jnp.exp(jax.random.uniform(k[12], (n_gdn, GDN_V_HEADS), jnp.float32, math.log(1e-3), math.log(1e-1)))
    gdn_dt_bias = dt + jnp.log(-jnp.expm1(-dt))
    gdn_norm_w = 1.0 + 0.05 * jax.random.normal(k[13], (n_gdn, HEAD_DIM), jnp.float32)
    gdn_w_out = w(k[14], (n_gdn, GDN_DV, D_MODEL), GDN_DV, BETA_DN)
    nsa_w_in = w(k[15], (n_nsa, D_MODEL, NSA_IN), D_MODEL)
    nsa_cmp_pe = 0.1 * jax.random.normal(k[16], (n_nsa, 2, CMP_LEN, HEAD_DIM), jnp.float32)
    nsa_cmp_w1 = w(k[17], (n_nsa, 2, CMP_LEN * HEAD_DIM, HEAD_DIM), CMP_LEN * HEAD_DIM)
    nsa_cmp_w2 = w(k[18], (n_nsa, 2, HEAD_DIM, HEAD_DIM), HEAD_DIM)
    nsa_w_out = w(k[19], (n_nsa, NSA_DQ, D_MODEL), NSA_DQ, BETA_DN)
    return {'x': x, 'c': c, 'positions': positions, 'mod_w': mod_w, 'mod_b': mod_b,
            'ln_g': ln_g, 'ln_b': ln_b, 'ffn_w_gu': ffn_w_gu, 'ffn_w_down': ffn_w_down,
            'gdn_w_in': gdn_w_in, 'gdn_conv_w': gdn_conv_w, 'gdn_a_log': gdn_a_log,
            'gdn_dt_bias': gdn_dt_bias, 'gdn_norm_w': gdn_norm_w, 'gdn_w_out': gdn_w_out,
            'nsa_w_in': nsa_w_in, 'nsa_cmp_pe': nsa_cmp_pe, 'nsa_cmp_w1': nsa_cmp_w1,
            'nsa_cmp_w2': nsa_cmp_w2, 'nsa_w_out': nsa_w_out}


def reference(x, c, positions, mod_w, mod_b, ln_g, ln_b, ffn_w_gu, ffn_w_down,
              gdn_w_in, gdn_conv_w, gdn_a_log, gdn_dt_bias, gdn_norm_w, gdn_w_out,
              nsa_w_in, nsa_cmp_pe, nsa_cmp_w1, nsa_cmp_w2, nsa_w_out):
    cos, sin = rope_tables(positions)
    cond = jax.nn.silu(c)
    for i in range(DEPTH):
        mod = (cond @ mod_w[i] + mod_b[i])[:, None, :]
        sh1, sc1, ga1, sh2, sc2, ga2 = jnp.split(mod, 6, axis=-1)
        j = i // N_MIXERS
        h = x * (1 + sc1) + sh1
        if i % N_MIXERS == 0:
            y = gated_deltanet(h, gdn_w_in[j], gdn_conv_w[j], gdn_a_log[j], gdn_dt_bias[j],
                               gdn_norm_w[j], gdn_w_out[j])
        else:
            y = nsa_attention(h, cos, sin, nsa_w_in[j], nsa_cmp_pe[j], nsa_cmp_w1[j],
                              nsa_cmp_w2[j], nsa_w_out[j])
        x = layer_norm(ALPHA_DN * x + (1 + ga1) * y, ln_g[i, 0], ln_b[i, 0])
        h = x * (1 + sc2) + sh2
        y = swiglu(h, ffn_w_gu[i], ffn_w_down[i])
        x = layer_norm(ALPHA_DN * x + (1 + ga2) * y, ln_g[i, 1], ln_b[i, 1])
    return x
```

```python
import functools

import jax
import jax.numpy as jnp
from jax import lax
from jax.experimental import pallas as pl
from jax.experimental.pallas import tpu as pltpu

F32 = jnp.float32
BF16 = jnp.bfloat16
HIGHEST = lax.Precision.HIGHEST

HEAD_DIM = 128
N_MIXERS = 2
GDN_CONV = 4
GDN_CHUNK = 64
NSA_GROUP = 4
N_BRANCH = 3
CMP_LEN = 32
CMP_STRIDE = 16
SLC_LEN = 64
SLC_TOP = 16
WINDOW = 512
ROPE_THETA = 10000.0
LN_EPS = 1e-5
NORM_EPS = 1e-6

LANES = 128
VMEM_LIMIT_BYTES = 56 * 1024 * 1024
NEG_BIG = -1e30


def _params(*sem):
    return pltpu.CompilerParams(dimension_semantics=sem, vmem_limit_bytes=VMEM_LIMIT_BYTES)


def _silu(v):
    return v * jax.nn.sigmoid(v)


def _dot(a, b):
    return jnp.dot(a.astype(BF16), b.astype(BF16), preferred_element_type=F32)


def _dot_nt(a, b):
    return lax.dot_general(a.astype(BF16), b.astype(BF16), (((1,), (1,)), ((), ())),
                           preferred_element_type=F32)


def _split(a):
    hi = a.astype(BF16)
    lo = (a - hi.astype(F32)).astype(BF16)
    return hi, lo


def _dot3(a, b):
    ah, al = _split(a)
    bh, bl = _split(b)
    d = functools.partial(jnp.dot, preferred_element_type=F32)
    return d(ah, bh) + (d(ah, bl) + d(al, bh))


def _dot_sel(a, onehot):
    ah, al = _split(a)
    e = onehot.astype(BF16)
    d = functools.partial(jnp.dot, preferred_element_type=F32)
    return d(ah, e) + d(al, e)


def _sel_dot(onehot, b):
    bh, bl = _split(b)
    e = onehot.astype(BF16)
    d = functools.partial(jnp.dot, preferred_element_type=F32)
    return d(e, bh) + d(e, bl)


def _mod_kernel(c_ref, w_ref, b_ref, o_ref):
    cond = _silu(c_ref[...])
    o_ref[0] = _dot3(cond, w_ref[0]) + b_ref[0]


def _modulation(c, mod_w, mod_b):
    depth, d, n = mod_w.shape
    tn = 1536
    c8 = jnp.broadcast_to(c[:1], (8, d))
    out = pl.pallas_call(
        _mod_kernel,
        grid=(depth, n // tn),
        in_specs=[pl.BlockSpec((8, d), lambda l, j: (0, 0)),
                  pl.BlockSpec((1, d, tn), lambda l, j: (l, 0, j)),
                  pl.BlockSpec((1, 1, tn), lambda l, j: (l, 0, j))],
        out_specs=pl.BlockSpec((1, 8, tn), lambda l, j: (l, 0, j)),
        out_shape=jax.ShapeDtypeStruct((depth, 8, n), F32),
        compiler_params=_params("parallel", "parallel"),
        name="modulation",
    )(c8, mod_w, mod_b.reshape(depth, 1, n))
    return out[:, 0:1, :]


def _mod_matmul_kernel(x_ref, sc_ref, sh_ref, w_ref, o_ref, h_scr):
    @pl.when(pl.program_id(1) == 0)
    def _():
        h_scr[...] = (x_ref[...] * (1.0 + sc_ref[...]) + sh_ref[...]).astype(BF16)

    o_ref[...] = jnp.dot(h_scr[...], w_ref[...].astype(BF16),
                         preferred_element_type=F32).astype(o_ref.dtype)


def _mod_matmul(x, sc, sh, w, *, tm=512, tn=512, out_dtype=F32, name="mod_matmul"):
    m, k = x.shape
    n = w.shape[1]
    tn = min(tn, n)
    return pl.pallas_call(
        _mod_matmul_kernel,
        grid=(m // tm, n // tn),
        in_specs=[pl.BlockSpec((tm, k), lambda i, j: (i, 0)),
                  pl.BlockSpec((1, k), lambda i, j: (0, 0)),
                  pl.BlockSpec((1, k), lambda i, j: (0, 0)),
                  pl.BlockSpec((k, tn), lambda i, j: (0, j))],
        out_specs=pl.BlockSpec((tm, tn), lambda i, j: (i, j)),
        out_shape=jax.ShapeDtypeStruct((m, n), out_dtype),
        scratch_shapes=[pltpu.VMEM((tm, k), BF16)],
        compiler_params=_params("parallel", "arbitrary"),
        name=name,
    )(x, sc, sh, w)


def _swiglu_up_kernel(x_ref, sc_ref, sh_ref, wg_ref, wu_ref, o_ref, h_scr):
    @pl.when(pl.program_id(1) == 0)
    def _():
        h_scr[...] = (x_ref[...] * (1.0 + sc_ref[...]) + sh_ref[...]).astype(BF16)

    h = h_scr[...]
    g = jnp.dot(h, wg_ref[...].astype(BF16), preferred_element_type=F32)
    u = jnp.dot(h, wu_ref[...].astype(BF16), preferred_element_type=F32)
    o_ref[...] = (_silu(g) * u).astype(o_ref.dtype)


def _swiglu_up(x, sc, sh, w_gu, *, tm=512, tn=512):
    m, k = x.shape
    dff = w_gu.shape[1] // 2
    nb = dff // tn
    return pl.pallas_call(
        _swiglu_up_kernel,
        grid=(m // tm, nb),
        in_specs=[pl.BlockSpec((tm, k), lambda i, j: (i, 0)),
                  pl.BlockSpec((1, k), lambda i, j: (0, 0)),
                  pl.BlockSpec((1, k), lambda i, j: (0, 0)),
                  pl.BlockSpec((k, tn), lambda i, j: (0, j)),
                  pl.BlockSpec((k, tn), lambda i, j: (0, j + nb))],
        out_specs=pl.BlockSpec((tm, tn), lambda i, j: (i, j)),
        out_shape=jax.ShapeDtypeStruct((m, dff), BF16),
        scratch_shapes=[pltpu.VMEM((tm, k), BF16)],
        compiler_params=_params("parallel", "arbitrary"),
        name="swiglu_up",
    )(x, sc, sh, w_gu, w_gu)


def _matmul_ln_kernel(a_ref, w_ref, x_ref, ga_ref, g_ref, b_ref, o_ref, acc, *, alpha):
    kk = pl.program_id(1)

    @pl.when(kk == 0)
    def _():
        acc[...] = jnp.zeros_like(acc)

    acc[...] += jnp.dot(a_ref[...].astype(BF16), w_ref[...].astype(BF16),
                        preferred_element_type=F32)

    @pl.when(kk == pl.num_programs(1) - 1)
    def _():
        r = alpha * x_ref[...] + (1.0 + ga_ref[...]) * acc[...]
        mu = jnp.mean(r, axis=-1, keepdims=True)
        d = r - mu
        var = jnp.mean(d * d, axis=-1, keepdims=True)
        o_ref[...] = d * lax.rsqrt(var + LN_EPS) * g_ref[...] + b_ref[...]


def _matmul_ln(a, w, x, ga, g, b, *, alpha, tm=512, tk=512):
    m, k = a.shape
    n = w.shape[1]
    return pl.pallas_call(
        functools.partial(_matmul_ln_kernel, alpha=alpha),
        grid=(m // tm, k // tk),
        in_specs=[pl.BlockSpec((tm, tk), lambda i, kk: (i, kk)),
                  pl.BlockSpec((tk, n), lambda i, kk: (kk, 0)),
                  pl.BlockSpec((tm, n), lambda i, kk: (i, 0)),
                  pl.BlockSpec((1, n), lambda i, kk: (0, 0)),
                  pl.BlockSpec((1, n), lambda i, kk: (0, 0)),
                  pl.BlockSpec((1, n), lambda i, kk: (0, 0))],
        out_specs=pl.BlockSpec((tm, n), lambda i, kk: (i, 0)),
        out_shape=jax.ShapeDtypeStruct((m, n), F32),
        scratch_shapes=[pltpu.VMEM((tm, n), F32)],
        compiler_params=_params("parallel", "arbitrary"),
        name="matmul_ln",
    )(a, w, x, ga, g, b)


def _gdn_conv_kernel(xp_ref, xc_ref, w_ref, o_ref, buf, *, normalize, scale):
    ts, tc = xc_ref.shape
    buf[0:8, :] = jnp.where(pl.program_id(0) > 0, xp_ref[...], 0.0)
    buf[8:, :] = xc_ref[...]
    w = w_ref[...]
    y = w[0:1, :] * buf[5:5 + ts, :]
    for i in range(1, GDN_CONV):
        y = y + w[i:i + 1, :] * buf[5 + i:5 + i + ts, :]
    y = _silu(y)
    if normalize:
        for hh in range(tc // HEAD_DIM):
            seg = y[:, hh * HEAD_DIM:(hh + 1) * HEAD_DIM]
            ss = jnp.sum(seg * seg, axis=-1, keepdims=True)
            o_ref[:, hh * HEAD_DIM:(hh + 1) * HEAD_DIM] = seg * (lax.rsqrt(ss + NORM_EPS) * scale)
    else:
        o_ref[...] = y


def _gdn_conv(proj, conv_w, col0, ncols, *, normalize, scale, ts=256, tc=512):
    s = proj.shape[0]
    cb0 = col0 // tc
    hb = ts // 8
    return pl.pallas_call(
        functools.partial(_gdn_conv_kernel, normalize=normalize, scale=scale),
        grid=(s // ts, ncols // tc),
        in_specs=[pl.BlockSpec((8, tc), lambda i, j: (jnp.maximum(i * hb - 1, 0), j + cb0)),
                  pl.BlockSpec((ts, tc), lambda i, j: (i, j + cb0)),
                  pl.BlockSpec((GDN_CONV, tc), lambda i, j: (0, j + cb0))],
        out_specs=pl.BlockSpec((ts, tc), lambda i, j: (i, j)),
        out_shape=jax.ShapeDtypeStruct((s, ncols), F32),
        scratch_shapes=[pltpu.VMEM((ts + 8, tc), F32)],
        compiler_params=_params("parallel", "parallel"),
        name="gdn_conv",
    )(proj, proj, conv_w)


def _gdn_gates_kernel(ba_ref, alog_ref, dtb_ref, beta_ref, gc_ref, gl_ref):
    ts = ba_ref.shape[0]
    nh = ba_ref.shape[1] // 2
    x = ba_ref[...]
    beta_ref[...] = jax.nn.sigmoid(x[:, :nh])
    z = x[:, nh:] + dtb_ref[...]
    softplus = jnp.maximum(z, 0.0) + jnp.log1p(jnp.exp(-jnp.abs(z)))
    g = -jnp.exp(alog_ref[...]) * softplus
    ri = lax.broadcasted_iota(jnp.int32, (ts, ts), 0)
    ci = lax.broadcasted_iota(jnp.int32, (ts, ts), 1)
    same = (ri // GDN_CHUNK) == (ci // GDN_CHUNK)
    gc_ref[...] = _sel_dot(jnp.where(same & (ci <= ri), 1.0, 0.0), g)
    gl_ref[...] = _sel_dot(jnp.where(same, 1.0, 0.0), g)


def _gdn_gates(ba, a_log, dt_bias, *, ts=512):
    s, two_h = ba.shape
    nh = two_h // 2
    shp = jax.ShapeDtypeStruct((s, nh), F32)
    spec = pl.BlockSpec((ts, nh), lambda i: (i, 0))
    return pl.pallas_call(
        _gdn_gates_kernel,
        grid=(s // ts,),
        in_specs=[pl.BlockSpec((ts, two_h), lambda i: (i, 0)),
                  pl.BlockSpec((1, nh), lambda i: (0, 0)),
                  pl.BlockSpec((1, nh), lambda i: (0, 0))],
        out_specs=[spec, spec, spec],
        out_shape=[shp, shp, shp],
        compiler_params=_params("parallel"),
        name="gdn_gates",
    )(ba, a_log.reshape(1, nh), dt_bias.reshape(1, nh))


def _unit_lower_inverse(low, ri, ci):
    def blk(b):
        return (ri // b) == (ci // b)

    eye = jnp.where(ri == ci, 1.0, 0.0)
    n1 = jnp.where(blk(16), -low, 0.0)
    n2 = _dot3(n1, n1)
    n4 = _dot3(n2, n2)
    n8 = _dot3(n4, n4)
    t = eye + n1
    t = t + _dot3(n2, t)
    t = t + _dot3(n4, t)
    t = t + _dot3(n8, t)
    for b in (32, 64):
        off = jnp.where(blk(b) & jnp.logical_not(blk(b // 2)), low, 0.0)
        t = t - _dot3(t, _dot3(off, t))
    return t


def _gdn_chunk_kernel(q_ref, k_ref, v_ref, beta_ref, gc_ref, gl_ref, gct_ref, o_ref, state):
    hh = pl.program_id(0)
    rows = q_ref.shape[0]
    nh = beta_ref.shape[1]
    nchunk = rows // GDN_CHUNK

    @pl.when(pl.program_id(1) == 0)
    def _():
        state[...] = jnp.zeros_like(state)

    pick = jnp.where(lax.broadcasted_iota(jnp.int32, (nh, HEAD_DIM), 0) == hh, 1.0, 0.0)
    pick_t = jnp.where(lax.broadcasted_iota(jnp.int32, (rows, nh), 1) == hh, 1.0, 0.0)
    beta = _dot_sel(beta_ref[...], pick)
    gcol = _dot_sel(gc_ref[...], pick)
    glast = _dot_sel(gl_ref[...], pick)
    g_j = _sel_dot(pick_t, gct_ref[...])
    g_i = jnp.concatenate([gcol] * (rows // HEAD_DIM), axis=1)

    ri = lax.broadcasted_iota(jnp.int32, (rows, rows), 0)
    ci = lax.broadcasted_iota(jnp.int32, (rows, rows), 1)
    same = (ri // GDN_CHUNK) == (ci // GDN_CHUNK)
    lower = same & (ci <= ri)
    strict = same & (ci < ri)
    decay = jnp.where(lower, jnp.exp(jnp.where(lower, g_i - g_j, 0.0)), 0.0)

    q = q_ref[...]
    k = k_ref[...]
    v = v_ref[...]
    kb = k * beta
    low = jnp.where(strict, _dot_nt(kb, k) * decay, 0.0)
    t = _unit_lower_inverse(low, ri, ci)
    u = _dot3(t, v * beta)
    w = _dot3(t, kb * jnp.exp(gcol))
    a_qk = jnp.where(lower, _dot_nt(q, k) * decay, 0.0)
    q_eff = q * jnp.exp(gcol) - _dot(a_qk, w)
    a_u = _dot(a_qk, u)
    k_g = k * jnp.exp(glast - gcol)

    s = state[...]
    for c in range(nchunk):
        r0 = c * GDN_CHUNK
        sl = slice(r0, r0 + GDN_CHUNK)
        o_ref[sl, :] = _dot(q_eff[sl], s) + a_u[sl]
        v_new = u[sl] - _dot(w[sl], s)
        s = s * jnp.exp(glast[r0:r0 + 1, :]) + lax.dot_general(
            k_g[sl].astype(BF16), v_new.astype(BF16), (((0,), (0,)), ((), ())),
            preferred_element_type=F32)
    state[...] = s


def _gdn_chunk(q, k, v, beta, gc, gl, gct, *, rows=256):
    s, dv_total = v.shape
    nh = dv_total // HEAD_DIM
    rep = nh // (q.shape[1] // HEAD_DIM)
    gspec = pl.BlockSpec((rows, nh), lambda h, i: (i, 0))
    return pl.pallas_call(
        _gdn_chunk_kernel,
        grid=(nh, s // rows),
        in_specs=[pl.BlockSpec((rows, HEAD_DIM), lambda h, i: (i, h // rep)),
                  pl.BlockSpec((rows, HEAD_DIM), lambda h, i: (i, h // rep)),
                  pl.BlockSpec((rows, HEAD_DIM), lambda h, i: (i, h)),
                  gspec, gspec, gspec,
                  pl.BlockSpec((nh, rows), lambda h, i: (0, i))],
        out_specs=pl.BlockSpec((rows, HEAD_DIM), lambda h, i: (i, h)),
        out_shape=jax.ShapeDtypeStruct((s, dv_total), F32),
        scratch_shapes=[pltpu.VMEM((HEAD_DIM, HEAD_DIM), F32)],
        compiler_params=_params("parallel", "arbitrary"),
        name="gdn_chunk",
    )(q, k, v, beta, gc, gl, gct)


def _gdn_post_kernel(o_ref, z_ref, nw_ref, out_ref):
    nw = nw_ref[...]
    for hh in range(o_ref.shape[1] // HEAD_DIM):
        sl = slice(hh * HEAD_DIM, (hh + 1) * HEAD_DIM)
        o = o_ref[:, sl]
        ms = jnp.mean(o * o, axis=-1, keepdims=True)
        out_ref[:, sl] = ((o * lax.rsqrt(ms + NORM_EPS)) * nw * _silu(z_ref[:, sl])).astype(out_ref.dtype)


def _gdn_post(o, proj, z_col0, norm_w, *, ts=256):
    s, dv = o.shape
    zb = z_col0 // dv
    return pl.pallas_call(
        _gdn_post_kernel,
        grid=(s // ts,),
        in_specs=[pl.BlockSpec((ts, dv), lambda i: (i, 0)),
                  pl.BlockSpec((ts, dv), lambda i: (i, zb)),
                  pl.BlockSpec((1, HEAD_DIM), lambda i: (0, 0))],
        out_specs=pl.BlockSpec((ts, dv), lambda i: (i, 0)),
        out_shape=jax.ShapeDtypeStruct((s, dv), BF16),
        compiler_params=_params("parallel"),
        name="gdn_post",
    )(o, proj, norm_w.reshape(1, HEAD_DIM))


def _gated_deltanet(x, sc, sh, w_in, conv_w, a_log, dt_bias, norm_w):
    d = x.shape[1]
    nvh = a_log.shape[0]
    dv = nvh * HEAD_DIM
    dqk = (conv_w.shape[1] - dv) // 2
    n_main = conv_w.shape[1] + dv
    proj = _mod_matmul(x, sc, sh, w_in[:, :n_main], name="gdn_in_proj")
    ba = _mod_matmul(x, sc, sh, w_in[:, n_main:], name="gdn_in_gates")
    q = _gdn_conv(proj, conv_w, 0, dqk, normalize=True, scale=HEAD_DIM ** -0.5)
    k = _gdn_conv(proj, conv_w, dqk, dqk, normalize=True, scale=1.0)
    v = _gdn_conv(proj, conv_w, 2 * dqk, dv, normalize=False, scale=1.0)
    beta, gc, gl = _gdn_gates(ba, a_log, dt_bias)
    o = _gdn_chunk(q, k, v, beta, gc, gl, gc.T)
    return _gdn_post(o, proj, conv_w.shape[1], norm_w)


def _rope(x, c, sg):
    return x * c + pltpu.roll(x, HEAD_DIM // 2, 1) * sg


def _nsa_prep_kernel(p_ref, c_ref, s_ref, q_ref, kc_ref, ks_ref, vs_ref, kw_ref, vw_ref, *, dq, dkv):
    c = c_ref[...]
    sg = s_ref[...]
    scale = HEAD_DIM ** -0.5

    def heads(col0, n, out_ref, rope, mul):
        for hh in range(n // HEAD_DIM):
            x = p_ref[:, col0 + hh * HEAD_DIM:col0 + (hh + 1) * HEAD_DIM]
            if rope:
                x = _rope(x, c, sg)
            if mul != 1.0:
                x = x * mul
            out_ref[:, hh * HEAD_DIM:(hh + 1) * HEAD_DIM] = x.astype(out_ref.dtype)

    heads(0, dq, q_ref, True, scale)
    heads(dq, dkv, kc_ref, True, 1.0)
    heads(dq + 2 * dkv, dkv, ks_ref, True, 1.0)
    heads(dq + 3 * dkv, dkv, vs_ref, False, 1.0)
    heads(dq + 4 * dkv, dkv, kw_ref, True, 1.0)
    heads(dq + 5 * dkv, dkv, vw_ref, False, 1.0)


def _nsa_prep(proj, cos2, sin2, dq, dkv, *, ts=256):
    s, n = proj.shape
    kv_spec = pl.BlockSpec((ts, dkv), lambda i: (i, 0))
    tab = pl.BlockSpec((ts, HEAD_DIM), lambda i: (i, 0))
    kv16 = jax.ShapeDtypeStruct((s, dkv), BF16)
    return pl.pallas_call(
        functools.partial(_nsa_prep_kernel, dq=dq, dkv=dkv),
        grid=(s // ts,),
        in_specs=[pl.BlockSpec((ts, n), lambda i: (i, 0)), tab, tab],
        out_specs=[pl.BlockSpec((ts, dq), lambda i: (i, 0)), kv_spec, kv_spec, kv_spec, kv_spec, kv_spec],
        out_shape=[jax.ShapeDtypeStruct((s, dq), BF16), jax.ShapeDtypeStruct((s, dkv), F32),
                   kv16, kv16, kv16, kv16],
        compiler_params=_params("parallel"),
        name="nsa_prep",
    )(proj, cos2, sin2)


def _nsa_compress_kernel(x_ref, pe_ref, w1_ref, w2_ref, o_ref):
    x = x_ref[0, 0]
    n = x.shape[0]
    a = _dot(x + pe_ref[0, 0:1, :], w1_ref[0, 0])
    b = _dot(x + pe_ref[0, 1:2, :], w1_ref[0, 1])
    y = a + pltpu.roll(b, n - 1, 0)
    o_ref[0, 0] = _dot(_silu(y), w2_ref[0])


def _nsa_compress(xk, xv, pe, w1, w2):
    hk, ng, gw = xk.shape
    half = CMP_LEN // 2
    x = jnp.stack([xk, xv])
    pe2 = pe.reshape(2, 2, half * HEAD_DIM)
    w1r = w1.reshape(2, 2, half * HEAD_DIM, HEAD_DIM)
    return pl.pallas_call(
        _nsa_compress_kernel,
        grid=(2, hk),
        in_specs=[pl.BlockSpec((1, 1, ng, gw), lambda t, h: (t, h, 0, 0)),
                  pl.BlockSpec((1, 2, gw), lambda t, h: (t, 0, 0)),
                  pl.BlockSpec((1, 2, gw, HEAD_DIM), lambda t, h: (t, 0, 0, 0)),
                  pl.BlockSpec((1, HEAD_DIM, HEAD_DIM), lambda t, h: (t, 0, 0))],
        out_specs=pl.BlockSpec((1, 1, ng, HEAD_DIM), lambda t, h: (t, h, 0, 0)),
        out_shape=jax.ShapeDtypeStruct((2, hk, ng, HEAD_DIM), F32),
        compiler_params=_params("parallel", "parallel"),
        name="nsa_compress",
    )(x, pe2, w1r, w2)


def _nsa_cmp_select_kernel(q_ref, kc_ref, vc_ref, ov_ref, o_ref, sel_ref):
    tq = q_ref.shape[0]
    nc = kc_ref.shape[2]
    nbp = ov_ref.shape[1]
    t0 = pl.program_id(0) * tq
    tpos = t0 + lax.broadcasted_iota(jnp.int32, (tq, nc), 0)
    nidx = lax.broadcasted_iota(jnp.int32, (tq, nc), 1)
    valid = (nidx * CMP_STRIDE + (CMP_LEN - 1)) <= tpos
    kc = kc_ref[0, 0].astype(BF16)
    vc = vc_ref[0, 0].astype(BF16)
    psum = jnp.zeros((tq, nc), F32)
    for g in range(NSA_GROUP):
        sl = slice(g * HEAD_DIM, (g + 1) * HEAD_DIM)
        s = jnp.where(valid, _dot_nt(q_ref[:, sl], kc), NEG_BIG)
        m = jnp.max(s, axis=-1, keepdims=True)
        p = jnp.where(valid, jnp.exp(s - m), 0.0)
        den = jnp.sum(p, axis=-1, keepdims=True)
        p = p / jnp.where(den > 0.0, den, 1.0)
        o_ref[:, sl] = jnp.dot(p.astype(BF16), vc, preferred_element_type=F32)
        psum = psum + p
    imp = _dot_sel(psum, ov_ref[...])
    j = lax.broadcasted_iota(jnp.int32, (tq, nbp), 1)
    cur = (t0 + lax.broadcasted_iota(jnp.int32, (tq, nbp), 0)) // SLC_LEN
    forced = (j == 0) | (j == cur) | (j == cur - 1)
    work = jnp.where(j <= cur, jnp.where(forced, jnp.inf, imp), -jnp.inf)
    sel = jnp.zeros((tq, nbp), F32)
    for _ in range(SLC_TOP):
        m = jnp.max(work, axis=-1, keepdims=True)
        first = jnp.min(jnp.where(work == m, j, nbp), axis=-1, keepdims=True)
        hit = j == first
        sel = jnp.where(hit, 1.0, sel)
        work = jnp.where(hit, -jnp.inf, work)
    sel_ref[0] = jnp.where(j <= cur, sel, 0.0).astype(sel_ref.dtype)


def _nsa_cmp_select(q, cmp, ov, *, tq=128):
    s, dq = q.shape
    _, hk, nc, _ = cmp.shape
    nbp = ov.shape[1]
    gw = NSA_GROUP * HEAD_DIM
    return pl.pallas_call(
        _nsa_cmp_select_kernel,
        grid=(s // tq, hk),
        in_specs=[pl.BlockSpec((tq, gw), lambda i, h: (i, h)),
                  pl.BlockSpec((1, 1, nc, HEAD_DIM), lambda i, h: (0, h, 0, 0)),
                  pl.BlockSpec((1, 1, nc, HEAD_DIM), lambda i, h: (1, h, 0, 0)),
                  pl.BlockSpec((nc, nbp), lambda i, h: (0, 0))],
        out_specs=[pl.BlockSpec((tq, gw), lambda i, h: (i, h)),
                   pl.BlockSpec((1, tq, nbp), lambda i, h: (h, i, 0))],
        out_shape=[jax.ShapeDtypeStruct((s, dq), F32),
                   jax.ShapeDtypeStruct((hk, s, nbp), BF16)],
        compiler_params=_params("parallel", "parallel"),
        name="nsa_cmp_select",
    )(q, cmp, cmp, ov)


def _masked_attn_kernel(*refs, tq, tk, nkv, selected):
    if selected:
        q_ref, k_ref, v_ref, sel_ref, o_ref, m_scr, l_scr, acc = refs
    else:
        q_ref, k_ref, v_ref, o_ref, m_scr, l_scr, acc = refs
    i = pl.program_id(0)
    j = pl.program_id(2)
    if selected:
        kb = j
        active = kb * tk <= i * tq + tq - 1
    else:
        kb = (i * tq) // tk - (nkv - 1) + j
        active = kb >= 0

    @pl.when(j == 0)
    def _():
        m_scr[...] = jnp.full_like(m_scr, NEG_BIG)
        l_scr[...] = jnp.zeros_like(l_scr)
        acc[...] = jnp.zeros_like(acc)

    @pl.when(active)
    def _():
        q4 = jnp.concatenate([q_ref[:, g * HEAD_DIM:(g + 1) * HEAD_DIM] for g in range(NSA_GROUP)], axis=0)
        s = _dot_nt(q4, k_ref[...])
        tpos = i * tq + lax.broadcasted_iota(jnp.int32, (tq, tk), 0)
        kpos = kb * tk + lax.broadcasted_iota(jnp.int32, (tq, tk), 1)
        if selected:
            nbp = sel_ref.shape[2]
            expand = jnp.where(lax.broadcasted_iota(jnp.int32, (nbp, tk), 0)
                               == (kb * tk + lax.broadcasted_iota(jnp.int32, (nbp, tk), 1)) // SLC_LEN, 1.0, 0.0)
            picked = jnp.dot(sel_ref[0], expand.astype(BF16), preferred_element_type=F32)
            mask = (picked > 0.5) & (kpos <= tpos)
        else:
            mask = (kpos <= tpos) & (kpos > tpos - WINDOW)
        mask4 = jnp.concatenate([mask] * NSA_GROUP, axis=0)
        s = jnp.where(mask4, s, NEG_BIG)
        m_prev = m_scr[:, 0:1]
        m_new = jnp.maximum(m_prev, jnp.max(s, axis=-1, keepdims=True))
        alpha = jnp.exp(m_prev - m_new)
        p = jnp.where(mask4, jnp.exp(s - m_new), 0.0)
        l_scr[...] = jnp.broadcast_to(alpha * l_scr[:, 0:1] + jnp.sum(p, axis=-1, keepdims=True), l_scr.shape)
        acc[...] = alpha * acc[...] + jnp.dot(p.astype(BF16), v_ref[...], preferred_element_type=F32)
        m_scr[...] = jnp.broadcast_to(m_new, m_scr.shape)

    @pl.when(j == nkv - 1)
    def _():
        l = l_scr[:, 0:1]
        out = acc[...] / jnp.where(l > 0.0, l, 1.0)
        for g in range(NSA_GROUP):
            o_ref[:, g * HEAD_DIM:(g + 1) * HEAD_DIM] = out[g * tq:(g + 1) * tq, :]


def _masked_attn(q, k, v, sel, *, tq, tk):
    s, dq = q.shape
    hk = k.shape[1] // HEAD_DIM
    gw = NSA_GROUP * HEAD_DIM
    selected = sel is not None
    if selected:
        nkv = s // tk

        def kv_map(i, h, j):
            return (jnp.minimum(j, (i * tq + tq - 1) // tk), h)
    else:
        assert tq == tk
        nkv = -(-(WINDOW - 1) // tk) + 1

        def kv_map(i, h, j):
            return (jnp.maximum(i - (nkv - 1) + j, 0), h)

    in_specs = [pl.BlockSpec((tq, gw), lambda i, h, j: (i, h)),
                pl.BlockSpec((tk, HEAD_DIM), kv_map),
                pl.BlockSpec((tk, HEAD_DIM), kv_map)]
    args = [q, k, v]
    if selected:
        in_specs.append(pl.BlockSpec((1, tq, sel.shape[2]), lambda i, h, j: (h, i, 0)))
        args.append(sel)
    rows = NSA_GROUP * tq
    return pl.pallas_call(
        functools.partial(_masked_attn_kernel, tq=tq, tk=tk, nkv=nkv, selected=selected),
        grid=(s // tq, hk, nkv),
        in_specs=in_specs,
        out_specs=pl.BlockSpec((tq, gw), lambda i, h, j: (i, h)),
        out_shape=jax.ShapeDtypeStruct((s, dq), F32),
        scratch_shapes=[pltpu.VMEM((rows, LANES), F32), pltpu.VMEM((rows, LANES), F32),
                        pltpu.VMEM((rows, HEAD_DIM), F32)],
        compiler_params=_params("parallel", "parallel", "arbitrary"),
        name="nsa_selected_attn" if selected else "nsa_window_attn",
    )(*args)


def _nsa_combine_kernel(oc_ref, os_ref, ow_ref, g_ref, out_ref):
    gates = jax.nn.sigmoid(g_ref[...])
    ng, dq = g_ref.shape[1], oc_ref.shape[1]
    row = lax.broadcasted_iota(jnp.int32, (ng, dq), 0)
    head3 = (lax.broadcasted_iota(jnp.int32, (ng, dq), 1) // HEAD_DIM) * N_BRANCH
    acc = None
    for b, ref in enumerate((oc_ref, os_ref, ow_ref)):
        gb = _dot_sel(gates, jnp.where(row == head3 + b, 1.0, 0.0))
        term = gb * ref[...]
        acc = term if acc is None else acc + term
    out_ref[...] = acc.astype(out_ref.dtype)


def _nsa_combine(oc, osel, ow, gates, *, ts=256):
    s, dq = oc.shape
    spec = pl.BlockSpec((ts, dq), lambda i: (i, 0))
    return pl.pallas_call(
        _nsa_combine_kernel,
        grid=(s // ts,),
        in_specs=[spec, spec, spec, pl.BlockSpec((ts, gates.shape[1]), lambda i: (i, 0))],
        out_specs=spec,
        out_shape=jax.ShapeDtypeStruct((s, dq), BF16),
        compiler_params=_params("parallel"),
        name="nsa_combine",
    )(oc, osel, ow, gates)


def _nsa_attention(x, sc, sh, cos2, sin2, w_in, cmp_pe, cmp_w1, cmp_w2):
    s, d = x.shape
    dq = d
    hk = d // HEAD_DIM // NSA_GROUP
    dkv = hk * HEAD_DIM
    n_main = dq + 6 * dkv
    ngate = w_in.shape[1] - n_main
    proj = _mod_matmul(x, sc, sh, w_in[:, :n_main], name="nsa_in_proj")
    w_gate = jnp.pad(w_in[:, n_main:], ((0, 0), (0, LANES - ngate)))
    gates = _mod_matmul(x, sc, sh, w_gate, name="nsa_in_gates")
    q, kc, ks, vs, kw, vw = _nsa_prep(proj, cos2, sin2, dq, dkv)
    vc = proj[:, dq + dkv:dq + 2 * dkv]

    def groups(t):
        return t.reshape(s // CMP_STRIDE, CMP_STRIDE, hk, HEAD_DIM).transpose(2, 0, 1, 3).reshape(
            hk, s // CMP_STRIDE, CMP_STRIDE * HEAD_DIM)

    cmp = _nsa_compress(groups(kc), groups(vc), cmp_pe, cmp_w1, cmp_w2)

    nc = s // CMP_STRIDE
    n_slc = s // SLC_LEN
    nbp = -(-n_slc // LANES) * LANES
    c_start = CMP_STRIDE * jnp.arange(nc)
    s_start = SLC_LEN * jnp.arange(nbp)
    overlap = jnp.clip(jnp.minimum(c_start[:, None] + CMP_LEN, s_start[None, :] + SLC_LEN)
                       - jnp.maximum(c_start[:, None], s_start[None, :]), 0, None).astype(F32) / CMP_LEN
    o_c, sel = _nsa_cmp_select(q, cmp, overlap.astype(BF16))
    o_s = _masked_attn(q, ks, vs, sel, tq=128, tk=min(512, s))
    o_w = _masked_attn(q, kw, vw, None, tq=256, tk=256)
    return _nsa_combine(o_c, o_s, o_w, gates)


def kernel(x, c, positions, mod_w, mod_b, ln_g, ln_b, ffn_w_gu, ffn_w_down, gdn_w_in, gdn_conv_w,
           gdn_a_log, gdn_dt_bias, gdn_norm_w, gdn_w_out, nsa_w_in, nsa_cmp_pe, nsa_cmp_w1,
           nsa_cmp_w2, nsa_w_out):
    bsz, s, d = x.shape
    assert bsz == 1
    depth = mod_w.shape[0]
    alpha = (2 * depth) ** 0.25

    inv = ROPE_THETA ** (-jnp.arange(0, HEAD_DIM, 2, dtype=F32) / HEAD_DIM)
    ang = positions[0].astype(F32)[:, None] * inv
    cos, sin = jnp.cos(ang), jnp.sin(ang)
    cos2 = jnp.concatenate([cos, cos], axis=-1)
    sin2 = jnp.concatenate([-sin, sin], axis=-1)

    mod = _modulation(c, mod_w, mod_b)
    xs = x[0]
    for i in range(depth):
        sh1, sc1, ga1, sh2, sc2, ga2 = [mod[i, :, r * d:(r + 1) * d] for r in range(6)]
        j = i // N_MIXERS
        if i % N_MIXERS == 0:
            y = _gated_deltanet(xs, sc1, sh1, gdn_w_in[j], gdn_conv_w[j], gdn_a_log[j],
                                gdn_dt_bias[j], gdn_norm_w[j])
            w_out = gdn_w_out[j]
        else:
            y = _nsa_attention(xs, sc1, sh1, cos2, sin2, nsa_w_in[j], nsa_cmp_pe[j],
                               nsa_cmp_w1[j], nsa_cmp_w2[j])
            w_out = nsa_w_out[j]
        xs = _matmul_ln(y, w_out, xs, ga1, ln_g[i, 0:1], ln_b[i, 0:1], alpha=alpha)
        a = _swiglu_up(xs, sc2, sh2, ffn_w_gu[i])
        xs = _matmul_ln(a, ffn_w_down[i], xs, ga2, ln_g[i, 1:2], ln_b[i, 1:2], alpha=alpha)
    return xs[None]
```

```python
import functools

import jax
import jax.numpy as jnp
from jax import lax
from jax.experimental import pallas as pl
from jax.experimental.pallas import tpu as pltpu

F32 = jnp.float32
BF16 = jnp.bfloat16
HIGHEST = lax.Precision.HIGHEST

HEAD_DIM = 128
N_MIXERS = 2
GDN_CONV = 4
GDN_BLOCK = 128
NSA_GROUP = 4
N_BRANCH = 3
CMP_LEN = 32
CMP_STRIDE = 16
SLC_LEN = 64
SLC_TOP = 16
WINDOW = 512
ROPE_THETA = 10000.0
LN_EPS = 1e-5
NORM_EPS = 1e-6

LANES = 128
VMEM_LIMIT_BYTES = 56 * 1024 * 1024
NEG_BIG = -1e30


def _params(*sem):
    return pltpu.CompilerParams(dimension_semantics=sem, vmem_limit_bytes=VMEM_LIMIT_BYTES)


def _silu(v):
    return v * jax.nn.sigmoid(v)


def _dot(a, b):
    return jnp.dot(a.astype(BF16), b.astype(BF16), preferred_element_type=F32)


def _dot_nt(a, b):
    return lax.dot_general(a.astype(BF16), b.astype(BF16), (((1,), (1,)), ((), ())),
                           preferred_element_type=F32)


def _split(a):
    hi = a.astype(BF16)
    lo = (a - hi.astype(F32)).astype(BF16)
    return hi, lo


def _dot3(a, b):
    ah, al = _split(a)
    bh, bl = _split(b)
    d = functools.partial(jnp.dot, preferred_element_type=F32)
    return d(ah, bh) + (d(ah, bl) + d(al, bh))


def _dot_sel(a, onehot):
    ah, al = _split(a)
    e = onehot.astype(BF16)
    d = functools.partial(jnp.dot, preferred_element_type=F32)
    return d(ah, e) + d(al, e)


def _sel_dot(onehot, b):
    bh, bl = _split(b)
    e = onehot.astype(BF16)
    d = functools.partial(jnp.dot, preferred_element_type=F32)
    return d(e, bh) + d(e, bl)


def _mod_kernel(c_ref, w_ref, b_ref, o_ref):
    cond = _silu(c_ref[...])
    o_ref[0] = _dot3(cond, w_ref[0]) + b_ref[0]


def _modulation(c, mod_w, mod_b):
    depth, d, n = mod_w.shape
    tn = 1536
    c8 = jnp.broadcast_to(c[:1], (8, d))
    out = pl.pallas_call(
        _mod_kernel,
        grid=(depth, n // tn),
        in_specs=[pl.BlockSpec((8, d), lambda l, j: (0, 0)),
                  pl.BlockSpec((1, d, tn), lambda l, j: (l, 0, j)),
                  pl.BlockSpec((1, 1, tn), lambda l, j: (l, 0, j))],
        out_specs=pl.BlockSpec((1, 8, tn), lambda l, j: (l, 0, j)),
        out_shape=jax.ShapeDtypeStruct((depth, 8, n), F32),
        compiler_params=_params("parallel", "parallel"),
        name="modulation",
    )(c8, mod_w, mod_b.reshape(depth, 1, n))
    return out[:, 0:1, :]


def _mod_matmul_kernel(x_ref, sc_ref, sh_ref, w_ref, o_ref, h_scr):
    @pl.when(pl.program_id(1) == 0)
    def _():
        h_scr[...] = (x_ref[...] * (1.0 + sc_ref[...]) + sh_ref[...]).astype(BF16)

    o_ref[...] = jnp.dot(h_scr[...], w_ref[...].astype(BF16),
                         preferred_element_type=F32).astype(o_ref.dtype)


def _mod_matmul(x, sc, sh, w, *, tm=512, tn=512, out_dtype=F32, name="mod_matmul"):
    m, k = x.shape
    n = w.shape[1]
    tn = min(tn, n)
    return pl.pallas_call(
        _mod_matmul_kernel,
        grid=(m // tm, n // tn),
        in_specs=[pl.BlockSpec((tm, k), lambda i, j: (i, 0)),
                  pl.BlockSpec((1, k), lambda i, j: (0, 0)),
                  pl.BlockSpec((1, k), lambda i, j: (0, 0)),
                  pl.BlockSpec((k, tn), lambda i, j: (0, j))],
        out_specs=pl.BlockSpec((tm, tn), lambda i, j: (i, j)),
        out_shape=jax.ShapeDtypeStruct((m, n), out_dtype),
        scratch_shapes=[pltpu.VMEM((tm, k), BF16)],
        compiler_params=_params("parallel", "arbitrary"),
        name=name,
    )(x, sc, sh, w)


def _swiglu_up_kernel(x_ref, sc_ref, sh_ref, wg_ref, wu_ref, o_ref, h_scr):
    @pl.when(pl.program_id(1) == 0)
    def _():
        h_scr[...] = (x_ref[...] * (1.0 + sc_ref[...]) + sh_ref[...]).astype(BF16)

    h = h_scr[...]
    g = jnp.dot(h, wg_ref[...].astype(BF16), preferred_element_type=F32)
    u = jnp.dot(h, wu_ref[...].astype(BF16), preferred_element_type=F32)
    o_ref[...] = (_silu(g) * u).astype(o_ref.dtype)


def _swiglu_up(x, sc, sh, w_gu, *, tm=512, tn=512):
    m, k = x.shape
    dff = w_gu.shape[1] // 2
    nb = dff // tn
    return pl.pallas_call(
        _swiglu_up_kernel,
        grid=(m // tm, nb),
        in_specs=[pl.BlockSpec((tm, k), lambda i, j: (i, 0)),
                  pl.BlockSpec((1, k), lambda i, j: (0, 0)),
                  pl.BlockSpec((1, k), lambda i, j: (0, 0)),
                  pl.BlockSpec((k, tn), lambda i, j: (0, j)),
                  pl.BlockSpec((k, tn), lambda i, j: (0, j + nb))],
        out_specs=pl.BlockSpec((tm, tn), lambda i, j: (i, j)),
        out_shape=jax.ShapeDtypeStruct((m, dff), BF16),
        scratch_shapes=[pltpu.VMEM((tm, k), BF16)],
        compiler_params=_params("parallel", "arbitrary"),
        name="swiglu_up",
    )(x, sc, sh, w_gu, w_gu)


def _matmul_ln_kernel(a_ref, w_ref, x_ref, ga_ref, g_ref, b_ref, o_ref, acc, *, alpha):
    kk = pl.program_id(1)

    @pl.when(kk == 0)
    def _():
        acc[...] = jnp.zeros_like(acc)

    acc[...] += jnp.dot(a_ref[...].astype(BF16), w_ref[...].astype(BF16),
                        preferred_element_type=F32)

    @pl.when(kk == pl.num_programs(1) - 1)
    def _():
        r = alpha * x_ref[...] + (1.0 + ga_ref[...]) * acc[...]
        mu = jnp.mean(r, axis=-1, keepdims=True)
        d = r - mu
        var = jnp.mean(d * d, axis=-1, keepdims=True)
        o_ref[...] = d * lax.rsqrt(var + LN_EPS) * g_ref[...] + b_ref[...]


def _matmul_ln(a, w, x, ga, g, b, *, alpha, tm=512, tk=512):
    m, k = a.shape
    n = w.shape[1]
    return pl.pallas_call(
        functools.partial(_matmul_ln_kernel, alpha=alpha),
        grid=(m // tm, k // tk),
        in_specs=[pl.BlockSpec((tm, tk), lambda i, kk: (i, kk)),
                  pl.BlockSpec((tk, n), lambda i, kk: (kk, 0)),
                  pl.BlockSpec((tm, n), lambda i, kk: (i, 0)),
                  pl.BlockSpec((1, n), lambda i, kk: (0, 0)),
                  pl.BlockSpec((1, n), lambda i, kk: (0, 0)),
                  pl.BlockSpec((1, n), lambda i, kk: (0, 0))],
        out_specs=pl.BlockSpec((tm, n), lambda i, kk: (i, 0)),
        out_shape=jax.ShapeDtypeStruct((m, n), F32),
        scratch_shapes=[pltpu.VMEM((tm, n), F32)],
        compiler_params=_params("parallel", "arbitrary"),
        name="matmul_ln",
    )(a, w, x, ga, g, b)


def _gdn_conv_kernel(xp_ref, xc_ref, w_ref, o_ref, buf, *, normalize, scale):
    ts, tc = xc_ref.shape
    buf[0:8, :] = jnp.where(pl.program_id(0) > 0, xp_ref[...], 0.0)
    buf[8:, :] = xc_ref[...]
    w = w_ref[...]
    y = w[0:1, :] * buf[5:5 + ts, :]
    for i in range(1, GDN_CONV):
        y = y + w[i:i + 1, :] * buf[5 + i:5 + i + ts, :]
    y = _silu(y)
    if normalize:
        for hh in range(tc // HEAD_DIM):
            seg = y[:, hh * HEAD_DIM:(hh + 1) * HEAD_DIM]
            ss = jnp.sum(seg * seg, axis=-1, keepdims=True)
            o_ref[:, hh * HEAD_DIM:(hh + 1) * HEAD_DIM] = seg * (lax.rsqrt(ss + NORM_EPS) * scale)
    else:
        o_ref[...] = y


def _gdn_conv(proj, conv_w, col0, ncols, *, normalize, scale, ts=256, tc=512):
    s = proj.shape[0]
    cb0 = col0 // tc
    hb = ts // 8
    return pl.pallas_call(
        functools.partial(_gdn_conv_kernel, normalize=normalize, scale=scale),
        grid=(s // ts, ncols // tc),
        in_specs=[pl.BlockSpec((8, tc), lambda i, j: (jnp.maximum(i * hb - 1, 0), j + cb0)),
                  pl.BlockSpec((ts, tc), lambda i, j: (i, j + cb0)),
                  pl.BlockSpec((GDN_CONV, tc), lambda i, j: (0, j + cb0))],
        out_specs=pl.BlockSpec((ts, tc), lambda i, j: (i, j)),
        out_shape=jax.ShapeDtypeStruct((s, ncols), F32),
        scratch_shapes=[pltpu.VMEM((ts + 8, tc), F32)],
        compiler_params=_params("parallel", "parallel"),
        name="gdn_conv",
    )(proj, proj, conv_w)


def _gdn_gates_kernel(ba_ref, alog_ref, dtb_ref, out_ref):
    ts = ba_ref.shape[0]
    nh = ba_ref.shape[1] // 2
    x = ba_ref[...]
    beta = jax.nn.sigmoid(x[:, :nh])
    z = x[:, nh:] + dtb_ref[...]
    softplus = jnp.maximum(z, 0.0) + jnp.log1p(jnp.exp(-jnp.abs(z)))
    g = -jnp.exp(alog_ref[...]) * softplus
    ri = lax.broadcasted_iota(jnp.int32, (ts, ts), 0)
    ci = lax.broadcasted_iota(jnp.int32, (ts, ts), 1)
    same = (ri // GDN_BLOCK) == (ci // GDN_BLOCK)
    gc = _sel_dot(jnp.where(same & (ci <= ri), 1.0, 0.0), g)
    gl = _sel_dot(jnp.where(same, 1.0, 0.0), g)
    out_ref[...] = jnp.concatenate([beta, gc, gl, jnp.zeros((ts, LANES - 3 * nh), F32)], axis=1)


def _gdn_gates(ba, a_log, dt_bias, *, ts=512):
    s, two_h = ba.shape
    nh = two_h // 2
    assert 3 * nh <= LANES
    return pl.pallas_call(
        _gdn_gates_kernel,
        grid=(s // ts,),
        in_specs=[pl.BlockSpec((ts, two_h), lambda i: (i, 0)),
                  pl.BlockSpec((1, nh), lambda i: (0, 0)),
                  pl.BlockSpec((1, nh), lambda i: (0, 0))],
        out_specs=pl.BlockSpec((ts, LANES), lambda i: (i, 0)),
        out_shape=jax.ShapeDtypeStruct((s, LANES), F32),
        compiler_params=_params("parallel"),
        name="gdn_gates",
    )(ba, a_log.reshape(1, nh), dt_bias.reshape(1, nh))


def _bdot(a, b):
    return jnp.einsum("hij,hjk->hik", a.astype(BF16), b.astype(BF16), preferred_element_type=F32)


def _bdot3(a, b):
    ah, al = _split(a)
    bh, bl = _split(b)
    return jnp.einsum("hij,hjk->hik", jnp.concatenate([ah, ah, al], axis=2),
                      jnp.concatenate([bh, bl, bh], axis=1), preferred_element_type=F32)


def _unit_lower_inverse(low, ri, ci):
    n = low.shape[-1]

    def blk(b):
        return ((ri // b) == (ci // b))[None]

    eye = jnp.where(ri == ci, 1.0, 0.0)[None]
    n1 = jnp.where(blk(16), -low, 0.0)
    n2 = _bdot(n1, n1)
    n4 = _bdot(n2, n2)
    n8 = _bdot(n4, n4)
    t = eye + n1
    t = t + _bdot(n2, t)
    t = t + _bdot(n4, t)
    t = t + _bdot(n8, t)
    b = 32
    while b <= n:
        off = jnp.where(blk(b) & jnp.logical_not(blk(b // 2)), low, 0.0)
        t = t - _bdot(t, _bdot(off, t))
        b *= 2
    return t


def _gdn_chunk_kernel(q_ref, k_ref, v_ref, gates_ref, gct_ref, o_ref, state, *, heads):
    hg = pl.program_id(0)
    rows = q_ref.shape[0]
    nh = gct_ref.shape[0]
    dh = HEAD_DIM
    rep = rows // dh

    @pl.when(pl.program_id(1) == 0)
    def _():
        state[...] = jnp.zeros_like(state)

    ri = lax.broadcasted_iota(jnp.int32, (rows, rows), 0)
    ci = lax.broadcasted_iota(jnp.int32, (rows, rows), 1)
    lower = (ci <= ri)[None]
    strict = (ci < ri)[None]
    dot = functools.partial(jnp.dot, preferred_element_type=F32)

    gates_hi, gates_lo = _split(gates_ref[...])
    width = heads * 3 * dh
    prow = lax.broadcasted_iota(jnp.int32, (LANES, width), 0)
    pcol = lax.broadcasted_iota(jnp.int32, (LANES, width), 1)
    pick = jnp.where(prow == ((pcol // dh) % 3) * nh + hg * heads + pcol // (3 * dh), 1.0, 0.0).astype(BF16)
    cols = dot(jnp.concatenate([gates_hi, gates_lo], axis=1),
               jnp.concatenate([pick, pick], axis=0))

    def per_head(j):
        return jnp.stack([cols[:, (3 * hl + j) * dh:(3 * hl + j + 1) * dh] for hl in range(heads)])

    beta, gcol, glast = per_head(0), per_head(1), per_head(2)
    gct_hi, gct_lo = _split(gct_ref[...])
    trow = lax.broadcasted_iota(jnp.int32, (heads * rows, nh), 0) // rows
    tcol = lax.broadcasted_iota(jnp.int32, (heads * rows, nh), 1)
    pick_t = jnp.where(tcol == hg * heads + trow, 1.0, 0.0).astype(BF16)
    g_j = dot(jnp.concatenate([pick_t, pick_t], axis=1),
              jnp.concatenate([gct_hi, gct_lo], axis=0)).reshape(heads, rows, rows)
    g_i = jnp.concatenate([gcol] * rep, axis=2)
    beta_i = jnp.concatenate([beta] * rep, axis=2)
    decay = jnp.where(lower, jnp.exp(jnp.where(lower, g_i - g_j, 0.0)), 0.0)

    def pairs(ref):
        return [ref[:, p * dh:(p + 1) * dh] for p in range(heads // 2)]

    def nt(a, b):
        return jnp.einsum("pid,pjd->pij", a.astype(BF16), b.astype(BF16), preferred_element_type=F32)

    def both(x):
        return jnp.stack([x[hl // 2] for hl in range(heads)])

    qp, kp = jnp.stack(pairs(q_ref)), jnp.stack(pairs(k_ref))
    q, k = both(qp), both(kp)
    kk, qk = both(nt(kp, kp)), both(nt(qp, kp))
    v = jnp.stack([v_ref[:, hl * dh:(hl + 1) * dh] for hl in range(heads)])

    low = jnp.where(strict, kk * beta_i * decay, 0.0)
    t0 = _unit_lower_inverse(low, ri, ci)
    rhs = jnp.concatenate([v * beta, k * (beta * jnp.exp(gcol))], axis=2)
    x = _bdot(t0, rhs)
    x = x + _bdot(t0, rhs - x - _bdot3(low, x))
    u, w = x[:, :, :dh], x[:, :, dh:]
    a_x = _bdot(jnp.where(lower, qk * decay, 0.0), x)
    q_eff = q * jnp.exp(gcol) - a_x[:, :, dh:]
    k_g = k * jnp.exp(glast - gcol)

    s = state[...]
    o = _bdot(q_eff, s) + a_x[:, :, :dh]
    v_new = u - _bdot(w, s)
    state[...] = s * jnp.exp(glast[:, 0:1, :]) + jnp.einsum(
        "hck,hcv->hkv", k_g.astype(BF16), v_new.astype(BF16), preferred_element_type=F32)
    for hl in range(heads):
        o_ref[:, hl * dh:(hl + 1) * dh] = o[hl]


def _gdn_chunk(q, k, v, gates, gct, *, heads=8):
    rows = GDN_BLOCK
    s, dv_total = v.shape
    nh = dv_total // HEAD_DIM
    assert nh == 2 * (q.shape[1] // HEAD_DIM) and heads % 2 == 0 and nh % heads == 0
    qk_spec = pl.BlockSpec((rows, heads // 2 * HEAD_DIM), lambda h, i: (i, h))
    v_spec = pl.BlockSpec((rows, heads * HEAD_DIM), lambda h, i: (i, h))
    return pl.pallas_call(
        functools.partial(_gdn_chunk_kernel, heads=heads),
        grid=(nh // heads, s // rows),
        in_specs=[qk_spec, qk_spec, v_spec,
                  pl.BlockSpec((rows, LANES), lambda h, i: (i, 0)),
                  pl.BlockSpec((nh, rows), lambda h, i: (0, i))],
        out_specs=v_spec,
        out_shape=jax.ShapeDtypeStruct((s, dv_total), F32),
        scratch_shapes=[pltpu.VMEM((heads, HEAD_DIM, HEAD_DIM), F32)],
        compiler_params=_params("parallel", "arbitrary"),
        name="gdn_chunk",
    )(q, k, v, gates, gct)


def _gdn_post_kernel(o_ref, z_ref, nw_ref, out_ref):
    nw = nw_ref[...]
    for hh in range(o_ref.shape[1] // HEAD_DIM):
        sl = slice(hh * HEAD_DIM, (hh + 1) * HEAD_DIM)
        o = o_ref[:, sl]
        ms = jnp.mean(o * o, axis=-1, keepdims=True)
        out_ref[:, sl] = ((o * lax.rsqrt(ms + NORM_EPS)) * nw * _silu(z_ref[:, sl])).astype(out_ref.dtype)


def _gdn_post(o, proj, z_col0, norm_w, *, ts=256):
    s, dv = o.shape
    zb = z_col0 // dv
    return pl.pallas_call(
        _gdn_post_kernel,
        grid=(s // ts,),
        in_specs=[pl.BlockSpec((ts, dv), lambda i: (i, 0)),
                  pl.BlockSpec((ts, dv), lambda i: (i, zb)),
                  pl.BlockSpec((1, HEAD_DIM), lambda i: (0, 0))],
        out_specs=pl.BlockSpec((ts, dv), lambda i: (i, 0)),
        out_shape=jax.ShapeDtypeStruct((s, dv), BF16),
        compiler_params=_params("parallel"),
        name="gdn_post",
    )(o, proj, norm_w.reshape(1, HEAD_DIM))


def _gated_deltanet(x, sc, sh, w_in, conv_w, a_log, dt_bias, norm_w):
    d = x.shape[1]
    nvh = a_log.shape[0]
    dv = nvh * HEAD_DIM
    dqk = (conv_w.shape[1] - dv) // 2
    n_main = conv_w.shape[1] + dv
    proj = _mod_matmul(x, sc, sh, w_in[:, :n_main], name="gdn_in_proj")
    ba = _mod_matmul(x, sc, sh, w_in[:, n_main:], name="gdn_in_gates")
    q = _gdn_conv(proj, conv_w, 0, dqk, normalize=True, scale=HEAD_DIM ** -0.5)
    k = _gdn_conv(proj, conv_w, dqk, dqk, normalize=True, scale=1.0)
    v = _gdn_conv(proj, conv_w, 2 * dqk, dv, normalize=False, scale=1.0)
    gates = _gdn_gates(ba, a_log, dt_bias)
    o = _gdn_chunk(q, k, v, gates, gates[:, nvh:2 * nvh].T)
    return _gdn_post(o, proj, conv_w.shape[1], norm_w)


def _rope(x, c, sg):
    return x * c + pltpu.roll(x, HEAD_DIM // 2, 1) * sg


def _nsa_prep_kernel(p_ref, c_ref, s_ref, q_ref, kc_ref, ks_ref, vs_ref, kw_ref, vw_ref, *, dq, dkv):
    c = c_ref[...]
    sg = s_ref[...]
    scale = HEAD_DIM ** -0.5

    def heads(col0, n, out_ref, rope, mul):
        for hh in range(n // HEAD_DIM):
            x = p_ref[:, col0 + hh * HEAD_DIM:col0 + (hh + 1) * HEAD_DIM]
            if rope:
                x = _rope(x, c, sg)
            if mul != 1.0:
                x = x * mul
            out_ref[:, hh * HEAD_DIM:(hh + 1) * HEAD_DIM] = x.astype(out_ref.dtype)

    heads(0, dq, q_ref, True, scale)
    heads(dq, dkv, kc_ref, True, 1.0)
    heads(dq + 2 * dkv, dkv, ks_ref, True, 1.0)
    heads(dq + 3 * dkv, dkv, vs_ref, False, 1.0)
    heads(dq + 4 * dkv, dkv, kw_ref, True, 1.0)
    heads(dq + 5 * dkv, dkv, vw_ref, False, 1.0)


def _nsa_prep(proj, cos2, sin2, dq, dkv, *, ts=256):
    s, n = proj.shape
    kv_spec = pl.BlockSpec((ts, dkv), lambda i: (i, 0))
    tab = pl.BlockSpec((ts, HEAD_DIM), lambda i: (i, 0))
    kv16 = jax.ShapeDtypeStruct((s, dkv), BF16)
    return pl.pallas_call(
        functools.partial(_nsa_prep_kernel, dq=dq, dkv=dkv),
        grid=(s // ts,),
        in_specs=[pl.BlockSpec((ts, n), lambda i: (i, 0)), tab, tab],
        out_specs=[pl.BlockSpec((ts, dq), lambda i: (i, 0)), kv_spec, kv_spec, kv_spec, kv_spec, kv_spec],
        out_shape=[jax.ShapeDtypeStruct((s, dq), BF16), jax.ShapeDtypeStruct((s, dkv), F32),
                   kv16, kv16, kv16, kv16],
        compiler_params=_params("parallel"),
        name="nsa_prep",
    )(proj, cos2, sin2)


def _nsa_compress_kernel(x_ref, pe_ref, w1_ref, w2_ref, o_ref):
    x = x_ref[0, 0]
    n = x.shape[0]
    a = _dot(x + pe_ref[0, 0:1, :], w1_ref[0, 0])
    b = _dot(x + pe_ref[0, 1:2, :], w1_ref[0, 1])
    y = a + pltpu.roll(b, n - 1, 0)
    o_ref[0, 0] = _dot(_silu(y), w2_ref[0])


def _nsa_compress(xk, xv, pe, w1, w2):
    hk, ng, gw = xk.shape
    half = CMP_LEN // 2
    x = jnp.stack([xk, xv])
    pe2 = pe.reshape(2, 2, half * HEAD_DIM)
    w1r = w1.reshape(2, 2, half * HEAD_DIM, HEAD_DIM)
    return pl.pallas_call(
        _nsa_compress_kernel,
        grid=(2, hk),
        in_specs=[pl.BlockSpec((1, 1, ng, gw), lambda t, h: (t, h, 0, 0)),
                  pl.BlockSpec((1, 2, gw), lambda t, h: (t, 0, 0)),
                  pl.BlockSpec((1, 2, gw, HEAD_DIM), lambda t, h: (t, 0, 0, 0)),
                  pl.BlockSpec((1, HEAD_DIM, HEAD_DIM), lambda t, h: (t, 0, 0))],
        out_specs=pl.BlockSpec((1, 1, ng, HEAD_DIM), lambda t, h: (t, h, 0, 0)),
        out_shape=jax.ShapeDtypeStruct((2, hk, ng, HEAD_DIM), F32),
        compiler_params=_params("parallel", "parallel"),
        name="nsa_compress",
    )(x, pe2, w1r, w2)


def _nsa_cmp_select_kernel(q_ref, kc_ref, vc_ref, ov_ref, o_ref, sel_ref):
    tq = q_ref.shape[0]
    nc = kc_ref.shape[2]
    nbp = ov_ref.shape[1]
    t0 = pl.program_id(0) * tq
    tpos = t0 + lax.broadcasted_iota(jnp.int32, (tq, nc), 0)
    nidx = lax.broadcasted_iota(jnp.int32, (tq, nc), 1)
    valid = (nidx * CMP_STRIDE + (CMP_LEN - 1)) <= tpos
    kc = kc_ref[0, 0].astype(BF16)
    vc = vc_ref[0, 0].astype(BF16)
    psum = jnp.zeros((tq, nc), F32)
    for g in range(NSA_GROUP):
        sl = slice(g * HEAD_DIM, (g + 1) * HEAD_DIM)
        s = jnp.where(valid, _dot_nt(q_ref[:, sl], kc), NEG_BIG)
        m = jnp.max(s, axis=-1, keepdims=True)
        p = jnp.where(valid, jnp.exp(s - m), 0.0)
        den = jnp.sum(p, axis=-1, keepdims=True)
        p = p / jnp.where(den > 0.0, den, 1.0)
        o_ref[:, sl] = jnp.dot(p.astype(BF16), vc, preferred_element_type=F32)
        psum = psum + p
    imp = _dot_sel(psum, ov_ref[...])
    j = lax.broadcasted_iota(jnp.int32, (tq, nbp), 1)
    cur = (t0 + lax.broadcasted_iota(jnp.int32, (tq, nbp), 0)) // SLC_LEN
    forced = (j == 0) | (j == cur) | (j == cur - 1)
    work = jnp.where(j <= cur, jnp.where(forced, jnp.inf, imp), -jnp.inf)
    sel = jnp.zeros((tq, nbp), F32)
    for _ in range(SLC_TOP):
        m = jnp.max(work, axis=-1, keepdims=True)
        first = jnp.min(jnp.where(work == m, j, nbp), axis=-1, keepdims=True)
        hit = j == first
        sel = jnp.where(hit, 1.0, sel)
        work = jnp.where(hit, -jnp.inf, work)
    sel_ref[0] = jnp.where(j <= cur, sel, 0.0).astype(sel_ref.dtype)


def _nsa_cmp_select(q, cmp, ov, *, tq=512):
    s, dq = q.shape
    _, hk, nc, _ = cmp.shape
    nbp = ov.shape[1]
    gw = NSA_GROUP * HEAD_DIM
    return pl.pallas_call(
        _nsa_cmp_select_kernel,
        grid=(s // tq, hk),
        in_specs=[pl.BlockSpec((tq, gw), lambda i, h: (i, h)),
                  pl.BlockSpec((1, 1, nc, HEAD_DIM), lambda i, h: (0, h, 0, 0)),
                  pl.BlockSpec((1, 1, nc, HEAD_DIM), lambda i, h: (1, h, 0, 0)),
                  pl.BlockSpec((nc, nbp), lambda i, h: (0, 0))],
        out_specs=[pl.BlockSpec((tq, gw), lambda i, h: (i, h)),
                   pl.BlockSpec((1, tq, nbp), lambda i, h: (h, i, 0))],
        out_shape=[jax.ShapeDtypeStruct((s, dq), F32),
                   jax.ShapeDtypeStruct((hk, s, nbp), BF16)],
        compiler_params=_params("parallel", "parallel"),
        name="nsa_cmp_select",
    )(q, cmp, cmp, ov)


ATTN_ROWS = 64


def _masked_attn_kernel(*refs, tq, tk, selected):
    if selected:
        (qi_ref, kj_ref, first_ref, last_ref, q_ref, k_ref, v_ref, sel_ref, o_ref,
         q4, bias, s_scr, p_scr, a_scr, m_scr, acc) = refs
    else:
        (qi_ref, kj_ref, first_ref, last_ref, q_ref, k_ref, v_ref, o_ref,
         q4, bias, s_scr, p_scr, a_scr, m_scr, acc) = refs
    t = pl.program_id(1)
    qi = qi_ref[t]
    kj = kj_ref[t]
    dh = HEAD_DIM

    @pl.when(first_ref[t] == 1)
    def _():
        for g in range(NSA_GROUP):
            q4[g * tq:(g + 1) * tq, :] = q_ref[:, g * dh:(g + 1) * dh]
        m_scr[...] = jnp.full_like(m_scr, NEG_BIG)
        acc[...] = jnp.zeros_like(acc)

    tpos = qi * tq + lax.broadcasted_iota(jnp.int32, (tq, tk), 0)
    kpos = kj * tk + lax.broadcasted_iota(jnp.int32, (tq, tk), 1)
    if selected:
        nbp = sel_ref.shape[2]
        expand = jnp.where(lax.broadcasted_iota(jnp.int32, (nbp, tk), 0)
                           == (kj * tk + lax.broadcasted_iota(jnp.int32, (nbp, tk), 1)) // SLC_LEN, 1.0, 0.0)
        picked = jnp.dot(sel_ref[0], expand.astype(BF16), preferred_element_type=F32)
        mask = (picked > 0.5) & (kpos <= tpos)
    else:
        mask = (kpos <= tpos) & (kpos > tpos - WINDOW)
    bias[...] = jnp.where(mask, 0.0, NEG_BIG)

    s_scr[...] = lax.dot_general(q4[...], k_ref[...], (((1,), (1,)), ((), ())), preferred_element_type=F32)
    rb = ATTN_ROWS
    for r0 in range(0, NSA_GROUP * tq, rb):
        rows = slice(r0, r0 + rb)
        brow = r0 % tq
        s = s_scr[rows, :] + bias[brow:brow + rb, :]
        m_prev = m_scr[rows, :]
        m_new = jnp.maximum(m_prev, jnp.max(s, axis=-1, keepdims=True))
        a_scr[rows, :] = jnp.exp(m_prev - m_new)
        p_scr[rows, :] = jnp.exp(s - jnp.concatenate([m_new] * (tk // LANES), axis=1)).astype(BF16)
        m_scr[rows, :] = m_new
    v_one = jnp.concatenate([v_ref[...], jnp.ones((tk, dh), BF16)], axis=1)
    alpha = jnp.concatenate([a_scr[...]] * 2, axis=1)
    acc[...] = alpha * acc[...] + jnp.dot(p_scr[...], v_one, preferred_element_type=F32)

    @pl.when(last_ref[t] == 1)
    def _():
        for g in range(NSA_GROUP):
            rows = slice(g * tq, (g + 1) * tq)
            o_ref[:, g * dh:(g + 1) * dh] = acc[rows, :dh] / acc[rows, dh:]


def _masked_attn(q, k, v, sel, *, tq, tk):
    s, dq = q.shape
    hk = k.shape[1] // HEAD_DIM
    gw = NSA_GROUP * HEAD_DIM
    selected = sel is not None
    pairs = []
    for i in range(s // tq):
        lo = 0 if selected else max(i * tq - (WINDOW - 1), 0) // tk
        hi = (i * tq + tq - 1) // tk
        pairs += [(i, j, int(j == lo), int(j == hi)) for j in range(lo, hi + 1)]
    sched = [jnp.asarray([p[c] for p in pairs], jnp.int32) for c in range(4)]

    in_specs = [pl.BlockSpec((tq, gw), lambda h, t, qi, kj, fi, la: (qi[t], h)),
                pl.BlockSpec((tk, HEAD_DIM), lambda h, t, qi, kj, fi, la: (kj[t], h)),
                pl.BlockSpec((tk, HEAD_DIM), lambda h, t, qi, kj, fi, la: (kj[t], h))]
    args = [q, k, v]
    if selected:
        in_specs.append(pl.BlockSpec((1, tq, sel.shape[2]), lambda h, t, qi, kj, fi, la: (h, qi[t], 0)))
        args.append(sel)
    rows = NSA_GROUP * tq
    return pl.pallas_call(
        functools.partial(_masked_attn_kernel, tq=tq, tk=tk, selected=selected),
        grid_spec=pltpu.PrefetchScalarGridSpec(
            num_scalar_prefetch=4,
            grid=(hk, len(pairs)),
            in_specs=in_specs,
            out_specs=pl.BlockSpec((tq, gw), lambda h, t, qi, kj, fi, la: (qi[t], h)),
            scratch_shapes=[pltpu.VMEM((rows, HEAD_DIM), BF16),
                            pltpu.VMEM((tq, tk), F32),
                            pltpu.VMEM((rows, tk), F32),
                            pltpu.VMEM((rows, tk), BF16),
                            pltpu.VMEM((rows, LANES), F32),
                            pltpu.VMEM((rows, LANES), F32),
                            pltpu.VMEM((rows, 2 * HEAD_DIM), F32)]),
        out_shape=jax.ShapeDtypeStruct((s, dq), F32),
        compiler_params=_params("parallel", "arbitrary"),
        name="nsa_selected_attn" if selected else "nsa_window_attn",
    )(*sched, *args)


def _nsa_combine_kernel(oc_ref, os_ref, ow_ref, g_ref, out_ref):
    gates = jax.nn.sigmoid(g_ref[...])
    ng, dq = g_ref.shape[1], oc_ref.shape[1]
    row = lax.broadcasted_iota(jnp.int32, (ng, dq), 0)
    head3 = (lax.broadcasted_iota(jnp.int32, (ng, dq), 1) // HEAD_DIM) * N_BRANCH
    acc = None
    for b, ref in enumerate((oc_ref, os_ref, ow_ref)):
        gb = _dot_sel(gates, jnp.where(row == head3 + b, 1.0, 0.0))
        term = gb * ref[...]
        acc = term if acc is None else acc + term
    out_ref[...] = acc.astype(out_ref.dtype)


def _nsa_combine(oc, osel, ow, gates, *, ts=256):
    s, dq = oc.shape
    spec = pl.BlockSpec((ts, dq), lambda i: (i, 0))
    return pl.pallas_call(
        _nsa_combine_kernel,
        grid=(s // ts,),
        in_specs=[spec, spec, spec, pl.BlockSpec((ts, gates.shape[1]), lambda i: (i, 0))],
        out_specs=spec,
        out_shape=jax.ShapeDtypeStruct((s, dq), BF16),
        compiler_params=_params("parallel"),
        name="nsa_combine",
    )(oc, osel, ow, gates)


def _nsa_attention(x, sc, sh, cos2, sin2, w_in, cmp_pe, cmp_w1, cmp_w2):
    s, d = x.shape
    dq = d
    hk = d // HEAD_DIM // NSA_GROUP
    dkv = hk * HEAD_DIM
    n_main = dq + 6 * dkv
    ngate = w_in.shape[1] - n_main
    proj = _mod_matmul(x, sc, sh, w_in[:, :n_main], name="nsa_in_proj")
    w_gate = jnp.pad(w_in[:, n_main:], ((0, 0), (0, LANES - ngate)))
    gates = _mod_matmul(x, sc, sh, w_gate, name="nsa_in_gates")
    q, kc, ks, vs, kw, vw = _nsa_prep(proj, cos2, sin2, dq, dkv)
    vc = proj[:, dq + dkv:dq + 2 * dkv]

    def groups(t):
        return t.reshape(s // CMP_STRIDE, CMP_STRIDE, hk, HEAD_DIM).transpose(2, 0, 1, 3).reshape(
            hk, s // CMP_STRIDE, CMP_STRIDE * HEAD_DIM)

    cmp = _nsa_compress(groups(kc), groups(vc), cmp_pe, cmp_w1, cmp_w2)

    nc = s // CMP_STRIDE
    n_slc = s // SLC_LEN
    nbp = -(-n_slc // LANES) * LANES
    c_start = CMP_STRIDE * jnp.arange(nc)
    s_start = SLC_LEN * jnp.arange(nbp)
    overlap = jnp.clip(jnp.minimum(c_start[:, None] + CMP_LEN, s_start[None, :] + SLC_LEN)
                       - jnp.maximum(c_start[:, None], s_start[None, :]), 0, None).astype(F32) / CMP_LEN
    o_c, sel = _nsa_cmp_select(q, cmp, overlap.astype(BF16))
    o_s = _masked_attn(q, ks, vs, sel, tq=256, tk=512)
    o_w = _masked_attn(q, kw, vw, None, tq=256, tk=256)
    return _nsa_combine(o_c, o_s, o_w, gates)


def kernel(x, c, positions, mod_w, mod_b, ln_g, ln_b, ffn_w_gu, ffn_w_down, gdn_w_in, gdn_conv_w,
           gdn_a_log, gdn_dt_bias, gdn_norm_w, gdn_w_out, nsa_w_in, nsa_cmp_pe, nsa_cmp_w1,
           nsa_cmp_w2, nsa_w_out):
    bsz, s, d = x.shape
    assert bsz == 1
    depth = mod_w.shape[0]
    alpha = (2 * depth) ** 0.25

    inv = ROPE_THETA ** (-jnp.arange(0, HEAD_DIM, 2, dtype=F32) / HEAD_DIM)
    ang = positions[0].astype(F32)[:, None] * inv
    cos, sin = jnp.cos(ang), jnp.sin(ang)
    cos2 = jnp.concatenate([cos, cos], axis=-1)
    sin2 = jnp.concatenate([-sin, sin], axis=-1)

    mod = _modulation(c, mod_w, mod_b)
    xs = x[0]
    for i in range(depth):
        sh1, sc1, ga1, sh2, sc2, ga2 = [mod[i, :, r * d:(r + 1) * d] for r in range(6)]
        j = i // N_MIXERS
        if i % N_MIXERS == 0:
            y = _gated_deltanet(xs, sc1, sh1, gdn_w_in[j], gdn_conv_w[j], gdn_a_log[j],
                                gdn_dt_bias[j], gdn_norm_w[j])
            w_out = gdn_w_out[j]
        else:
            y = _nsa_attention(xs, sc1, sh1, cos2, sin2, nsa_w_in[j], nsa_cmp_pe[j],
                               nsa_cmp_w1[j], nsa_cmp_w2[j])
            w_out = nsa_w_out[j]
        xs = _matmul_ln(y, w_out, xs, ga1, ln_g[i, 0:1], ln_b[i, 0:1], alpha=alpha)
        a = _swiglu_up(xs, sc2, sh2, ffn_w_gu[i])
        xs = _matmul_ln(a, ffn_w_down[i], xs, ga2, ln_g[i, 1:2], ln_b[i, 1:2], alpha=alpha)
    return xs[None]
```

```python
import functools

import jax
import jax.numpy as jnp
from jax import lax
from jax.experimental import pallas as pl
from jax.experimental.pallas import tpu as pltpu

F32 = jnp.float32
BF16 = jnp.bfloat16
HIGHEST = lax.Precision.HIGHEST

HEAD_DIM = 128
N_MIXERS = 2
GDN_CONV = 4
GDN_BLOCK = 128
NSA_GROUP = 4
N_BRANCH = 3
CMP_LEN = 32
CMP_STRIDE = 16
SLC_LEN = 64
SLC_TOP = 16
WINDOW = 512
ROPE_THETA = 10000.0
LN_EPS = 1e-5
NORM_EPS = 1e-6

LOG2E = 1.4426950408889634
MATMUL_ROWS = 1024
LANES = 128
VMEM_LIMIT_BYTES = 56 * 1024 * 1024
NEG_BIG = -1e30


def _params(*sem):
    return pltpu.CompilerParams(dimension_semantics=sem, vmem_limit_bytes=VMEM_LIMIT_BYTES)


def _silu(v):
    return v * jax.nn.sigmoid(v)


def _dot(a, b):
    return jnp.dot(a.astype(BF16), b.astype(BF16), preferred_element_type=F32)


def _dot_nt(a, b):
    return lax.dot_general(a.astype(BF16), b.astype(BF16), (((1,), (1,)), ((), ())),
                           preferred_element_type=F32)


def _split(a):
    hi = a.astype(BF16)
    lo = (a - hi.astype(F32)).astype(BF16)
    return hi, lo


def _dot3(a, b):
    ah, al = _split(a)
    bh, bl = _split(b)
    d = functools.partial(jnp.dot, preferred_element_type=F32)
    return d(ah, bh) + (d(ah, bl) + d(al, bh))


def _dot_sel(a, onehot):
    ah, al = _split(a)
    e = onehot.astype(BF16)
    d = functools.partial(jnp.dot, preferred_element_type=F32)
    return d(ah, e) + d(al, e)


def _sel_dot(onehot, b):
    bh, bl = _split(b)
    e = onehot.astype(BF16)
    d = functools.partial(jnp.dot, preferred_element_type=F32)
    return d(e, bh) + d(e, bl)


def _mod_kernel(c_ref, w_ref, b_ref, o_ref):
    cond = _silu(c_ref[...])
    o_ref[0] = _dot3(cond, w_ref[0]) + b_ref[0]


def _modulation(c, mod_w, mod_b):
    depth, d, n = mod_w.shape
    tn = 1536
    c8 = jnp.broadcast_to(c[:1], (8, d))
    out = pl.pallas_call(
        _mod_kernel,
        grid=(depth, n // tn),
        in_specs=[pl.BlockSpec((8, d), lambda l, j: (0, 0)),
                  pl.BlockSpec((1, d, tn), lambda l, j: (l, 0, j)),
                  pl.BlockSpec((1, 1, tn), lambda l, j: (l, 0, j))],
        out_specs=pl.BlockSpec((1, 8, tn), lambda l, j: (l, 0, j)),
        out_shape=jax.ShapeDtypeStruct((depth, 8, n), F32),
        compiler_params=_params("parallel", "parallel"),
        name="modulation",
    )(c8, mod_w, mod_b.reshape(depth, 1, n))
    return out[:, 0:1, :]


def _mod_matmul_kernel(x_ref, sc_ref, sh_ref, w_ref, o_ref, h_scr):
    @pl.when(pl.program_id(1) == 0)
    def _():
        h_scr[...] = (x_ref[...] * (1.0 + sc_ref[...]) + sh_ref[...]).astype(BF16)

    o_ref[...] = jnp.dot(h_scr[...], w_ref[...].astype(BF16),
                         preferred_element_type=F32).astype(o_ref.dtype)


def _mod_matmul(x, sc, sh, w, layer, col0, n, *, tm=MATMUL_ROWS, tn=512, name="mod_matmul"):
    m, k = x.shape
    tm = min(tm, m)
    tn = min(tn, n)
    cb0 = col0 // tn
    assert col0 % tn == 0 and n % tn == 0 and m % tm == 0
    return pl.pallas_call(
        _mod_matmul_kernel,
        grid=(m // tm, n // tn),
        in_specs=[pl.BlockSpec((tm, k), lambda i, j: (i, 0)),
                  pl.BlockSpec((1, k), lambda i, j: (0, 0)),
                  pl.BlockSpec((1, k), lambda i, j: (0, 0)),
                  pl.BlockSpec((None, k, tn), lambda i, j: (layer, 0, j + cb0))],
        out_specs=pl.BlockSpec((tm, tn), lambda i, j: (i, j)),
        out_shape=jax.ShapeDtypeStruct((m, n), F32),
        scratch_shapes=[pltpu.VMEM((tm, k), BF16)],
        compiler_params=_params("parallel", "arbitrary"),
        name=name,
    )(x, sc, sh, w)


def _swiglu_up_kernel(x_ref, sc_ref, sh_ref, wg_ref, wu_ref, o_ref, h_scr):
    @pl.when(pl.program_id(1) == 0)
    def _():
        h_scr[...] = (x_ref[...] * (1.0 + sc_ref[...]) + sh_ref[...]).astype(BF16)

    h = h_scr[...]
    g = jnp.dot(h, wg_ref[...].astype(BF16), preferred_element_type=F32)
    u = jnp.dot(h, wu_ref[...].astype(BF16), preferred_element_type=F32)
    o_ref[...] = (_silu(g) * u).astype(o_ref.dtype)


def _swiglu_up(x, sc, sh, w_gu, layer, *, tm=MATMUL_ROWS, tn=512):
    m, k = x.shape
    tm = min(tm, m)
    dff = w_gu.shape[2] // 2
    nb = dff // tn
    assert dff % tn == 0 and m % tm == 0
    return pl.pallas_call(
        _swiglu_up_kernel,
        grid=(m // tm, nb),
        in_specs=[pl.BlockSpec((tm, k), lambda i, j: (i, 0)),
                  pl.BlockSpec((1, k), lambda i, j: (0, 0)),
                  pl.BlockSpec((1, k), lambda i, j: (0, 0)),
                  pl.BlockSpec((None, k, tn), lambda i, j: (layer, 0, j)),
                  pl.BlockSpec((None, k, tn), lambda i, j: (layer, 0, j + nb))],
        out_specs=pl.BlockSpec((tm, tn), lambda i, j: (i, j)),
        out_shape=jax.ShapeDtypeStruct((m, dff), BF16),
        scratch_shapes=[pltpu.VMEM((tm, k), BF16)],
        compiler_params=_params("parallel", "arbitrary"),
        name="swiglu_up",
    )(x, sc, sh, w_gu, w_gu)


LN_ROWS = 256


def _matmul_ln_kernel(a_ref, w_ref, x_ref, ga_ref, g_ref, b_ref, o_ref, *, alpha):
    kk = pl.program_id(1)
    part = jnp.dot(a_ref[...].astype(BF16), w_ref[...].astype(BF16), preferred_element_type=F32)

    @pl.when(kk == 0)
    def _():
        o_ref[...] = part

    @pl.when(kk > 0)
    def _():
        o_ref[...] += part

    @pl.when(kk == pl.num_programs(1) - 1)
    def _():
        for r0 in range(0, o_ref.shape[0], LN_ROWS):
            rows = slice(r0, r0 + LN_ROWS)
            r = alpha * x_ref[rows, :] + (1.0 + ga_ref[...]) * o_ref[rows, :]
            mu = jnp.mean(r, axis=-1, keepdims=True)
            d = r - mu
            var = jnp.mean(d * d, axis=-1, keepdims=True)
            o_ref[rows, :] = d * lax.rsqrt(var + LN_EPS) * g_ref[...] + b_ref[...]


def _matmul_ln(a, w, layer, x, ga, g, b, *, alpha, tm=MATMUL_ROWS, tk=256):
    m, k = a.shape
    n = w.shape[2]
    tm = min(tm, m)
    assert m % tm == 0 and k % tk == 0 and tm % LN_ROWS == 0
    return pl.pallas_call(
        functools.partial(_matmul_ln_kernel, alpha=alpha),
        grid=(m // tm, k // tk),
        in_specs=[pl.BlockSpec((tm, tk), lambda i, kk: (i, kk)),
                  pl.BlockSpec((None, tk, n), lambda i, kk: (layer, kk, 0)),
                  pl.BlockSpec((tm, n), lambda i, kk: (i, 0)),
                  pl.BlockSpec((1, n), lambda i, kk: (0, 0)),
                  pl.BlockSpec((1, n), lambda i, kk: (0, 0)),
                  pl.BlockSpec((1, n), lambda i, kk: (0, 0))],
        out_specs=pl.BlockSpec((tm, n), lambda i, kk: (i, 0)),
        out_shape=jax.ShapeDtypeStruct((m, n), F32),
        compiler_params=_params("parallel", "arbitrary"),
        name="matmul_ln",
    )(a, w, x, ga, g, b)


def _gdn_conv_kernel(xp_ref, xc_ref, w_ref, o_ref, buf, *, normalize, scale):
    ts, tc = xc_ref.shape
    buf[0:8, :] = jnp.where(pl.program_id(0) > 0, xp_ref[...], 0.0)
    buf[8:, :] = xc_ref[...]
    w = w_ref[...]
    y = w[0:1, :] * buf[5:5 + ts, :]
    for i in range(1, GDN_CONV):
        y = y + w[i:i + 1, :] * buf[5 + i:5 + i + ts, :]
    y = _silu(y)
    if normalize:
        for hh in range(tc // HEAD_DIM):
            seg = y[:, hh * HEAD_DIM:(hh + 1) * HEAD_DIM]
            ss = jnp.sum(seg * seg, axis=-1, keepdims=True)
            o_ref[:, hh * HEAD_DIM:(hh + 1) * HEAD_DIM] = seg * (lax.rsqrt(ss + NORM_EPS) * scale)
    else:
        o_ref[...] = y


def _gdn_conv(proj, conv_w, layer, col0, ncols, *, normalize, scale, ts=256, tc=512):
    s = proj.shape[0]
    cb0 = col0 // tc
    hb = ts // 8
    return pl.pallas_call(
        functools.partial(_gdn_conv_kernel, normalize=normalize, scale=scale),
        grid=(s // ts, ncols // tc),
        in_specs=[pl.BlockSpec((8, tc), lambda i, j: (jnp.maximum(i * hb - 1, 0), j + cb0)),
                  pl.BlockSpec((ts, tc), lambda i, j: (i, j + cb0)),
                  pl.BlockSpec((None, GDN_CONV, tc), lambda i, j: (layer, 0, j + cb0))],
        out_specs=pl.BlockSpec((ts, tc), lambda i, j: (i, j)),
        out_shape=jax.ShapeDtypeStruct((s, ncols), F32),
        scratch_shapes=[pltpu.VMEM((ts + 8, tc), F32)],
        compiler_params=_params("parallel", "parallel"),
        name="gdn_conv",
    )(proj, proj, conv_w)


def _gdn_gates_kernel(ba_ref, alog_ref, dtb_ref, out_ref):
    ts = ba_ref.shape[0]
    nh = ba_ref.shape[1] // 2
    x = ba_ref[...]
    beta = jax.nn.sigmoid(x[:, :nh])
    z = x[:, nh:] + dtb_ref[...]
    softplus = jnp.maximum(z, 0.0) + jnp.log1p(jnp.exp(-jnp.abs(z)))
    g = -jnp.exp(alog_ref[...]) * softplus
    ri = lax.broadcasted_iota(jnp.int32, (ts, ts), 0)
    ci = lax.broadcasted_iota(jnp.int32, (ts, ts), 1)
    same = (ri // GDN_BLOCK) == (ci // GDN_BLOCK)
    gc = _sel_dot(jnp.where(same & (ci <= ri), 1.0, 0.0), g)
    gl = _sel_dot(jnp.where(same, 1.0, 0.0), g)
    out_ref[...] = jnp.concatenate([beta, gc, gl, jnp.zeros((ts, LANES - 3 * nh), F32)], axis=1)


def _gdn_gates(ba, a_log, dt_bias, *, ts=512):
    s, two_h = ba.shape
    nh = two_h // 2
    assert 3 * nh <= LANES
    return pl.pallas_call(
        _gdn_gates_kernel,
        grid=(s // ts,),
        in_specs=[pl.BlockSpec((ts, two_h), lambda i: (i, 0)),
                  pl.BlockSpec((1, nh), lambda i: (0, 0)),
                  pl.BlockSpec((1, nh), lambda i: (0, 0))],
        out_specs=pl.BlockSpec((ts, LANES), lambda i: (i, 0)),
        out_shape=jax.ShapeDtypeStruct((s, LANES), F32),
        compiler_params=_params("parallel"),
        name="gdn_gates",
    )(ba, a_log.reshape(1, nh), dt_bias.reshape(1, nh))


def _bdot(a, b):
    return jnp.einsum("hij,hjk->hik", a.astype(BF16), b.astype(BF16), preferred_element_type=F32)


def _bdot3(a, b):
    ah, al = _split(a)
    bh, bl = _split(b)
    return jnp.einsum("hij,hjk->hik", jnp.concatenate([ah, ah, al], axis=2),
                      jnp.concatenate([bh, bl, bh], axis=1), preferred_element_type=F32)


def _unit_lower_inverse(low, ri, ci):
    n = low.shape[-1]

    def blk(b):
        return ((ri // b) == (ci // b))[None]

    eye = jnp.where(ri == ci, 1.0, 0.0)[None]
    n1 = jnp.where(blk(16), -low, 0.0)
    n2 = _bdot(n1, n1)
    n4 = _bdot(n2, n2)
    n8 = _bdot(n4, n4)
    t = eye + n1
    t = t + _bdot(n2, t)
    t = t + _bdot(n4, t)
    t = t + _bdot(n8, t)
    b = 32
    while b <= n:
        off = jnp.where(blk(b) & jnp.logical_not(blk(b // 2)), low, 0.0)
        t = t - _bdot(t, _bdot(off, t))
        b *= 2
    return t


def _gdn_chunk_kernel(q_ref, k_ref, v_ref, gates_ref, gct_ref, o_ref, state, *, heads):
    hg = pl.program_id(0)
    rows = q_ref.shape[0]
    nh = gct_ref.shape[0]
    dh = HEAD_DIM
    rep = rows // dh

    @pl.when(pl.program_id(1) == 0)
    def _():
        state[...] = jnp.zeros_like(state)

    ri = lax.broadcasted_iota(jnp.int32, (rows, rows), 0)
    ci = lax.broadcasted_iota(jnp.int32, (rows, rows), 1)
    lower = (ci <= ri)[None]
    strict = (ci < ri)[None]
    dot = functools.partial(jnp.dot, preferred_element_type=F32)

    gates_hi, gates_lo = _split(gates_ref[...])
    width = heads * 3 * dh
    prow = lax.broadcasted_iota(jnp.int32, (LANES, width), 0)
    pcol = lax.broadcasted_iota(jnp.int32, (LANES, width), 1)
    pick = jnp.where(prow == ((pcol // dh) % 3) * nh + hg * heads + pcol // (3 * dh), 1.0, 0.0).astype(BF16)
    cols = dot(jnp.concatenate([gates_hi, gates_lo], axis=1),
               jnp.concatenate([pick, pick], axis=0))

    def per_head(j):
        return jnp.stack([cols[:, (3 * hl + j) * dh:(3 * hl + j + 1) * dh] for hl in range(heads)])

    beta, gcol, glast = per_head(0), per_head(1), per_head(2)
    gct_hi, gct_lo = _split(gct_ref[...])
    trow = lax.broadcasted_iota(jnp.int32, (heads * rows, nh), 0) // rows
    tcol = lax.broadcasted_iota(jnp.int32, (heads * rows, nh), 1)
    pick_t = jnp.where(tcol == hg * heads + trow, 1.0, 0.0).astype(BF16)
    g_j = dot(jnp.concatenate([pick_t, pick_t], axis=1),
              jnp.concatenate([gct_hi, gct_lo], axis=0)).reshape(heads, rows, rows)
    g_i = jnp.concatenate([gcol] * rep, axis=2)
    beta_i = jnp.concatenate([beta] * rep, axis=2)
    decay = jnp.where(lower, jnp.exp(jnp.where(lower, g_i - g_j, 0.0)), 0.0)

    def pairs(ref):
        return [ref[:, p * dh:(p + 1) * dh] for p in range(heads // 2)]

    def nt(a, b):
        return jnp.einsum("pid,pjd->pij", a.astype(BF16), b.astype(BF16), preferred_element_type=F32)

    def both(x):
        return jnp.stack([x[hl // 2] for hl in range(heads)])

    qp, kp = jnp.stack(pairs(q_ref)), jnp.stack(pairs(k_ref))
    q, k = both(qp), both(kp)
    kk, qk = both(nt(kp, kp)), both(nt(qp, kp))
    v = jnp.stack([v_ref[:, hl * dh:(hl + 1) * dh] for hl in range(heads)])

    low = jnp.where(strict, kk * beta_i * decay, 0.0)
    t0 = _unit_lower_inverse(low, ri, ci)
    rhs = jnp.concatenate([v * beta, k * (beta * jnp.exp(gcol))], axis=2)
    x = _bdot(t0, rhs)
    x = x + _bdot(t0, rhs - x - _bdot3(low, x))
    u, w = x[:, :, :dh], x[:, :, dh:]
    a_x = _bdot(jnp.where(lower, qk * decay, 0.0), x)
    q_eff = q * jnp.exp(gcol) - a_x[:, :, dh:]
    k_g = k * jnp.exp(glast - gcol)

    s = state[...]
    o = _bdot(q_eff, s) + a_x[:, :, :dh]
    v_new = u - _bdot(w, s)
    state[...] = s * jnp.exp(glast[:, 0:1, :]) + jnp.einsum(
        "hck,hcv->hkv", k_g.astype(BF16), v_new.astype(BF16), preferred_element_type=F32)
    for hl in range(heads):
        o_ref[:, hl * dh:(hl + 1) * dh] = o[hl]


def _gdn_chunk(q, k, v, gates, gct, *, heads=8):
    rows = GDN_BLOCK
    s, dv_total = v.shape
    nh = dv_total // HEAD_DIM
    assert nh == 2 * (q.shape[1] // HEAD_DIM) and heads % 2 == 0 and nh % heads == 0
    qk_spec = pl.BlockSpec((rows, heads // 2 * HEAD_DIM), lambda h, i: (i, h))
    v_spec = pl.BlockSpec((rows, heads * HEAD_DIM), lambda h, i: (i, h))
    return pl.pallas_call(
        functools.partial(_gdn_chunk_kernel, heads=heads),
        grid=(nh // heads, s // rows),
        in_specs=[qk_spec, qk_spec, v_spec,
                  pl.BlockSpec((rows, LANES), lambda h, i: (i, 0)),
                  pl.BlockSpec((nh, rows), lambda h, i: (0, i))],
        out_specs=v_spec,
        out_shape=jax.ShapeDtypeStruct((s, dv_total), F32),
        scratch_shapes=[pltpu.VMEM((heads, HEAD_DIM, HEAD_DIM), F32)],
        compiler_params=_params("parallel", "arbitrary"),
        name="gdn_chunk",
    )(q, k, v, gates, gct)


def _gdn_post_kernel(o_ref, z_ref, nw_ref, out_ref):
    nw = nw_ref[...]
    for hh in range(o_ref.shape[1] // HEAD_DIM):
        sl = slice(hh * HEAD_DIM, (hh + 1) * HEAD_DIM)
        o = o_ref[:, sl]
        ms = jnp.mean(o * o, axis=-1, keepdims=True)
        out_ref[:, sl] = ((o * lax.rsqrt(ms + NORM_EPS)) * nw * _silu(z_ref[:, sl])).astype(out_ref.dtype)


def _gdn_post(o, proj, z_col0, norm_w, *, ts=256):
    s, dv = o.shape
    zb = z_col0 // dv
    return pl.pallas_call(
        _gdn_post_kernel,
        grid=(s // ts,),
        in_specs=[pl.BlockSpec((ts, dv), lambda i: (i, 0)),
                  pl.BlockSpec((ts, dv), lambda i: (i, zb)),
                  pl.BlockSpec((1, HEAD_DIM), lambda i: (0, 0))],
        out_specs=pl.BlockSpec((ts, dv), lambda i: (i, 0)),
        out_shape=jax.ShapeDtypeStruct((s, dv), BF16),
        compiler_params=_params("parallel"),
        name="gdn_post",
    )(o, proj, norm_w.reshape(1, HEAD_DIM))


def _gated_deltanet(x, sc, sh, w_in, conv_w, layer, a_log, dt_bias, norm_w):
    nvh = a_log.shape[0]
    dv = nvh * HEAD_DIM
    conv_ch = conv_w.shape[2]
    dqk = (conv_ch - dv) // 2
    n_main = conv_ch + dv
    proj = _mod_matmul(x, sc, sh, w_in, layer, 0, n_main, name="gdn_in_proj")
    ba = _mod_matmul(x, sc, sh, w_in[layer, :, n_main:][None], 0, 0, 2 * nvh, name="gdn_in_gates")
    q = _gdn_conv(proj, conv_w, layer, 0, dqk, normalize=True, scale=HEAD_DIM ** -0.5)
    k = _gdn_conv(proj, conv_w, layer, dqk, dqk, normalize=True, scale=1.0)
    v = _gdn_conv(proj, conv_w, layer, 2 * dqk, dv, normalize=False, scale=1.0)
    gates = _gdn_gates(ba, a_log, dt_bias)
    o = _gdn_chunk(q, k, v, gates, gates[:, nvh:2 * nvh].T)
    return _gdn_post(o, proj, conv_ch, norm_w)


def _rope(x, c, sg):
    return x * c + pltpu.roll(x, HEAD_DIM // 2, 1) * sg


def _nsa_prep_kernel(p_ref, c_ref, s_ref, q_ref, kvc_ref, ks_ref, vs_ref, kw_ref, vw_ref, *, dq, dkv):
    c = c_ref[...]
    sg = s_ref[...]
    dh = HEAD_DIM

    def head(col0, hh, rope, mul):
        x = p_ref[:, col0 + hh * dh:col0 + (hh + 1) * dh]
        if rope:
            x = _rope(x, c, sg)
        return x if mul == 1.0 else x * mul

    for hh in range(dq // dh):
        q_ref[:, hh * dh:(hh + 1) * dh] = head(0, hh, True, dh ** -0.5 * LOG2E).astype(q_ref.dtype)
    for hh in range(dkv // dh):
        sl = slice(hh * dh, (hh + 1) * dh)
        kvc_ref[0, hh] = head(dq, hh, True, 1.0)
        kvc_ref[1, hh] = head(dq + dkv, hh, False, 1.0)
        ks_ref[:, sl] = head(dq + 2 * dkv, hh, True, 1.0).astype(ks_ref.dtype)
        vs_ref[:, sl] = head(dq + 3 * dkv, hh, False, 1.0).astype(vs_ref.dtype)
        kw_ref[:, sl] = head(dq + 4 * dkv, hh, True, 1.0).astype(kw_ref.dtype)
        vw_ref[:, sl] = head(dq + 5 * dkv, hh, False, 1.0).astype(vw_ref.dtype)


def _nsa_prep(proj, cos2, sin2, dq, dkv, *, ts=256):
    s, n = proj.shape
    hk = dkv // HEAD_DIM
    kv_spec = pl.BlockSpec((ts, dkv), lambda i: (i, 0))
    tab = pl.BlockSpec((ts, HEAD_DIM), lambda i: (i, 0))
    kv16 = jax.ShapeDtypeStruct((s, dkv), BF16)
    return pl.pallas_call(
        functools.partial(_nsa_prep_kernel, dq=dq, dkv=dkv),
        grid=(s // ts,),
        in_specs=[pl.BlockSpec((ts, n), lambda i: (i, 0)), tab, tab],
        out_specs=[pl.BlockSpec((ts, dq), lambda i: (i, 0)),
                   pl.BlockSpec((2, hk, ts, HEAD_DIM), lambda i: (0, 0, i, 0)),
                   kv_spec, kv_spec, kv_spec, kv_spec],
        out_shape=[jax.ShapeDtypeStruct((s, dq), BF16),
                   jax.ShapeDtypeStruct((2, hk, s, HEAD_DIM), F32),
                   kv16, kv16, kv16, kv16],
        compiler_params=_params("parallel"),
        name="nsa_prep",
    )(proj, cos2, sin2)


def _nsa_compress_kernel(x_ref, pe_ref, w1_ref, w2_ref, o_ref):
    x = x_ref[0, 0]
    n = x.shape[0]
    a = _dot(x + pe_ref[0, 0:1, :], w1_ref[0, 0])
    b = _dot(x + pe_ref[0, 1:2, :], w1_ref[0, 1])
    y = a + pltpu.roll(b, n - 1, 0)
    o_ref[0, 0] = _dot(_silu(y), w2_ref[0])


def _nsa_compress(x, pe, w1, w2):
    _, hk, ng, gw = x.shape
    half = CMP_LEN // 2
    pe2 = pe.reshape(2, 2, half * HEAD_DIM)
    w1r = w1.reshape(2, 2, half * HEAD_DIM, HEAD_DIM)
    return pl.pallas_call(
        _nsa_compress_kernel,
        grid=(2, hk),
        in_specs=[pl.BlockSpec((1, 1, ng, gw), lambda t, h: (t, h, 0, 0)),
                  pl.BlockSpec((1, 2, gw), lambda t, h: (t, 0, 0)),
                  pl.BlockSpec((1, 2, gw, HEAD_DIM), lambda t, h: (t, 0, 0, 0)),
                  pl.BlockSpec((1, HEAD_DIM, HEAD_DIM), lambda t, h: (t, 0, 0))],
        out_specs=pl.BlockSpec((1, 1, ng, HEAD_DIM), lambda t, h: (t, h, 0, 0)),
        out_shape=jax.ShapeDtypeStruct((2, hk, ng, HEAD_DIM), F32),
        compiler_params=_params("parallel", "parallel"),
        name="nsa_compress",
    )(x, pe2, w1r, w2)


def _nsa_cmp_select_kernel(q_ref, kc_ref, vc_ref, ov_ref, o_ref, sel_ref):
    tq = q_ref.shape[0]
    nc = kc_ref.shape[2]
    nbp = ov_ref.shape[1]
    t0 = pl.program_id(0) * tq
    tpos = t0 + lax.broadcasted_iota(jnp.int32, (tq, nc), 0)
    nidx = lax.broadcasted_iota(jnp.int32, (tq, nc), 1)
    valid = (nidx * CMP_STRIDE + (CMP_LEN - 1)) <= tpos
    kc = kc_ref[0, 0].astype(BF16)
    vc = vc_ref[0, 0].astype(BF16)
    psum = jnp.zeros((tq, nc), F32)
    for g in range(NSA_GROUP):
        sl = slice(g * HEAD_DIM, (g + 1) * HEAD_DIM)
        s = jnp.where(valid, _dot_nt(q_ref[:, sl], kc), NEG_BIG)
        m = jnp.max(s, axis=-1, keepdims=True)
        p = jnp.where(valid, jnp.exp2(s - m), 0.0)
        den = jnp.sum(p, axis=-1, keepdims=True)
        p = p / jnp.where(den > 0.0, den, 1.0)
        o_ref[:, sl] = jnp.dot(p.astype(BF16), vc, preferred_element_type=F32)
        psum = psum + p
    imp = _dot_sel(psum, ov_ref[...])
    j = lax.broadcasted_iota(jnp.int32, (tq, nbp), 1)
    cur = (t0 + lax.broadcasted_iota(jnp.int32, (tq, nbp), 0)) // SLC_LEN
    forced = (j == 0) | (j == cur) | (j == cur - 1)
    work = jnp.where(j <= cur, jnp.where(forced, jnp.inf, imp), -jnp.inf)
    sel = jnp.zeros((tq, nbp), F32)
    for _ in range(SLC_TOP):
        m = jnp.max(work, axis=-1, keepdims=True)
        first = jnp.min(jnp.where(work == m, j, nbp), axis=-1, keepdims=True)
        hit = j == first
        sel = jnp.where(hit, 1.0, sel)
        work = jnp.where(hit, -jnp.inf, work)
    sel_ref[0] = jnp.where(j <= cur, sel, 0.0).astype(sel_ref.dtype)


def _nsa_cmp_select(q, cmp, ov, *, tq=512):
    s, dq = q.shape
    _, hk, nc, _ = cmp.shape
    nbp = ov.shape[1]
    gw = NSA_GROUP * HEAD_DIM
    return pl.pallas_call(
        _nsa_cmp_select_kernel,
        grid=(s // tq, hk),
        in_specs=[pl.BlockSpec((tq, gw), lambda i, h: (i, h)),
                  pl.BlockSpec((1, 1, nc, HEAD_DIM), lambda i, h: (0, h, 0, 0)),
                  pl.BlockSpec((1, 1, nc, HEAD_DIM), lambda i, h: (1, h, 0, 0)),
                  pl.BlockSpec((nc, nbp), lambda i, h: (0, 0))],
        out_specs=[pl.BlockSpec((tq, gw), lambda i, h: (i, h)),
                   pl.BlockSpec((1, tq, nbp), lambda i, h: (h, i, 0))],
        out_shape=[jax.ShapeDtypeStruct((s, dq), F32),
                   jax.ShapeDtypeStruct((hk, s, nbp), BF16)],
        compiler_params=_params("parallel", "parallel"),
        name="nsa_cmp_select",
    )(q, cmp, cmp, ov)


ATTN_ROWS = 64


def _masked_attn_kernel(*refs, tq, tk, selected):
    if selected:
        (qi_ref, kj_ref, first_ref, last_ref, q_ref, k_ref, v_ref, sel_ref, o_ref,
         q4, bias, s_scr, p_scr, a_scr, m_scr, acc) = refs
    else:
        (qi_ref, kj_ref, first_ref, last_ref, q_ref, k_ref, v_ref, o_ref,
         q4, bias, s_scr, p_scr, a_scr, m_scr, acc) = refs
    t = pl.program_id(1)
    qi = qi_ref[t]
    kj = kj_ref[t]
    dh = HEAD_DIM

    @pl.when(first_ref[t] == 1)
    def _():
        for g in range(NSA_GROUP):
            q4[g * tq:(g + 1) * tq, :] = q_ref[:, g * dh:(g + 1) * dh]
        m_scr[...] = jnp.full_like(m_scr, NEG_BIG)
        acc[...] = jnp.zeros_like(acc)

    tpos = qi * tq + lax.broadcasted_iota(jnp.int32, (tq, tk), 0)
    kpos = kj * tk + lax.broadcasted_iota(jnp.int32, (tq, tk), 1)
    if selected:
        nbp = sel_ref.shape[2]
        expand = jnp.where(lax.broadcasted_iota(jnp.int32, (nbp, tk), 0)
                           == (kj * tk + lax.broadcasted_iota(jnp.int32, (nbp, tk), 1)) // SLC_LEN, 1.0, 0.0)
        picked = jnp.dot(sel_ref[0], expand.astype(BF16), preferred_element_type=F32)
        mask = (picked > 0.5) & (kpos <= tpos)
    else:
        mask = (kpos <= tpos) & (kpos > tpos - WINDOW)
    bias[...] = jnp.where(mask, 0.0, NEG_BIG)

    s_scr[...] = lax.dot_general(q4[...], k_ref[...], (((1,), (1,)), ((), ())), preferred_element_type=F32)
    rb = ATTN_ROWS
    for r0 in range(0, NSA_GROUP * tq, rb):
        rows = slice(r0, r0 + rb)
        brow = r0 % tq
        s = s_scr[rows, :] + bias[brow:brow + rb, :]
        m_prev = m_scr[rows, :]
        m_new = jnp.maximum(m_prev, jnp.max(s, axis=-1, keepdims=True))
        a_scr[rows, :] = jnp.exp2(m_prev - m_new)
        p_scr[rows, :] = jnp.exp2(s - jnp.concatenate([m_new] * (tk // LANES), axis=1)).astype(BF16)
        m_scr[rows, :] = m_new
    v_one = jnp.concatenate([v_ref[...], jnp.ones((tk, dh), BF16)], axis=1)
    alpha = jnp.concatenate([a_scr[...]] * 2, axis=1)
    acc[...] = alpha * acc[...] + jnp.dot(p_scr[...], v_one, preferred_element_type=F32)

    @pl.when(last_ref[t] == 1)
    def _():
        for g in range(NSA_GROUP):
            rows = slice(g * tq, (g + 1) * tq)
            o_ref[:, g * dh:(g + 1) * dh] = acc[rows, :dh] / acc[rows, dh:]


def _masked_attn(q, k, v, sel, *, tq, tk):
    s, dq = q.shape
    hk = k.shape[1] // HEAD_DIM
    gw = NSA_GROUP * HEAD_DIM
    selected = sel is not None
    pairs = []
    for i in range(s // tq):
        lo = 0 if selected else max(i * tq - (WINDOW - 1), 0) // tk
        hi = (i * tq + tq - 1) // tk
        pairs += [(i, j, int(j == lo), int(j == hi)) for j in range(lo, hi + 1)]
    sched = [jnp.asarray([p[c] for p in pairs], jnp.int32) for c in range(4)]

    in_specs = [pl.BlockSpec((tq, gw), lambda h, t, qi, kj, fi, la: (qi[t], h)),
                pl.BlockSpec((tk, HEAD_DIM), lambda h, t, qi, kj, fi, la: (kj[t], h)),
                pl.BlockSpec((tk, HEAD_DIM), lambda h, t, qi, kj, fi, la: (kj[t], h))]
    args = [q, k, v]
    if selected:
        in_specs.append(pl.BlockSpec((1, tq, sel.shape[2]), lambda h, t, qi, kj, fi, la: (h, qi[t], 0)))
        args.append(sel)
    rows = NSA_GROUP * tq
    return pl.pallas_call(
        functools.partial(_masked_attn_kernel, tq=tq, tk=tk, selected=selected),
        grid_spec=pltpu.PrefetchScalarGridSpec(
            num_scalar_prefetch=4,
            grid=(hk, len(pairs)),
            in_specs=in_specs,
            out_specs=pl.BlockSpec((tq, gw), lambda h, t, qi, kj, fi, la: (qi[t], h)),
            scratch_shapes=[pltpu.VMEM((rows, HEAD_DIM), BF16),
                            pltpu.VMEM((tq, tk), F32),
                            pltpu.VMEM((rows, tk), F32),
                            pltpu.VMEM((rows, tk), BF16),
                            pltpu.VMEM((rows, LANES), F32),
                            pltpu.VMEM((rows, LANES), F32),
                            pltpu.VMEM((rows, 2 * HEAD_DIM), F32)]),
        out_shape=jax.ShapeDtypeStruct((s, dq), F32),
        compiler_params=_params("parallel", "arbitrary"),
        name="nsa_selected_attn" if selected else "nsa_window_attn",
    )(*sched, *args)


def _nsa_combine_kernel(oc_ref, os_ref, ow_ref, g_ref, out_ref):
    gates = jax.nn.sigmoid(g_ref[...])
    ng, dq = g_ref.shape[1], oc_ref.shape[1]
    row = lax.broadcasted_iota(jnp.int32, (ng, dq), 0)
    head3 = (lax.broadcasted_iota(jnp.int32, (ng, dq), 1) // HEAD_DIM) * N_BRANCH
    acc = None
    for b, ref in enumerate((oc_ref, os_ref, ow_ref)):
        gb = _dot_sel(gates, jnp.where(row == head3 + b, 1.0, 0.0))
        term = gb * ref[...]
        acc = term if acc is None else acc + term
    out_ref[...] = acc.astype(out_ref.dtype)


def _nsa_combine(oc, osel, ow, gates, *, ts=256):
    s, dq = oc.shape
    spec = pl.BlockSpec((ts, dq), lambda i: (i, 0))
    return pl.pallas_call(
        _nsa_combine_kernel,
        grid=(s // ts,),
        in_specs=[spec, spec, spec, pl.BlockSpec((ts, gates.shape[1]), lambda i: (i, 0))],
        out_specs=spec,
        out_shape=jax.ShapeDtypeStruct((s, dq), BF16),
        compiler_params=_params("parallel"),
        name="nsa_combine",
    )(oc, osel, ow, gates)


def _nsa_attention(x, sc, sh, cos2, sin2, w_in, layer, cmp_pe, cmp_w1, cmp_w2):
    s, d = x.shape
    dq = d
    hk = d // HEAD_DIM // NSA_GROUP
    dkv = hk * HEAD_DIM
    n_main = dq + 6 * dkv
    ngate = w_in.shape[2] - n_main
    proj = _mod_matmul(x, sc, sh, w_in, layer, 0, n_main, name="nsa_in_proj")
    w_gate = jnp.pad(w_in[layer, :, n_main:], ((0, 0), (0, LANES - ngate)))[None]
    gates = _mod_matmul(x, sc, sh, w_gate, 0, 0, LANES, name="nsa_in_gates")
    q, kvc, ks, vs, kw, vw = _nsa_prep(proj, cos2, sin2, dq, dkv)
    groups = kvc.reshape(2, hk, s // CMP_STRIDE, CMP_STRIDE * HEAD_DIM)
    cmp = _nsa_compress(groups, cmp_pe, cmp_w1, cmp_w2)

    nc = s // CMP_STRIDE
    n_slc = s // SLC_LEN
    nbp = -(-n_slc // LANES) * LANES
    c_start = CMP_STRIDE * jnp.arange(nc)
    s_start = SLC_LEN * jnp.arange(nbp)
    overlap = jnp.clip(jnp.minimum(c_start[:, None] + CMP_LEN, s_start[None, :] + SLC_LEN)
                       - jnp.maximum(c_start[:, None], s_start[None, :]), 0, None).astype(F32) / CMP_LEN
    o_c, sel = _nsa_cmp_select(q, cmp, overlap.astype(BF16))
    o_s = _masked_attn(q, ks, vs, sel, tq=256, tk=512)
    o_w = _masked_attn(q, kw, vw, None, tq=256, tk=256)
    return _nsa_combine(o_c, o_s, o_w, gates)


def kernel(x, c, positions, mod_w, mod_b, ln_g, ln_b, ffn_w_gu, ffn_w_down, gdn_w_in, gdn_conv_w,
           gdn_a_log, gdn_dt_bias, gdn_norm_w, gdn_w_out, nsa_w_in, nsa_cmp_pe, nsa_cmp_w1,
           nsa_cmp_w2, nsa_w_out):
    bsz, s, d = x.shape
    assert bsz == 1
    depth = mod_w.shape[0]
    alpha = (2 * depth) ** 0.25

    inv = ROPE_THETA ** (-jnp.arange(0, HEAD_DIM, 2, dtype=F32) / HEAD_DIM)
    ang = positions[0].astype(F32)[:, None] * inv
    cos, sin = jnp.cos(ang), jnp.sin(ang)
    cos2 = jnp.concatenate([cos, cos], axis=-1)
    sin2 = jnp.concatenate([-sin, sin], axis=-1)

    mod = _modulation(c, mod_w, mod_b)
    xs = x[0]
    for i in range(depth):
        sh1, sc1, ga1, sh2, sc2, ga2 = [mod[i, :, r * d:(r + 1) * d] for r in range(6)]
        j = i // N_MIXERS
        if i % N_MIXERS == 0:
            y = _gated_deltanet(xs, sc1, sh1, gdn_w_in, gdn_conv_w, j, gdn_a_log[j],
                                gdn_dt_bias[j], gdn_norm_w[j])
            w_out = gdn_w_out
        else:
            y = _nsa_attention(xs, sc1, sh1, cos2, sin2, nsa_w_in, j, nsa_cmp_pe[j],
                               nsa_cmp_w1[j], nsa_cmp_w2[j])
            w_out = nsa_w_out
        xs = _matmul_ln(y, w_out, j, xs, ga1, ln_g[i, 0:1], ln_b[i, 0:1], alpha=alpha)
        a = _swiglu_up(xs, sc2, sh2, ffn_w_gu, i)
        xs = _matmul_ln(a, ffn_w_down, i, xs, ga2, ln_g[i, 1:2], ln_b[i, 1:2], alpha=alpha)
    return xs[None]
```

```python
import functools

import jax
import jax.numpy as jnp
from jax import lax
from jax.experimental import pallas as pl
from jax.experimental.pallas import tpu as pltpu

F32 = jnp.float32
BF16 = jnp.bfloat16
HIGHEST = lax.Precision.HIGHEST

HEAD_DIM = 128
N_MIXERS = 2
GDN_CONV = 4
GDN_BLOCK = 128
NSA_GROUP = 4
N_BRANCH = 3
CMP_LEN = 32
CMP_STRIDE = 16
SLC_LEN = 64
SLC_TOP = 16
WINDOW = 512
ROPE_THETA = 10000.0
LN_EPS = 1e-5
NORM_EPS = 1e-6

LOG2E = 1.4426950408889634
MATMUL_ROWS = 1024
LANES = 128
VMEM_LIMIT_BYTES = 56 * 1024 * 1024
NEG_BIG = -1e30


def _params(*sem):
    return pltpu.CompilerParams(dimension_semantics=sem, vmem_limit_bytes=VMEM_LIMIT_BYTES)


def _silu(v):
    return v * jax.nn.sigmoid(v)


def _dot(a, b):
    return jnp.dot(a.astype(BF16), b.astype(BF16), preferred_element_type=F32)


def _dot_nt(a, b):
    return lax.dot_general(a.astype(BF16), b.astype(BF16), (((1,), (1,)), ((), ())),
                           preferred_element_type=F32)


def _split(a):
    hi = a.astype(BF16)
    lo = (a - hi.astype(F32)).astype(BF16)
    return hi, lo


def _dot3(a, b):
    ah, al = _split(a)
    bh, bl = _split(b)
    d = functools.partial(jnp.dot, preferred_element_type=F32)
    return d(ah, bh) + (d(ah, bl) + d(al, bh))


def _dot_sel(a, onehot):
    ah, al = _split(a)
    e = onehot.astype(BF16)
    d = functools.partial(jnp.dot, preferred_element_type=F32)
    return d(ah, e) + d(al, e)


def _sel_dot(onehot, b):
    bh, bl = _split(b)
    e = onehot.astype(BF16)
    d = functools.partial(jnp.dot, preferred_element_type=F32)
    return d(e, bh) + d(e, bl)


def _mod_kernel(c_ref, w_ref, b_ref, o_ref):
    cond = _silu(c_ref[...])
    o_ref[0] = _dot3(cond, w_ref[0]) + b_ref[0]


def _modulation(c, mod_w, mod_b):
    depth, d, n = mod_w.shape
    tn = 1536
    c8 = jnp.broadcast_to(c[:1], (8, d))
    out = pl.pallas_call(
        _mod_kernel,
        grid=(depth, n // tn),
        in_specs=[pl.BlockSpec((8, d), lambda l, j: (0, 0)),
                  pl.BlockSpec((1, d, tn), lambda l, j: (l, 0, j)),
                  pl.BlockSpec((1, 1, tn), lambda l, j: (l, 0, j))],
        out_specs=pl.BlockSpec((1, 8, tn), lambda l, j: (l, 0, j)),
        out_shape=jax.ShapeDtypeStruct((depth, 8, n), F32),
        compiler_params=_params("parallel", "parallel"),
        name="modulation",
    )(c8, mod_w, mod_b.reshape(depth, 1, n))
    return out[:, 0:1, :]


def _mod_matmul_kernel(x_ref, sc_ref, sh_ref, w_ref, o_ref, h_scr):
    @pl.when(pl.program_id(1) == 0)
    def _():
        h_scr[...] = (x_ref[...] * (1.0 + sc_ref[...]) + sh_ref[...]).astype(BF16)

    o_ref[...] = jnp.dot(h_scr[...], w_ref[...].astype(BF16),
                         preferred_element_type=F32).astype(o_ref.dtype)


def _mod_matmul(x, sc, sh, w, layer, col0, n, *, tm=MATMUL_ROWS, tn=512, name="mod_matmul"):
    m, k = x.shape
    tm = min(tm, m)
    tn = min(tn, n)
    cb0 = col0 // tn
    assert col0 % tn == 0 and n % tn == 0 and m % tm == 0
    return pl.pallas_call(
        _mod_matmul_kernel,
        grid=(m // tm, n // tn),
        in_specs=[pl.BlockSpec((tm, k), lambda i, j: (i, 0)),
                  pl.BlockSpec((1, k), lambda i, j: (0, 0)),
                  pl.BlockSpec((1, k), lambda i, j: (0, 0)),
                  pl.BlockSpec((None, k, tn), lambda i, j: (layer, 0, j + cb0))],
        out_specs=pl.BlockSpec((tm, tn), lambda i, j: (i, j)),
        out_shape=jax.ShapeDtypeStruct((m, n), F32),
        scratch_shapes=[pltpu.VMEM((tm, k), BF16)],
        compiler_params=_params("parallel", "arbitrary"),
        name=name,
    )(x, sc, sh, w)


def _swiglu_up_kernel(x_ref, sc_ref, sh_ref, wg_ref, wu_ref, o_ref, h_scr):
    @pl.when(pl.program_id(1) == 0)
    def _():
        h_scr[...] = (x_ref[...] * (1.0 + sc_ref[...]) + sh_ref[...]).astype(BF16)

    h = h_scr[...]
    g = jnp.dot(h, wg_ref[...].astype(BF16), preferred_element_type=F32)
    u = jnp.dot(h, wu_ref[...].astype(BF16), preferred_element_type=F32)
    o_ref[...] = (_silu(g) * u).astype(o_ref.dtype)


def _swiglu_up(x, sc, sh, w_gu, layer, *, tm=MATMUL_ROWS, tn=512):
    m, k = x.shape
    tm = min(tm, m)
    dff = w_gu.shape[2] // 2
    nb = dff // tn
    assert dff % tn == 0 and m % tm == 0
    return pl.pallas_call(
        _swiglu_up_kernel,
        grid=(m // tm, nb),
        in_specs=[pl.BlockSpec((tm, k), lambda i, j: (i, 0)),
                  pl.BlockSpec((1, k), lambda i, j: (0, 0)),
                  pl.BlockSpec((1, k), lambda i, j: (0, 0)),
                  pl.BlockSpec((None, k, tn), lambda i, j: (layer, 0, j)),
                  pl.BlockSpec((None, k, tn), lambda i, j: (layer, 0, j + nb))],
        out_specs=pl.BlockSpec((tm, tn), lambda i, j: (i, j)),
        out_shape=jax.ShapeDtypeStruct((m, dff), BF16),
        scratch_shapes=[pltpu.VMEM((tm, k), BF16)],
        compiler_params=_params("parallel", "arbitrary"),
        name="swiglu_up",
    )(x, sc, sh, w_gu, w_gu)


LN_ROWS = 256
LN_MAX_TK = 1536


def _matmul_ln_kernel(a_ref, w_ref, x_ref, ga_ref, g_ref, b_ref, o_ref, *, alpha):
    kk = pl.program_id(1)
    part = jnp.dot(a_ref[...].astype(BF16), w_ref[...].astype(BF16), preferred_element_type=F32)

    @pl.when(kk == 0)
    def _():
        o_ref[...] = part

    @pl.when(kk > 0)
    def _():
        o_ref[...] += part

    @pl.when(kk == pl.num_programs(1) - 1)
    def _():
        for r0 in range(0, o_ref.shape[0], LN_ROWS):
            rows = slice(r0, r0 + LN_ROWS)
            r = alpha * x_ref[rows, :] + (1.0 + ga_ref[...]) * o_ref[rows, :]
            mu = jnp.mean(r, axis=-1, keepdims=True)
            d = r - mu
            var = jnp.mean(d * d, axis=-1, keepdims=True)
            o_ref[rows, :] = d * lax.rsqrt(var + LN_EPS) * g_ref[...] + b_ref[...]


def _matmul_ln(a, w, layer, x, ga, g, b, *, alpha, tm=512):
    m, k = a.shape
    n = w.shape[2]
    tm = min(tm, m)
    tk = next(t for t in range(min(k, LN_MAX_TK), 0, -LANES) if k % t == 0)
    assert m % tm == 0 and k % tk == 0 and tm % LN_ROWS == 0
    return pl.pallas_call(
        functools.partial(_matmul_ln_kernel, alpha=alpha),
        grid=(m // tm, k // tk),
        in_specs=[pl.BlockSpec((tm, tk), lambda i, kk: (i, kk)),
                  pl.BlockSpec((None, tk, n), lambda i, kk: (layer, kk, 0)),
                  pl.BlockSpec((tm, n), lambda i, kk: (i, 0)),
                  pl.BlockSpec((1, n), lambda i, kk: (0, 0)),
                  pl.BlockSpec((1, n), lambda i, kk: (0, 0)),
                  pl.BlockSpec((1, n), lambda i, kk: (0, 0))],
        out_specs=pl.BlockSpec((tm, n), lambda i, kk: (i, 0)),
        out_shape=jax.ShapeDtypeStruct((m, n), F32),
        compiler_params=_params("parallel", "arbitrary"),
        name="matmul_ln",
    )(a, w, x, ga, g, b)


def _gdn_conv_kernel(xp_ref, xc_ref, w_ref, o_ref, buf, *, normalize, scale):
    ts, tc = xc_ref.shape
    buf[0:8, :] = jnp.where(pl.program_id(0) > 0, xp_ref[...], 0.0)
    buf[8:, :] = xc_ref[...]
    w = w_ref[...]
    y = w[0:1, :] * buf[5:5 + ts, :]
    for i in range(1, GDN_CONV):
        y = y + w[i:i + 1, :] * buf[5 + i:5 + i + ts, :]
    y = _silu(y)
    if normalize:
        for hh in range(tc // HEAD_DIM):
            seg = y[:, hh * HEAD_DIM:(hh + 1) * HEAD_DIM]
            ss = jnp.sum(seg * seg, axis=-1, keepdims=True)
            o_ref[:, hh * HEAD_DIM:(hh + 1) * HEAD_DIM] = seg * (lax.rsqrt(ss + NORM_EPS) * scale)
    else:
        o_ref[...] = y


def _gdn_conv(proj, conv_w, layer, col0, ncols, *, normalize, scale, ts=256, tc=512):
    s = proj.shape[0]
    cb0 = col0 // tc
    hb = ts // 8
    return pl.pallas_call(
        functools.partial(_gdn_conv_kernel, normalize=normalize, scale=scale),
        grid=(s // ts, ncols // tc),
        in_specs=[pl.BlockSpec((8, tc), lambda i, j: (jnp.maximum(i * hb - 1, 0), j + cb0)),
                  pl.BlockSpec((ts, tc), lambda i, j: (i, j + cb0)),
                  pl.BlockSpec((None, GDN_CONV, tc), lambda i, j: (layer, 0, j + cb0))],
        out_specs=pl.BlockSpec((ts, tc), lambda i, j: (i, j)),
        out_shape=jax.ShapeDtypeStruct((s, ncols), F32),
        scratch_shapes=[pltpu.VMEM((ts + 8, tc), F32)],
        compiler_params=_params("parallel", "parallel"),
        name="gdn_conv",
    )(proj, proj, conv_w)


def _gdn_gates_kernel(ba_ref, alog_ref, dtb_ref, out_ref):
    ts = ba_ref.shape[0]
    nh = ba_ref.shape[1] // 2
    x = ba_ref[...]
    beta = jax.nn.sigmoid(x[:, :nh])
    z = x[:, nh:] + dtb_ref[...]
    softplus = jnp.maximum(z, 0.0) + jnp.log1p(jnp.exp(-jnp.abs(z)))
    g = -jnp.exp(alog_ref[...]) * softplus
    ri = lax.broadcasted_iota(jnp.int32, (ts, ts), 0)
    ci = lax.broadcasted_iota(jnp.int32, (ts, ts), 1)
    same = (ri // GDN_BLOCK) == (ci // GDN_BLOCK)
    gc = _sel_dot(jnp.where(same & (ci <= ri), 1.0, 0.0), g)
    gl = _sel_dot(jnp.where(same, 1.0, 0.0), g)
    out_ref[...] = jnp.concatenate([beta, gc, gl, jnp.zeros((ts, LANES - 3 * nh), F32)], axis=1)


def _gdn_gates(ba, a_log, dt_bias, *, ts=512):
    s, two_h = ba.shape
    nh = two_h // 2
    assert 3 * nh <= LANES
    return pl.pallas_call(
        _gdn_gates_kernel,
        grid=(s // ts,),
        in_specs=[pl.BlockSpec((ts, two_h), lambda i: (i, 0)),
                  pl.BlockSpec((1, nh), lambda i: (0, 0)),
                  pl.BlockSpec((1, nh), lambda i: (0, 0))],
        out_specs=pl.BlockSpec((ts, LANES), lambda i: (i, 0)),
        out_shape=jax.ShapeDtypeStruct((s, LANES), F32),
        compiler_params=_params("parallel"),
        name="gdn_gates",
    )(ba, a_log.reshape(1, nh), dt_bias.reshape(1, nh))


def _bdot(a, b):
    return jnp.einsum("hij,hjk->hik", a.astype(BF16), b.astype(BF16), preferred_element_type=F32)


def _bdot3(a, b):
    ah, al = _split(a)
    bh, bl = _split(b)
    return jnp.einsum("hij,hjk->hik", jnp.concatenate([ah, ah, al], axis=2),
                      jnp.concatenate([bh, bl, bh], axis=1), preferred_element_type=F32)


def _unit_lower_inverse(low, ri, ci):
    n = low.shape[-1]

    def blk(b):
        return ((ri // b) == (ci // b))[None]

    eye = jnp.where(ri == ci, 1.0, 0.0)[None]
    n1 = jnp.where(blk(16), -low, 0.0)
    n2 = _bdot(n1, n1)
    n4 = _bdot(n2, n2)
    n8 = _bdot(n4, n4)
    t = eye + n1
    t = t + _bdot(n2, t)
    t = t + _bdot(n4, t)
    t = t + _bdot(n8, t)
    b = 32
    while b <= n:
        off = jnp.where(blk(b) & jnp.logical_not(blk(b // 2)), low, 0.0)
        t = t - _bdot(t, _bdot(off, t))
        b *= 2
    return t


def _gdn_chunk_kernel(q_ref, k_ref, v_ref, gates_ref, gct_ref, o_ref, state, *, heads):
    hg = pl.program_id(0)
    rows = q_ref.shape[0]
    nh = gct_ref.shape[0]
    dh = HEAD_DIM
    rep = rows // dh

    @pl.when(pl.program_id(1) == 0)
    def _():
        state[...] = jnp.zeros_like(state)

    ri = lax.broadcasted_iota(jnp.int32, (rows, rows), 0)
    ci = lax.broadcasted_iota(jnp.int32, (rows, rows), 1)
    lower = (ci <= ri)[None]
    strict = (ci < ri)[None]
    dot = functools.partial(jnp.dot, preferred_element_type=F32)

    gates_hi, gates_lo = _split(gates_ref[...])
    width = heads * 3 * dh
    prow = lax.broadcasted_iota(jnp.int32, (LANES, width), 0)
    pcol = lax.broadcasted_iota(jnp.int32, (LANES, width), 1)
    pick = jnp.where(prow == ((pcol // dh) % 3) * nh + hg * heads + pcol // (3 * dh), 1.0, 0.0).astype(BF16)
    cols = dot(jnp.concatenate([gates_hi, gates_lo], axis=1),
               jnp.concatenate([pick, pick], axis=0))

    def per_head(j):
        return jnp.stack([cols[:, (3 * hl + j) * dh:(3 * hl + j + 1) * dh] for hl in range(heads)])

    beta, gcol, glast = per_head(0), per_head(1), per_head(2)
    gct_hi, gct_lo = _split(gct_ref[...])
    trow = lax.broadcasted_iota(jnp.int32, (heads * rows, nh), 0) // rows
    tcol = lax.broadcasted_iota(jnp.int32, (heads * rows, nh), 1)
    pick_t = jnp.where(tcol == hg * heads + trow, 1.0, 0.0).astype(BF16)
    g_j = dot(jnp.concatenate([pick_t, pick_t], axis=1),
              jnp.concatenate([gct_hi, gct_lo], axis=0)).reshape(heads, rows, rows)
    g_i = jnp.concatenate([gcol] * rep, axis=2)
    beta_i = jnp.concatenate([beta] * rep, axis=2)
    decay = jnp.where(lower, jnp.exp(jnp.where(lower, g_i - g_j, 0.0)), 0.0)

    def pairs(ref):
        return [ref[:, p * dh:(p + 1) * dh] for p in range(heads // 2)]

    def nt(a, b):
        return jnp.einsum("pid,pjd->pij", a.astype(BF16), b.astype(BF16), preferred_element_type=F32)

    def both(x):
        return jnp.stack([x[hl // 2] for hl in range(heads)])

    qp, kp = jnp.stack(pairs(q_ref)), jnp.stack(pairs(k_ref))
    q, k = both(qp), both(kp)
    kk, qk = both(nt(kp, kp)), both(nt(qp, kp))
    v = jnp.stack([v_ref[:, hl * dh:(hl + 1) * dh] for hl in range(heads)])

    low = jnp.where(strict, kk * beta_i * decay, 0.0)
    t0 = _unit_lower_inverse(low, ri, ci)
    rhs = jnp.concatenate([v * beta, k * (beta * jnp.exp(gcol))], axis=2)
    x = _bdot(t0, rhs)
    x = x + _bdot(t0, rhs - x - _bdot3(low, x))
    u, w = x[:, :, :dh], x[:, :, dh:]
    a_x = _bdot(jnp.where(lower, qk * decay, 0.0), x)
    q_eff = q * jnp.exp(gcol) - a_x[:, :, dh:]
    k_g = k * jnp.exp(glast - gcol)

    s = state[...]
    o = _bdot(q_eff, s) + a_x[:, :, :dh]
    v_new = u - _bdot(w, s)
    state[...] = s * jnp.exp(glast[:, 0:1, :]) + jnp.einsum(
        "hck,hcv->hkv", k_g.astype(BF16), v_new.astype(BF16), preferred_element_type=F32)
    for hl in range(heads):
        o_ref[:, hl * dh:(hl + 1) * dh] = o[hl]


def _gdn_chunk(q, k, v, gates, gct, *, heads=8):
    rows = GDN_BLOCK
    s, dv_total = v.shape
    nh = dv_total // HEAD_DIM
    assert nh == 2 * (q.shape[1] // HEAD_DIM) and heads % 2 == 0 and nh % heads == 0
    qk_spec = pl.BlockSpec((rows, heads // 2 * HEAD_DIM), lambda h, i: (i, h))
    v_spec = pl.BlockSpec((rows, heads * HEAD_DIM), lambda h, i: (i, h))
    return pl.pallas_call(
        functools.partial(_gdn_chunk_kernel, heads=heads),
        grid=(nh // heads, s // rows),
        in_specs=[qk_spec, qk_spec, v_spec,
                  pl.BlockSpec((rows, LANES), lambda h, i: (i, 0)),
                  pl.BlockSpec((nh, rows), lambda h, i: (0, i))],
        out_specs=v_spec,
        out_shape=jax.ShapeDtypeStruct((s, dv_total), F32),
        scratch_shapes=[pltpu.VMEM((heads, HEAD_DIM, HEAD_DIM), F32)],
        compiler_params=_params("parallel", "arbitrary"),
        name="gdn_chunk",
    )(q, k, v, gates, gct)


def _gdn_post_kernel(o_ref, z_ref, nw_ref, out_ref):
    nw = nw_ref[...]
    for hh in range(o_ref.shape[1] // HEAD_DIM):
        sl = slice(hh * HEAD_DIM, (hh + 1) * HEAD_DIM)
        o = o_ref[:, sl]
        ms = jnp.mean(o * o, axis=-1, keepdims=True)
        out_ref[:, sl] = ((o * lax.rsqrt(ms + NORM_EPS)) * nw * _silu(z_ref[:, sl])).astype(out_ref.dtype)


def _gdn_post(o, proj, z_col0, norm_w, *, ts=256):
    s, dv = o.shape
    zb = z_col0 // dv
    return pl.pallas_call(
        _gdn_post_kernel,
        grid=(s // ts,),
        in_specs=[pl.BlockSpec((ts, dv), lambda i: (i, 0)),
                  pl.BlockSpec((ts, dv), lambda i: (i, zb)),
                  pl.BlockSpec((1, HEAD_DIM), lambda i: (0, 0))],
        out_specs=pl.BlockSpec((ts, dv), lambda i: (i, 0)),
        out_shape=jax.ShapeDtypeStruct((s, dv), BF16),
        compiler_params=_params("parallel"),
        name="gdn_post",
    )(o, proj, norm_w.reshape(1, HEAD_DIM))


def _gated_deltanet(x, sc, sh, w_in, conv_w, layer, a_log, dt_bias, norm_w):
    nvh = a_log.shape[0]
    dv = nvh * HEAD_DIM
    conv_ch = conv_w.shape[2]
    dqk = (conv_ch - dv) // 2
    n_main = conv_ch + dv
    proj = _mod_matmul(x, sc, sh, w_in, layer, 0, n_main, name="gdn_in_proj")
    ba = _mod_matmul(x, sc, sh, w_in[layer, :, n_main:][None], 0, 0, 2 * nvh, name="gdn_in_gates")
    q = _gdn_conv(proj, conv_w, layer, 0, dqk, normalize=True, scale=HEAD_DIM ** -0.5)
    k = _gdn_conv(proj, conv_w, layer, dqk, dqk, normalize=True, scale=1.0)
    v = _gdn_conv(proj, conv_w, layer, 2 * dqk, dv, normalize=False, scale=1.0)
    gates = _gdn_gates(ba, a_log, dt_bias)
    o = _gdn_chunk(q, k, v, gates, gates[:, nvh:2 * nvh].T)
    return _gdn_post(o, proj, conv_ch, norm_w)


def _rope(x, c, sg):
    return x * c + pltpu.roll(x, HEAD_DIM // 2, 1) * sg


def _nsa_prep_kernel(p_ref, c_ref, s_ref, q_ref, kvc_ref, ks_ref, vs_ref, kw_ref, vw_ref, *, dq, dkv):
    c = c_ref[...]
    sg = s_ref[...]
    dh = HEAD_DIM

    def head(col0, hh, rope, mul):
        x = p_ref[:, col0 + hh * dh:col0 + (hh + 1) * dh]
        if rope:
            x = _rope(x, c, sg)
        return x if mul == 1.0 else x * mul

    for hh in range(dq // dh):
        q_ref[:, hh * dh:(hh + 1) * dh] = head(0, hh, True, dh ** -0.5 * LOG2E).astype(q_ref.dtype)
    for hh in range(dkv // dh):
        sl = slice(hh * dh, (hh + 1) * dh)
        kvc_ref[0, hh] = head(dq, hh, True, 1.0)
        kvc_ref[1, hh] = head(dq + dkv, hh, False, 1.0)
        ks_ref[:, sl] = head(dq + 2 * dkv, hh, True, 1.0).astype(ks_ref.dtype)
        vs_ref[:, sl] = head(dq + 3 * dkv, hh, False, 1.0).astype(vs_ref.dtype)
        kw_ref[:, sl] = head(dq + 4 * dkv, hh, True, 1.0).astype(kw_ref.dtype)
        vw_ref[:, sl] = head(dq + 5 * dkv, hh, False, 1.0).astype(vw_ref.dtype)


def _nsa_prep(proj, cos2, sin2, dq, dkv, *, ts=256):
    s, n = proj.shape
    hk = dkv // HEAD_DIM
    kv_spec = pl.BlockSpec((ts, dkv), lambda i: (i, 0))
    tab = pl.BlockSpec((ts, HEAD_DIM), lambda i: (i, 0))
    kv16 = jax.ShapeDtypeStruct((s, dkv), BF16)
    return pl.pallas_call(
        functools.partial(_nsa_prep_kernel, dq=dq, dkv=dkv),
        grid=(s // ts,),
        in_specs=[pl.BlockSpec((ts, n), lambda i: (i, 0)), tab, tab],
        out_specs=[pl.BlockSpec((ts, dq), lambda i: (i, 0)),
                   pl.BlockSpec((2, hk, ts, HEAD_DIM), lambda i: (0, 0, i, 0)),
                   kv_spec, kv_spec, kv_spec, kv_spec],
        out_shape=[jax.ShapeDtypeStruct((s, dq), BF16),
                   jax.ShapeDtypeStruct((2, hk, s, HEAD_DIM), F32),
                   kv16, kv16, kv16, kv16],
        compiler_params=_params("parallel"),
        name="nsa_prep",
    )(proj, cos2, sin2)


def _nsa_compress_kernel(x_ref, pe_ref, w1_ref, w2_ref, o_ref):
    x = x_ref[0, 0]
    n = x.shape[0]
    a = _dot(x + pe_ref[0, 0:1, :], w1_ref[0, 0])
    b = _dot(x + pe_ref[0, 1:2, :], w1_ref[0, 1])
    y = a + pltpu.roll(b, n - 1, 0)
    o_ref[0, 0] = _dot(_silu(y), w2_ref[0])


def _nsa_compress(x, pe, w1, w2):
    _, hk, ng, gw = x.shape
    half = CMP_LEN // 2
    pe2 = pe.reshape(2, 2, half * HEAD_DIM)
    w1r = w1.reshape(2, 2, half * HEAD_DIM, HEAD_DIM)
    return pl.pallas_call(
        _nsa_compress_kernel,
        grid=(2, hk),
        in_specs=[pl.BlockSpec((1, 1, ng, gw), lambda t, h: (t, h, 0, 0)),
                  pl.BlockSpec((1, 2, gw), lambda t, h: (t, 0, 0)),
                  pl.BlockSpec((1, 2, gw, HEAD_DIM), lambda t, h: (t, 0, 0, 0)),
                  pl.BlockSpec((1, HEAD_DIM, HEAD_DIM), lambda t, h: (t, 0, 0))],
        out_specs=pl.BlockSpec((1, 1, ng, HEAD_DIM), lambda t, h: (t, h, 0, 0)),
        out_shape=jax.ShapeDtypeStruct((2, hk, ng, HEAD_DIM), F32),
        compiler_params=_params("parallel", "parallel"),
        name="nsa_compress",
    )(x, pe2, w1r, w2)


def _nsa_cmp_select_kernel(q_ref, kc_ref, vc_ref, ov_ref, o_ref, sel_ref):
    tq = q_ref.shape[0]
    nc = kc_ref.shape[2]
    nbp = ov_ref.shape[1]
    t0 = pl.program_id(0) * tq
    tpos = t0 + lax.broadcasted_iota(jnp.int32, (tq, nc), 0)
    nidx = lax.broadcasted_iota(jnp.int32, (tq, nc), 1)
    valid = (nidx * CMP_STRIDE + (CMP_LEN - 1)) <= tpos
    kc = kc_ref[0, 0].astype(BF16)
    vc = vc_ref[0, 0].astype(BF16)
    psum = jnp.zeros((tq, nc), F32)
    q4 = jnp.concatenate([q_ref[:, g * HEAD_DIM:(g + 1) * HEAD_DIM] for g in range(NSA_GROUP)], axis=0)
    s_all = _dot_nt(q4, kc)
    probs = []
    for g in range(NSA_GROUP):
        s = jnp.where(valid, s_all[g * tq:(g + 1) * tq, :], NEG_BIG)
        m = jnp.max(s, axis=-1, keepdims=True)
        p = jnp.where(valid, jnp.exp2(s - m), 0.0)
        den = jnp.sum(p, axis=-1, keepdims=True)
        p = p / jnp.where(den > 0.0, den, 1.0)
        probs.append(p.astype(BF16))
        psum = psum + p
    o_all = jnp.dot(jnp.concatenate(probs, axis=0), vc, preferred_element_type=F32)
    for g in range(NSA_GROUP):
        o_ref[:, g * HEAD_DIM:(g + 1) * HEAD_DIM] = o_all[g * tq:(g + 1) * tq, :]
    imp = _dot_sel(psum, ov_ref[...])
    j = lax.broadcasted_iota(jnp.int32, (tq, nbp), 1)
    cur = (t0 + lax.broadcasted_iota(jnp.int32, (tq, nbp), 0)) // SLC_LEN
    forced = (j == 0) | (j == cur) | (j == cur - 1)
    work = jnp.where(j <= cur, jnp.where(forced, jnp.inf, imp), -jnp.inf)
    sel = jnp.zeros((tq, nbp), F32)
    for _ in range(SLC_TOP):
        m = jnp.max(work, axis=-1, keepdims=True)
        first = jnp.min(jnp.where(work == m, j, nbp), axis=-1, keepdims=True)
        hit = j == first
        sel = jnp.where(hit, 1.0, sel)
        work = jnp.where(hit, -jnp.inf, work)
    sel_ref[0] = jnp.where(j <= cur, sel, 0.0).astype(sel_ref.dtype)


def _nsa_cmp_select(q, cmp, ov, *, tq=512):
    s, dq = q.shape
    _, hk, nc, _ = cmp.shape
    nbp = ov.shape[1]
    gw = NSA_GROUP * HEAD_DIM
    return pl.pallas_call(
        _nsa_cmp_select_kernel,
        grid=(s // tq, hk),
        in_specs=[pl.BlockSpec((tq, gw), lambda i, h: (i, h)),
                  pl.BlockSpec((1, 1, nc, HEAD_DIM), lambda i, h: (0, h, 0, 0)),
                  pl.BlockSpec((1, 1, nc, HEAD_DIM), lambda i, h: (1, h, 0, 0)),
                  pl.BlockSpec((nc, nbp), lambda i, h: (0, 0))],
        out_specs=[pl.BlockSpec((tq, gw), lambda i, h: (i, h)),
                   pl.BlockSpec((1, tq, nbp), lambda i, h: (h, i, 0))],
        out_shape=[jax.ShapeDtypeStruct((s, dq), F32),
                   jax.ShapeDtypeStruct((hk, s, nbp), BF16)],
        compiler_params=_params("parallel", "parallel"),
        name="nsa_cmp_select",
    )(q, cmp, cmp, ov)


ATTN_ROWS = 64


def _masked_attn_kernel(*refs, tq, tk, sub, selected):
    if selected:
        (qi_ref, kj_ref, first_ref, last_ref, q_ref, k_ref, v_ref, sel_ref, o_ref,
         q4, bias, s_scr, p_scr, a_scr, m_scr, acc) = refs
    else:
        (qi_ref, kj_ref, first_ref, last_ref, q_ref, k_ref, v_ref, o_ref,
         q4, bias, s_scr, p_scr, a_scr, m_scr, acc) = refs
    t = pl.program_id(1)
    qi = qi_ref[t]
    kj = kj_ref[t]
    dh = HEAD_DIM

    @pl.when(first_ref[t] == 1)
    def _():
        for g in range(NSA_GROUP):
            q4[g * tq:(g + 1) * tq, :] = q_ref[:, g * dh:(g + 1) * dh]
        m_scr[...] = jnp.full_like(m_scr, NEG_BIG)
        acc[...] = jnp.zeros_like(acc)

    tpos = qi * tq + lax.broadcasted_iota(jnp.int32, (tq, tk), 0)
    kpos = kj * tk + lax.broadcasted_iota(jnp.int32, (tq, tk), 1)
    if selected:
        nbp = sel_ref.shape[2]
        expand = jnp.where(lax.broadcasted_iota(jnp.int32, (nbp, tk), 0)
                           == (kj * tk + lax.broadcasted_iota(jnp.int32, (nbp, tk), 1)) // SLC_LEN, 1.0, 0.0)
        picked = jnp.dot(sel_ref[0], expand.astype(BF16), preferred_element_type=F32)
        mask = (picked > 0.5) & (kpos <= tpos)
    else:
        mask = (kpos <= tpos) & (kpos > tpos - WINDOW)
    bias[...] = jnp.where(mask, 0.0, NEG_BIG)

    nsub = tk // sub
    for u in range(nsub):
        cols = slice(u * sub, (u + 1) * sub)
        s_scr[:, cols] = lax.dot_general(q4[...], k_ref[cols, :], (((1,), (1,)), ((), ())),
                                         preferred_element_type=F32)
    rb = ATTN_ROWS
    for u in range(nsub):
        cols = slice(u * sub, (u + 1) * sub)
        for r0 in range(0, NSA_GROUP * tq, rb):
            rows = slice(r0, r0 + rb)
            brow = r0 % tq
            s = s_scr[rows, cols] + bias[brow:brow + rb, cols]
            m_prev = m_scr[rows, :]
            m_new = jnp.maximum(m_prev, jnp.max(s, axis=-1, keepdims=True))
            a_scr[u, rows, :] = jnp.exp2(m_prev - m_new)
            p_scr[rows, cols] = jnp.exp2(s - jnp.concatenate([m_new] * (sub // LANES), axis=1)).astype(BF16)
            m_scr[rows, :] = m_new
        v_one = jnp.concatenate([v_ref[cols, :], jnp.ones((sub, dh), BF16)], axis=1)
        alpha = jnp.concatenate([a_scr[u]] * 2, axis=1)
        acc[...] = alpha * acc[...] + jnp.dot(p_scr[:, cols], v_one, preferred_element_type=F32)

    @pl.when(last_ref[t] == 1)
    def _():
        for g in range(NSA_GROUP):
            rows = slice(g * tq, (g + 1) * tq)
            o_ref[:, g * dh:(g + 1) * dh] = acc[rows, :dh] / acc[rows, dh:]


def _masked_attn(q, k, v, sel, *, tq, tk, sub):
    s, dq = q.shape
    hk = k.shape[1] // HEAD_DIM
    gw = NSA_GROUP * HEAD_DIM
    selected = sel is not None
    pairs = []
    for i in range(s // tq):
        lo = 0 if selected else max(i * tq - (WINDOW - 1), 0) // tk
        hi = (i * tq + tq - 1) // tk
        pairs += [(i, j, int(j == lo), int(j == hi)) for j in range(lo, hi + 1)]
    sched = [jnp.asarray([p[c] for p in pairs], jnp.int32) for c in range(4)]

    in_specs = [pl.BlockSpec((tq, gw), lambda h, t, qi, kj, fi, la: (qi[t], h)),
                pl.BlockSpec((tk, HEAD_DIM), lambda h, t, qi, kj, fi, la: (kj[t], h)),
                pl.BlockSpec((tk, HEAD_DIM), lambda h, t, qi, kj, fi, la: (kj[t], h))]
    args = [q, k, v]
    if selected:
        in_specs.append(pl.BlockSpec((1, tq, sel.shape[2]), lambda h, t, qi, kj, fi, la: (h, qi[t], 0)))
        args.append(sel)
    rows = NSA_GROUP * tq
    return pl.pallas_call(
        functools.partial(_masked_attn_kernel, tq=tq, tk=tk, sub=sub, selected=selected),
        grid_spec=pltpu.PrefetchScalarGridSpec(
            num_scalar_prefetch=4,
            grid=(hk, len(pairs)),
            in_specs=in_specs,
            out_specs=pl.BlockSpec((tq, gw), lambda h, t, qi, kj, fi, la: (qi[t], h)),
            scratch_shapes=[pltpu.VMEM((rows, HEAD_DIM), BF16),
                            pltpu.VMEM((tq, tk), F32),
                            pltpu.VMEM((rows, tk), F32),
                            pltpu.VMEM((rows, tk), BF16),
                            pltpu.VMEM((tk // sub, rows, LANES), F32),
                            pltpu.VMEM((rows, LANES), F32),
                            pltpu.VMEM((rows, 2 * HEAD_DIM), F32)]),
        out_shape=jax.ShapeDtypeStruct((s, dq), F32),
        compiler_params=_params("parallel", "arbitrary"),
        name="nsa_selected_attn" if selected else "nsa_window_attn",
    )(*sched, *args)


def _nsa_combine_kernel(oc_ref, os_ref, ow_ref, g_ref, out_ref):
    gates = jax.nn.sigmoid(g_ref[...])
    ng, dq = g_ref.shape[1], oc_ref.shape[1]
    row = lax.broadcasted_iota(jnp.int32, (ng, dq), 0)
    head3 = (lax.broadcasted_iota(jnp.int32, (ng, dq), 1) // HEAD_DIM) * N_BRANCH
    acc = None
    for b, ref in enumerate((oc_ref, os_ref, ow_ref)):
        gb = _dot_sel(gates, jnp.where(row == head3 + b, 1.0, 0.0))
        term = gb * ref[...]
        acc = term if acc is None else acc + term
    out_ref[...] = acc.astype(out_ref.dtype)


def _nsa_combine(oc, osel, ow, gates, *, ts=256):
    s, dq = oc.shape
    spec = pl.BlockSpec((ts, dq), lambda i: (i, 0))
    return pl.pallas_call(
        _nsa_combine_kernel,
        grid=(s // ts,),
        in_specs=[spec, spec, spec, pl.BlockSpec((ts, gates.shape[1]), lambda i: (i, 0))],
        out_specs=spec,
        out_shape=jax.ShapeDtypeStruct((s, dq), BF16),
        compiler_params=_params("parallel"),
        name="nsa_combine",
    )(oc, osel, ow, gates)


def _nsa_attention(x, sc, sh, cos2, sin2, w_in, layer, cmp_pe, cmp_w1, cmp_w2):
    s, d = x.shape
    dq = d
    hk = d // HEAD_DIM // NSA_GROUP
    dkv = hk * HEAD_DIM
    n_main = dq + 6 * dkv
    ngate = w_in.shape[2] - n_main
    proj = _mod_matmul(x, sc, sh, w_in, layer, 0, n_main, name="nsa_in_proj")
    w_gate = jnp.pad(w_in[layer, :, n_main:], ((0, 0), (0, LANES - ngate)))[None]
    gates = _mod_matmul(x, sc, sh, w_gate, 0, 0, LANES, name="nsa_in_gates")
    q, kvc, ks, vs, kw, vw = _nsa_prep(proj, cos2, sin2, dq, dkv)
    groups = kvc.reshape(2, hk, s // CMP_STRIDE, CMP_STRIDE * HEAD_DIM)
    cmp = _nsa_compress(groups, cmp_pe, cmp_w1, cmp_w2)

    nc = s // CMP_STRIDE
    n_slc = s // SLC_LEN
    nbp = -(-n_slc // LANES) * LANES
    c_start = CMP_STRIDE * jnp.arange(nc)
    s_start = SLC_LEN * jnp.arange(nbp)
    overlap = jnp.clip(jnp.minimum(c_start[:, None] + CMP_LEN, s_start[None, :] + SLC_LEN)
                       - jnp.maximum(c_start[:, None], s_start[None, :]), 0, None).astype(F32) / CMP_LEN
    o_c, sel = _nsa_cmp_select(q, cmp, overlap.astype(BF16))
    o_s = _masked_attn(q, ks, vs, sel, tq=256, tk=1024, sub=512)
    o_w = _masked_attn(q, kw, vw, None, tq=256, tk=256, sub=256)
    return _nsa_combine(o_c, o_s, o_w, gates)


def kernel(x, c, positions, mod_w, mod_b, ln_g, ln_b, ffn_w_gu, ffn_w_down, gdn_w_in, gdn_conv_w,
           gdn_a_log, gdn_dt_bias, gdn_norm_w, gdn_w_out, nsa_w_in, nsa_cmp_pe, nsa_cmp_w1,
           nsa_cmp_w2, nsa_w_out):
    bsz, s, d = x.shape
    assert bsz == 1
    depth = mod_w.shape[0]
    alpha = (2 * depth) ** 0.25

    inv = ROPE_THETA ** (-jnp.arange(0, HEAD_DIM, 2, dtype=F32) / HEAD_DIM)
    ang = positions[0].astype(F32)[:, None] * inv
    cos, sin = jnp.cos(ang), jnp.sin(ang)
    cos2 = jnp.concatenate([cos, cos], axis=-1)
    sin2 = jnp.concatenate([-sin, sin], axis=-1)

    mod = _modulation(c, mod_w, mod_b)
    gdn_w_out, nsa_w_out, ffn_w_down = [w.astype(BF16) for w in (gdn_w_out, nsa_w_out, ffn_w_down)]
    xs = x[0]
    for i in range(depth):
        sh1, sc1, ga1, sh2, sc2, ga2 = [mod[i, :, r * d:(r + 1) * d] for r in range(6)]
        j = i // N_MIXERS
        if i % N_MIXERS == 0:
            y = _gated_deltanet(xs, sc1, sh1, gdn_w_in, gdn_conv_w, j, gdn_a_log[j],
                                gdn_dt_bias[j], gdn_norm_w[j])
            w_out = gdn_w_out
        else:
            y = _nsa_attention(xs, sc1, sh1, cos2, sin2, nsa_w_in, j, nsa_cmp_pe[j],
                               nsa_cmp_w1[j], nsa_cmp_w2[j])
            w_out = nsa_w_out
        xs = _matmul_ln(y, w_out, j, xs, ga1, ln_g[i, 0:1], ln_b[i, 0:1], alpha=alpha)
        a = _swiglu_up(xs, sc2, sh2, ffn_w_gu, i)
        xs = _matmul_ln(a, ffn_w_down, i, xs, ga2, ln_g[i, 1:2], ln_b[i, 1:2], alpha=alpha)
    return xs[None]
```

```python
import functools

import jax
import jax.numpy as jnp
from jax import lax
from jax.experimental import pallas as pl
from jax.experimental.pallas import tpu as pltpu

F32 = jnp.float32
BF16 = jnp.bfloat16
HIGHEST = lax.Precision.HIGHEST

HEAD_DIM = 128
N_MIXERS = 2
GDN_CONV = 4
GDN_BLOCK = 128
NSA_GROUP = 4
N_BRANCH = 3
CMP_LEN = 32
CMP_STRIDE = 16
SLC_LEN = 64
SLC_TOP = 16
WINDOW = 512
ROPE_THETA = 10000.0
LN_EPS = 1e-5
NORM_EPS = 1e-6

LOG2E = 1.4426950408889634
MATMUL_ROWS = 1024
LANES = 128
VMEM_LIMIT_BYTES = 56 * 1024 * 1024
NEG_BIG = -1e30


def _params(*sem):
    return pltpu.CompilerParams(dimension_semantics=sem, vmem_limit_bytes=VMEM_LIMIT_BYTES)


def _silu(v):
    return v * jax.nn.sigmoid(v)


def _dot(a, b):
    return jnp.dot(a.astype(BF16), b.astype(BF16), preferred_element_type=F32)


def _dot_nt(a, b):
    return lax.dot_general(a.astype(BF16), b.astype(BF16), (((1,), (1,)), ((), ())),
                           preferred_element_type=F32)


def _split(a):
    hi = a.astype(BF16)
    lo = (a - hi.astype(F32)).astype(BF16)
    return hi, lo


def _dot3(a, b):
    ah, al = _split(a)
    bh, bl = _split(b)
    d = functools.partial(jnp.dot, preferred_element_type=F32)
    return d(ah, bh) + (d(ah, bl) + d(al, bh))


def _dot_sel(a, onehot):
    ah, al = _split(a)
    e = onehot.astype(BF16)
    d = functools.partial(jnp.dot, preferred_element_type=F32)
    return d(ah, e) + d(al, e)


def _sel_dot(onehot, b):
    bh, bl = _split(b)
    e = onehot.astype(BF16)
    d = functools.partial(jnp.dot, preferred_element_type=F32)
    return d(e, bh) + d(e, bl)


def _mod_kernel(c_ref, w_ref, b_ref, o_ref):
    cond = _silu(c_ref[...])
    o_ref[0] = _dot3(cond, w_ref[0]) + b_ref[0]


def _modulation(c, mod_w, mod_b):
    depth, d, n = mod_w.shape
    tn = 1536
    c8 = jnp.broadcast_to(c[:1], (8, d))
    out = pl.pallas_call(
        _mod_kernel,
        grid=(depth, n // tn),
        in_specs=[pl.BlockSpec((8, d), lambda l, j: (0, 0)),
                  pl.BlockSpec((1, d, tn), lambda l, j: (l, 0, j)),
                  pl.BlockSpec((1, 1, tn), lambda l, j: (l, 0, j))],
        out_specs=pl.BlockSpec((1, 8, tn), lambda l, j: (l, 0, j)),
        out_shape=jax.ShapeDtypeStruct((depth, 8, n), F32),
        compiler_params=_params("parallel", "parallel"),
        name="modulation",
    )(c8, mod_w, mod_b.reshape(depth, 1, n))
    return out[:, 0:1, :]


def _mod_matmul_kernel(x_ref, sc_ref, sh_ref, w_ref, o_ref, h_scr):
    @pl.when(pl.program_id(1) == 0)
    def _():
        h_scr[...] = (x_ref[...] * (1.0 + sc_ref[...]) + sh_ref[...]).astype(BF16)

    o_ref[...] = jnp.dot(h_scr[...], w_ref[...].astype(BF16),
                         preferred_element_type=F32).astype(o_ref.dtype)


def _mod_matmul(x, sc, sh, w, layer, col0, n, *, tm=MATMUL_ROWS, tn=512, name="mod_matmul"):
    m, k = x.shape
    tm = min(tm, m)
    tn = min(tn, n)
    cb0 = col0 // tn
    assert col0 % tn == 0 and n % tn == 0 and m % tm == 0
    return pl.pallas_call(
        _mod_matmul_kernel,
        grid=(m // tm, n // tn),
        in_specs=[pl.BlockSpec((tm, k), lambda i, j: (i, 0)),
                  pl.BlockSpec((1, k), lambda i, j: (0, 0)),
                  pl.BlockSpec((1, k), lambda i, j: (0, 0)),
                  pl.BlockSpec((None, k, tn), lambda i, j: (layer, 0, j + cb0))],
        out_specs=pl.BlockSpec((tm, tn), lambda i, j: (i, j)),
        out_shape=jax.ShapeDtypeStruct((m, n), F32),
        scratch_shapes=[pltpu.VMEM((tm, k), BF16)],
        compiler_params=_params("parallel", "arbitrary"),
        name=name,
    )(x, sc, sh, w)


def _swiglu_up_kernel(x_ref, sc_ref, sh_ref, wg_ref, wu_ref, o_ref, h_scr):
    @pl.when(pl.program_id(1) == 0)
    def _():
        h_scr[...] = (x_ref[...] * (1.0 + sc_ref[...]) + sh_ref[...]).astype(BF16)

    h = h_scr[...]
    g = jnp.dot(h, wg_ref[...].astype(BF16), preferred_element_type=F32)
    u = jnp.dot(h, wu_ref[...].astype(BF16), preferred_element_type=F32)
    o_ref[...] = (_silu(g) * u).astype(o_ref.dtype)


def _swiglu_up(x, sc, sh, w_gu, layer, *, tm=MATMUL_ROWS, tn=512):
    m, k = x.shape
    tm = min(tm, m)
    dff = w_gu.shape[2] // 2
    nb = dff // tn
    assert dff % tn == 0 and m % tm == 0
    return pl.pallas_call(
        _swiglu_up_kernel,
        grid=(m // tm, nb),
        in_specs=[pl.BlockSpec((tm, k), lambda i, j: (i, 0)),
                  pl.BlockSpec((1, k), lambda i, j: (0, 0)),
                  pl.BlockSpec((1, k), lambda i, j: (0, 0)),
                  pl.BlockSpec((None, k, tn), lambda i, j: (layer, 0, j)),
                  pl.BlockSpec((None, k, tn), lambda i, j: (layer, 0, j + nb))],
        out_specs=pl.BlockSpec((tm, tn), lambda i, j: (i, j)),
        out_shape=jax.ShapeDtypeStruct((m, dff), BF16),
        scratch_shapes=[pltpu.VMEM((tm, k), BF16)],
        compiler_params=_params("parallel", "arbitrary"),
        name="swiglu_up",
    )(x, sc, sh, w_gu, w_gu)


LN_ROWS = 256
LN_MAX_TK = 1536


def _matmul_ln_kernel(a_ref, w_ref, x_ref, ga_ref, g_ref, b_ref, o_ref, *, alpha):
    kk = pl.program_id(1)
    part = jnp.dot(a_ref[...].astype(BF16), w_ref[...].astype(BF16), preferred_element_type=F32)

    @pl.when(kk == 0)
    def _():
        o_ref[...] = part

    @pl.when(kk > 0)
    def _():
        o_ref[...] += part

    @pl.when(kk == pl.num_programs(1) - 1)
    def _():
        for r0 in range(0, o_ref.shape[0], LN_ROWS):
            rows = slice(r0, r0 + LN_ROWS)
            r = alpha * x_ref[rows, :] + (1.0 + ga_ref[...]) * o_ref[rows, :]
            mu = jnp.mean(r, axis=-1, keepdims=True)
            d = r - mu
            var = jnp.mean(d * d, axis=-1, keepdims=True)
            o_ref[rows, :] = d * lax.rsqrt(var + LN_EPS) * g_ref[...] + b_ref[...]


def _matmul_ln(a, w, layer, x, ga, g, b, *, alpha, tm=512):
    m, k = a.shape
    n = w.shape[2]
    tm = min(tm, m)
    tk = next(t for t in range(min(k, LN_MAX_TK), 0, -LANES) if k % t == 0)
    assert m % tm == 0 and k % tk == 0 and tm % LN_ROWS == 0
    return pl.pallas_call(
        functools.partial(_matmul_ln_kernel, alpha=alpha),
        grid=(m // tm, k // tk),
        in_specs=[pl.BlockSpec((tm, tk), lambda i, kk: (i, kk)),
                  pl.BlockSpec((None, tk, n), lambda i, kk: (layer, kk, 0)),
                  pl.BlockSpec((tm, n), lambda i, kk: (i, 0)),
                  pl.BlockSpec((1, n), lambda i, kk: (0, 0)),
                  pl.BlockSpec((1, n), lambda i, kk: (0, 0)),
                  pl.BlockSpec((1, n), lambda i, kk: (0, 0))],
        out_specs=pl.BlockSpec((tm, n), lambda i, kk: (i, 0)),
        out_shape=jax.ShapeDtypeStruct((m, n), F32),
        compiler_params=_params("parallel", "arbitrary"),
        name="matmul_ln",
    )(a, w, x, ga, g, b)


def _gdn_conv_kernel(xp_ref, xc_ref, w_ref, o_ref, buf, *, normalize, scale):
    ts, tc = xc_ref.shape
    buf[0:8, :] = jnp.where(pl.program_id(0) > 0, xp_ref[...], 0.0)
    buf[8:, :] = xc_ref[...]
    w = w_ref[...]
    y = w[0:1, :] * buf[5:5 + ts, :]
    for i in range(1, GDN_CONV):
        y = y + w[i:i + 1, :] * buf[5 + i:5 + i + ts, :]
    y = _silu(y)
    if normalize:
        for hh in range(tc // HEAD_DIM):
            seg = y[:, hh * HEAD_DIM:(hh + 1) * HEAD_DIM]
            ss = jnp.sum(seg * seg, axis=-1, keepdims=True)
            o_ref[:, hh * HEAD_DIM:(hh + 1) * HEAD_DIM] = seg * (lax.rsqrt(ss + NORM_EPS) * scale)
    else:
        o_ref[...] = y


def _gdn_conv(proj, conv_w, layer, col0, ncols, *, normalize, scale, ts=512, tc=512):
    s = proj.shape[0]
    cb0 = col0 // tc
    hb = ts // 8
    return pl.pallas_call(
        functools.partial(_gdn_conv_kernel, normalize=normalize, scale=scale),
        grid=(s // ts, ncols // tc),
        in_specs=[pl.BlockSpec((8, tc), lambda i, j: (jnp.maximum(i * hb - 1, 0), j + cb0)),
                  pl.BlockSpec((ts, tc), lambda i, j: (i, j + cb0)),
                  pl.BlockSpec((None, GDN_CONV, tc), lambda i, j: (layer, 0, j + cb0))],
        out_specs=pl.BlockSpec((ts, tc), lambda i, j: (i, j)),
        out_shape=jax.ShapeDtypeStruct((s, ncols), F32),
        scratch_shapes=[pltpu.VMEM((ts + 8, tc), F32)],
        compiler_params=_params("parallel", "parallel"),
        name="gdn_conv",
    )(proj, proj, conv_w)


def _gdn_gates_kernel(ba_ref, alog_ref, dtb_ref, out_ref):
    ts = ba_ref.shape[0]
    nh = ba_ref.shape[1] // 2
    x = ba_ref[...]
    beta = jax.nn.sigmoid(x[:, :nh])
    z = x[:, nh:] + dtb_ref[...]
    softplus = jnp.maximum(z, 0.0) + jnp.log1p(jnp.exp(-jnp.abs(z)))
    g = -jnp.exp(alog_ref[...]) * softplus
    ri = lax.broadcasted_iota(jnp.int32, (ts, ts), 0)
    ci = lax.broadcasted_iota(jnp.int32, (ts, ts), 1)
    same = (ri // GDN_BLOCK) == (ci // GDN_BLOCK)
    gc = _sel_dot(jnp.where(same & (ci <= ri), 1.0, 0.0), g)
    gl = _sel_dot(jnp.where(same, 1.0, 0.0), g)
    out_ref[...] = jnp.concatenate([beta, gc, gl, jnp.zeros((ts, LANES - 3 * nh), F32)], axis=1)


def _gdn_gates(ba, a_log, dt_bias, *, ts=512):
    s, two_h = ba.shape
    nh = two_h // 2
    assert 3 * nh <= LANES
    return pl.pallas_call(
        _gdn_gates_kernel,
        grid=(s // ts,),
        in_specs=[pl.BlockSpec((ts, two_h), lambda i: (i, 0)),
                  pl.BlockSpec((1, nh), lambda i: (0, 0)),
                  pl.BlockSpec((1, nh), lambda i: (0, 0))],
        out_specs=pl.BlockSpec((ts, LANES), lambda i: (i, 0)),
        out_shape=jax.ShapeDtypeStruct((s, LANES), F32),
        compiler_params=_params("parallel"),
        name="gdn_gates",
    )(ba, a_log.reshape(1, nh), dt_bias.reshape(1, nh))


def _bdot(a, b):
    return jnp.einsum("hij,hjk->hik", a.astype(BF16), b.astype(BF16), preferred_element_type=F32)


def _bdot3(a, b):
    ah, al = _split(a)
    bh, bl = _split(b)
    return jnp.einsum("hij,hjk->hik", jnp.concatenate([ah, ah, al], axis=2),
                      jnp.concatenate([bh, bl, bh], axis=1), preferred_element_type=F32)


def _unit_lower_inverse(low, ri, ci):
    n = low.shape[-1]

    def blk(b):
        return ((ri // b) == (ci // b))[None]

    eye = jnp.where(ri == ci, 1.0, 0.0)[None]
    n1 = jnp.where(blk(16), -low, 0.0)
    n2 = _bdot(n1, n1)
    n4 = _bdot(n2, n2)
    n8 = _bdot(n4, n4)
    t = eye + n1
    t = t + _bdot(n2, t)
    t = t + _bdot(n4, t)
    t = t + _bdot(n8, t)
    b = 32
    while b <= n:
        off = jnp.where(blk(b) & jnp.logical_not(blk(b // 2)), low, 0.0)
        t = t - _bdot(t, _bdot(off, t))
        b *= 2
    return t


def _gdn_chunk_kernel(q_ref, k_ref, v_ref, gates_ref, gct_ref, o_ref, state, *, heads):
    hg = pl.program_id(0)
    rows = q_ref.shape[0]
    nh = gct_ref.shape[0]
    dh = HEAD_DIM
    rep = rows // dh

    @pl.when(pl.program_id(1) == 0)
    def _():
        state[...] = jnp.zeros_like(state)

    ri = lax.broadcasted_iota(jnp.int32, (rows, rows), 0)
    ci = lax.broadcasted_iota(jnp.int32, (rows, rows), 1)
    lower = (ci <= ri)[None]
    strict = (ci < ri)[None]
    dot = functools.partial(jnp.dot, preferred_element_type=F32)

    gates_hi, gates_lo = _split(gates_ref[...])
    width = heads * 3 * dh
    prow = lax.broadcasted_iota(jnp.int32, (LANES, width), 0)
    pcol = lax.broadcasted_iota(jnp.int32, (LANES, width), 1)
    pick = jnp.where(prow == ((pcol // dh) % 3) * nh + hg * heads + pcol // (3 * dh), 1.0, 0.0).astype(BF16)
    cols = dot(jnp.concatenate([gates_hi, gates_lo], axis=1),
               jnp.concatenate([pick, pick], axis=0))

    def per_head(j):
        return jnp.stack([cols[:, (3 * hl + j) * dh:(3 * hl + j + 1) * dh] for hl in range(heads)])

    beta, gcol, glast = per_head(0), per_head(1), per_head(2)
    gct_hi, gct_lo = _split(gct_ref[...])
    trow = lax.broadcasted_iota(jnp.int32, (heads * rows, nh), 0) // rows
    tcol = lax.broadcasted_iota(jnp.int32, (heads * rows, nh), 1)
    pick_t = jnp.where(tcol == hg * heads + trow, 1.0, 0.0).astype(BF16)
    g_j = dot(jnp.concatenate([pick_t, pick_t], axis=1),
              jnp.concatenate([gct_hi, gct_lo], axis=0)).reshape(heads, rows, rows)
    g_i = jnp.concatenate([gcol] * rep, axis=2)
    beta_i = jnp.concatenate([beta] * rep, axis=2)
    decay = jnp.where(lower, jnp.exp(jnp.where(lower, g_i - g_j, 0.0)), 0.0)

    def pairs(ref):
        return [ref[:, p * dh:(p + 1) * dh] for p in range(heads // 2)]

    def nt(a, b):
        return jnp.einsum("pid,pjd->pij", a.astype(BF16), b.astype(BF16), preferred_element_type=F32)

    def both(x):
        return jnp.stack([x[hl // 2] for hl in range(heads)])

    qp, kp = jnp.stack(pairs(q_ref)), jnp.stack(pairs(k_ref))
    q, k = both(qp), both(kp)
    kk, qk = both(nt(kp, kp)), both(nt(qp, kp))
    v = jnp.stack([v_ref[:, hl * dh:(hl + 1) * dh] for hl in range(heads)])

    low = jnp.where(strict, kk * beta_i * decay, 0.0)
    t0 = _unit_lower_inverse(low, ri, ci)
    rhs = jnp.concatenate([v * beta, k * (beta * jnp.exp(gcol))], axis=2)
    x = _bdot(t0, rhs)
    x = x + _bdot(t0, rhs - x - _bdot3(low, x))
    u, w = x[:, :, :dh], x[:, :, dh:]
    a_x = _bdot(jnp.where(lower, qk * decay, 0.0), x)
    q_eff = q * jnp.exp(gcol) - a_x[:, :, dh:]
    k_g = k * jnp.exp(glast - gcol)

    s = state[...]
    o = _bdot(q_eff, s) + a_x[:, :, :dh]
    v_new = u - _bdot(w, s)
    state[...] = s * jnp.exp(glast[:, 0:1, :]) + jnp.einsum(
        "hck,hcv->hkv", k_g.astype(BF16), v_new.astype(BF16), preferred_element_type=F32)
    for hl in range(heads):
        o_ref[:, hl * dh:(hl + 1) * dh] = o[hl]


def _gdn_chunk(q, k, v, gates, gct, *, heads=16):
    rows = GDN_BLOCK
    s, dv_total = v.shape
    nh = dv_total // HEAD_DIM
    assert nh == 2 * (q.shape[1] // HEAD_DIM) and heads % 2 == 0 and nh % heads == 0
    qk_spec = pl.BlockSpec((rows, heads // 2 * HEAD_DIM), lambda h, i: (i, h))
    v_spec = pl.BlockSpec((rows, heads * HEAD_DIM), lambda h, i: (i, h))
    return pl.pallas_call(
        functools.partial(_gdn_chunk_kernel, heads=heads),
        grid=(nh // heads, s // rows),
        in_specs=[qk_spec, qk_spec, v_spec,
                  pl.BlockSpec((rows, LANES), lambda h, i: (i, 0)),
                  pl.BlockSpec((nh, rows), lambda h, i: (0, i))],
        out_specs=v_spec,
        out_shape=jax.ShapeDtypeStruct((s, dv_total), F32),
        scratch_shapes=[pltpu.VMEM((heads, HEAD_DIM, HEAD_DIM), F32)],
        compiler_params=_params("parallel", "arbitrary"),
        name="gdn_chunk",
    )(q, k, v, gates, gct)


def _gdn_post_kernel(o_ref, z_ref, nw_ref, out_ref):
    nw = nw_ref[...]
    for hh in range(o_ref.shape[1] // HEAD_DIM):
        sl = slice(hh * HEAD_DIM, (hh + 1) * HEAD_DIM)
        o = o_ref[:, sl]
        ms = jnp.mean(o * o, axis=-1, keepdims=True)
        out_ref[:, sl] = ((o * lax.rsqrt(ms + NORM_EPS)) * nw * _silu(z_ref[:, sl])).astype(out_ref.dtype)


def _gdn_post(o, proj, z_col0, norm_w, *, ts=256):
    s, dv = o.shape
    zb = z_col0 // dv
    return pl.pallas_call(
        _gdn_post_kernel,
        grid=(s // ts,),
        in_specs=[pl.BlockSpec((ts, dv), lambda i: (i, 0)),
                  pl.BlockSpec((ts, dv), lambda i: (i, zb)),
                  pl.BlockSpec((1, HEAD_DIM), lambda i: (0, 0))],
        out_specs=pl.BlockSpec((ts, dv), lambda i: (i, 0)),
        out_shape=jax.ShapeDtypeStruct((s, dv), BF16),
        compiler_params=_params("parallel"),
        name="gdn_post",
    )(o, proj, norm_w.reshape(1, HEAD_DIM))


def _gated_deltanet(x, sc, sh, w_in, conv_w, layer, a_log, dt_bias, norm_w):
    nvh = a_log.shape[0]
    dv = nvh * HEAD_DIM
    conv_ch = conv_w.shape[2]
    dqk = (conv_ch - dv) // 2
    n_main = conv_ch + dv
    proj = _mod_matmul(x, sc, sh, w_in, layer, 0, n_main, name="gdn_in_proj")
    ba = _mod_matmul(x, sc, sh, w_in[layer, :, n_main:][None], 0, 0, 2 * nvh, name="gdn_in_gates")
    q = _gdn_conv(proj, conv_w, layer, 0, dqk, normalize=True, scale=HEAD_DIM ** -0.5)
    k = _gdn_conv(proj, conv_w, layer, dqk, dqk, normalize=True, scale=1.0)
    v = _gdn_conv(proj, conv_w, layer, 2 * dqk, dv, normalize=False, scale=1.0)
    gates = _gdn_gates(ba, a_log, dt_bias)
    o = _gdn_chunk(q, k, v, gates, gates[:, nvh:2 * nvh].T)
    return _gdn_post(o, proj, conv_ch, norm_w)


def _rope(x, c, sg):
    return x * c + pltpu.roll(x, HEAD_DIM // 2, 1) * sg


def _nsa_prep_kernel(p_ref, c_ref, s_ref, q_ref, kvc_ref, ks_ref, vs_ref, kw_ref, vw_ref, *, dq, dkv):
    c = c_ref[...]
    sg = s_ref[...]
    dh = HEAD_DIM

    def head(col0, hh, rope, mul):
        x = p_ref[:, col0 + hh * dh:col0 + (hh + 1) * dh]
        if rope:
            x = _rope(x, c, sg)
        return x if mul == 1.0 else x * mul

    for hh in range(dq // dh):
        q_ref[:, hh * dh:(hh + 1) * dh] = head(0, hh, True, dh ** -0.5 * LOG2E).astype(q_ref.dtype)
    for hh in range(dkv // dh):
        sl = slice(hh * dh, (hh + 1) * dh)
        kvc_ref[0, hh] = head(dq, hh, True, 1.0)
        kvc_ref[1, hh] = head(dq + dkv, hh, False, 1.0)
        ks_ref[:, sl] = head(dq + 2 * dkv, hh, True, 1.0).astype(ks_ref.dtype)
        vs_ref[:, sl] = head(dq + 3 * dkv, hh, False, 1.0).astype(vs_ref.dtype)
        kw_ref[:, sl] = head(dq + 4 * dkv, hh, True, 1.0).astype(kw_ref.dtype)
        vw_ref[:, sl] = head(dq + 5 * dkv, hh, False, 1.0).astype(vw_ref.dtype)


def _nsa_prep(proj, cos2, sin2, dq, dkv, *, ts=256):
    s, n = proj.shape
    hk = dkv // HEAD_DIM
    kv_spec = pl.BlockSpec((ts, dkv), lambda i: (i, 0))
    tab = pl.BlockSpec((ts, HEAD_DIM), lambda i: (i, 0))
    kv16 = jax.ShapeDtypeStruct((s, dkv), BF16)
    return pl.pallas_call(
        functools.partial(_nsa_prep_kernel, dq=dq, dkv=dkv),
        grid=(s // ts,),
        in_specs=[pl.BlockSpec((ts, n), lambda i: (i, 0)), tab, tab],
        out_specs=[pl.BlockSpec((ts, dq), lambda i: (i, 0)),
                   pl.BlockSpec((2, hk, ts, HEAD_DIM), lambda i: (0, 0, i, 0)),
                   kv_spec, kv_spec, kv_spec, kv_spec],
        out_shape=[jax.ShapeDtypeStruct((s, dq), BF16),
                   jax.ShapeDtypeStruct((2, hk, s, HEAD_DIM), F32),
                   kv16, kv16, kv16, kv16],
        compiler_params=_params("parallel"),
        name="nsa_prep",
    )(proj, cos2, sin2)


def _nsa_compress_kernel(x_ref, pe_ref, w1_ref, w2_ref, o_ref):
    x = x_ref[0, 0]
    n = x.shape[0]
    a = _dot(x + pe_ref[0, 0:1, :], w1_ref[0, 0])
    b = _dot(x + pe_ref[0, 1:2, :], w1_ref[0, 1])
    y = a + pltpu.roll(b, n - 1, 0)
    o_ref[0, 0] = _dot(_silu(y), w2_ref[0])


def _nsa_compress(x, pe, w1, w2):
    _, hk, ng, gw = x.shape
    half = CMP_LEN // 2
    pe2 = pe.reshape(2, 2, half * HEAD_DIM)
    w1r = w1.reshape(2, 2, half * HEAD_DIM, HEAD_DIM)
    return pl.pallas_call(
        _nsa_compress_kernel,
        grid=(2, hk),
        in_specs=[pl.BlockSpec((1, 1, ng, gw), lambda t, h: (t, h, 0, 0)),
                  pl.BlockSpec((1, 2, gw), lambda t, h: (t, 0, 0)),
                  pl.BlockSpec((1, 2, gw, HEAD_DIM), lambda t, h: (t, 0, 0, 0)),
                  pl.BlockSpec((1, HEAD_DIM, HEAD_DIM), lambda t, h: (t, 0, 0))],
        out_specs=pl.BlockSpec((1, 1, ng, HEAD_DIM), lambda t, h: (t, h, 0, 0)),
        out_shape=jax.ShapeDtypeStruct((2, hk, ng, HEAD_DIM), F32),
        compiler_params=_params("parallel", "parallel"),
        name="nsa_compress",
    )(x, pe2, w1r, w2)


def _nsa_cmp_select_kernel(q_ref, kc_ref, vc_ref, ov_ref, o_ref, sel_ref):
    tq = q_ref.shape[0]
    nc = kc_ref.shape[2]
    nbp = ov_ref.shape[1]
    t0 = pl.program_id(0) * tq
    tpos = t0 + lax.broadcasted_iota(jnp.int32, (tq, nc), 0)
    nidx = lax.broadcasted_iota(jnp.int32, (tq, nc), 1)
    valid = (nidx * CMP_STRIDE + (CMP_LEN - 1)) <= tpos
    kc = kc_ref[0, 0].astype(BF16)
    vc = vc_ref[0, 0].astype(BF16)
    psum = jnp.zeros((tq, nc), F32)
    q4 = jnp.concatenate([q_ref[:, g * HEAD_DIM:(g + 1) * HEAD_DIM] for g in range(NSA_GROUP)], axis=0)
    s_all = _dot_nt(q4, kc)
    probs = []
    for g in range(NSA_GROUP):
        s = jnp.where(valid, s_all[g * tq:(g + 1) * tq, :], NEG_BIG)
        m = jnp.max(s, axis=-1, keepdims=True)
        p = jnp.where(valid, jnp.exp2(s - m), 0.0)
        den = jnp.sum(p, axis=-1, keepdims=True)
        p = p / jnp.where(den > 0.0, den, 1.0)
        probs.append(p.astype(BF16))
        psum = psum + p
    o_all = jnp.dot(jnp.concatenate(probs, axis=0), vc, preferred_element_type=F32)
    for g in range(NSA_GROUP):
        o_ref[:, g * HEAD_DIM:(g + 1) * HEAD_DIM] = o_all[g * tq:(g + 1) * tq, :]
    imp = _dot_sel(psum, ov_ref[...])
    imp_t = imp.T
    j = lax.broadcasted_iota(jnp.int32, (nbp, tq), 0)
    cur = (t0 + lax.broadcasted_iota(jnp.int32, (nbp, tq), 1)) // SLC_LEN
    forced = (j == 0) | (j == cur) | (j == cur - 1)
    work = jnp.where(j <= cur, jnp.where(forced, jnp.inf, imp_t), -jnp.inf)
    jf = j.astype(F32)
    for _ in range(SLC_TOP):
        m = jnp.max(work, axis=0, keepdims=True)
        first = jnp.min(jnp.where(work == m, jf, float(nbp)), axis=0, keepdims=True)
        work = jnp.where(jf == first, -jnp.inf, work)
    sel_ref[0] = jnp.where((j <= cur) & (work == -jnp.inf), 1.0, 0.0).T.astype(sel_ref.dtype)


def _nsa_cmp_select(q, cmp, ov, *, tq=512):
    s, dq = q.shape
    _, hk, nc, _ = cmp.shape
    nbp = ov.shape[1]
    gw = NSA_GROUP * HEAD_DIM
    return pl.pallas_call(
        _nsa_cmp_select_kernel,
        grid=(s // tq, hk),
        in_specs=[pl.BlockSpec((tq, gw), lambda i, h: (i, h)),
                  pl.BlockSpec((1, 1, nc, HEAD_DIM), lambda i, h: (0, h, 0, 0)),
                  pl.BlockSpec((1, 1, nc, HEAD_DIM), lambda i, h: (1, h, 0, 0)),
                  pl.BlockSpec((nc, nbp), lambda i, h: (0, 0))],
        out_specs=[pl.BlockSpec((tq, gw), lambda i, h: (i, h)),
                   pl.BlockSpec((1, tq, nbp), lambda i, h: (h, i, 0))],
        out_shape=[jax.ShapeDtypeStruct((s, dq), F32),
                   jax.ShapeDtypeStruct((hk, s, nbp), BF16)],
        compiler_params=_params("parallel", "parallel"),
        name="nsa_cmp_select",
    )(q, cmp, cmp, ov)


ATTN_ROWS = 64


def _masked_attn_kernel(*refs, tq, tk, sub, selected):
    if selected:
        (qi_ref, kj_ref, first_ref, last_ref, q_ref, k_ref, v_ref, sel_ref, o_ref,
         q4, bias, s_scr, p_scr, a_scr, m_scr, acc) = refs
    else:
        (qi_ref, kj_ref, first_ref, last_ref, q_ref, k_ref, v_ref, o_ref,
         q4, bias, s_scr, p_scr, a_scr, m_scr, acc) = refs
    t = pl.program_id(1)
    qi = qi_ref[t]
    kj = kj_ref[t]
    dh = HEAD_DIM

    @pl.when(first_ref[t] == 1)
    def _():
        for g in range(NSA_GROUP):
            q4[g * tq:(g + 1) * tq, :] = q_ref[:, g * dh:(g + 1) * dh]
        m_scr[...] = jnp.full_like(m_scr, NEG_BIG)
        acc[...] = jnp.zeros_like(acc)

    tpos = qi * tq + lax.broadcasted_iota(jnp.int32, (tq, tk), 0)
    kpos = kj * tk + lax.broadcasted_iota(jnp.int32, (tq, tk), 1)
    if selected:
        nbp = sel_ref.shape[2]
        expand = jnp.where(lax.broadcasted_iota(jnp.int32, (nbp, tk), 0)
                           == (kj * tk + lax.broadcasted_iota(jnp.int32, (nbp, tk), 1)) // SLC_LEN, 1.0, 0.0)
        picked = jnp.dot(sel_ref[0], expand.astype(BF16), preferred_element_type=F32)
        mask = (picked > 0.5) & (kpos <= tpos)
    else:
        mask = (kpos <= tpos) & (kpos > tpos - WINDOW)
    bias[...] = jnp.where(mask, 0.0, NEG_BIG)

    nsub = tk // sub
    for u in range(nsub):
        cols = slice(u * sub, (u + 1) * sub)
        s_scr[:, cols] = lax.dot_general(q4[...], k_ref[cols, :], (((1,), (1,)), ((), ())),
                                         preferred_element_type=F32)
    rb = ATTN_ROWS
    for u in range(nsub):
        cols = slice(u * sub, (u + 1) * sub)
        for r0 in range(0, NSA_GROUP * tq, rb):
            rows = slice(r0, r0 + rb)
            brow = r0 % tq
            s = s_scr[rows, cols] + bias[brow:brow + rb, cols]
            m_prev = m_scr[rows, :]
            m_new = jnp.maximum(m_prev, jnp.max(s, axis=-1, keepdims=True))
            a_scr[u, rows, :] = jnp.exp2(m_prev - m_new)
            p_scr[rows, cols] = jnp.exp2(s - jnp.concatenate([m_new] * (sub // LANES), axis=1)).astype(BF16)
            m_scr[rows, :] = m_new
        v_one = jnp.concatenate([v_ref[cols, :], jnp.ones((sub, dh), BF16)], axis=1)
        alpha = jnp.concatenate([a_scr[u]] * 2, axis=1)
        acc[...] = alpha * acc[...] + jnp.dot(p_scr[:, cols], v_one, preferred_element_type=F32)

    @pl.when(last_ref[t] == 1)
    def _():
        for g in range(NSA_GROUP):
            rows = slice(g * tq, (g + 1) * tq)
            o_ref[:, g * dh:(g + 1) * dh] = acc[rows, :dh] / acc[rows, dh:]


def _masked_attn(q, k, v, sel, *, tq, tk, sub):
    s, dq = q.shape
    hk = k.shape[1] // HEAD_DIM
    gw = NSA_GROUP * HEAD_DIM
    selected = sel is not None
    pairs = []
    for i in range(s // tq):
        lo = 0 if selected else max(i * tq - (WINDOW - 1), 0) // tk
        hi = (i * tq + tq - 1) // tk
        pairs += [(i, j, int(j == lo), int(j == hi)) for j in range(lo, hi + 1)]
    sched = [jnp.asarray([p[c] for p in pairs], jnp.int32) for c in range(4)]

    in_specs = [pl.BlockSpec((tq, gw), lambda h, t, qi, kj, fi, la: (qi[t], h)),
                pl.BlockSpec((tk, HEAD_DIM), lambda h, t, qi, kj, fi, la: (kj[t], h)),
                pl.BlockSpec((tk, HEAD_DIM), lambda h, t, qi, kj, fi, la: (kj[t], h))]
    args = [q, k, v]
    if selected:
        in_specs.append(pl.BlockSpec((1, tq, sel.shape[2]), lambda h, t, qi, kj, fi, la: (h, qi[t], 0)))
        args.append(sel)
    rows = NSA_GROUP * tq
    return pl.pallas_call(
        functools.partial(_masked_attn_kernel, tq=tq, tk=tk, sub=sub, selected=selected),
        grid_spec=pltpu.PrefetchScalarGridSpec(
            num_scalar_prefetch=4,
            grid=(hk, len(pairs)),
            in_specs=in_specs,
            out_specs=pl.BlockSpec((tq, gw), lambda h, t, qi, kj, fi, la: (qi[t], h)),
            scratch_shapes=[pltpu.VMEM((rows, HEAD_DIM), BF16),
                            pltpu.VMEM((tq, tk), F32),
                            pltpu.VMEM((rows, tk), F32),
                            pltpu.VMEM((rows, tk), BF16),
                            pltpu.VMEM((tk // sub, rows, LANES), F32),
                            pltpu.VMEM((rows, LANES), F32),
                            pltpu.VMEM((rows, 2 * HEAD_DIM), F32)]),
        out_shape=jax.ShapeDtypeStruct((s, dq), F32),
        compiler_params=_params("parallel", "arbitrary"),
        name="nsa_selected_attn" if selected else "nsa_window_attn",
    )(*sched, *args)


def _nsa_combine_kernel(oc_ref, os_ref, ow_ref, g_ref, out_ref):
    gates = jax.nn.sigmoid(g_ref[...])
    ng, dq = g_ref.shape[1], oc_ref.shape[1]
    row = lax.broadcasted_iota(jnp.int32, (ng, dq), 0)
    head3 = (lax.broadcasted_iota(jnp.int32, (ng, dq), 1) // HEAD_DIM) * N_BRANCH
    acc = None
    for b, ref in enumerate((oc_ref, os_ref, ow_ref)):
        gb = _dot_sel(gates, jnp.where(row == head3 + b, 1.0, 0.0))
        term = gb * ref[...]
        acc = term if acc is None else acc + term
    out_ref[...] = acc.astype(out_ref.dtype)


def _nsa_combine(oc, osel, ow, gates, *, ts=256):
    s, dq = oc.shape
    spec = pl.BlockSpec((ts, dq), lambda i: (i, 0))
    return pl.pallas_call(
        _nsa_combine_kernel,
        grid=(s // ts,),
        in_specs=[spec, spec, spec, pl.BlockSpec((ts, gates.shape[1]), lambda i: (i, 0))],
        out_specs=spec,
        out_shape=jax.ShapeDtypeStruct((s, dq), BF16),
        compiler_params=_params("parallel"),
        name="nsa_combine",
    )(oc, osel, ow, gates)


def _nsa_attention(x, sc, sh, cos2, sin2, w_in, layer, cmp_pe, cmp_w1, cmp_w2):
    s, d = x.shape
    dq = d
    hk = d // HEAD_DIM // NSA_GROUP
    dkv = hk * HEAD_DIM
    n_main = dq + 6 * dkv
    ngate = w_in.shape[2] - n_main
    proj = _mod_matmul(x, sc, sh, w_in, layer, 0, n_main, name="nsa_in_proj")
    w_gate = jnp.pad(w_in[layer, :, n_main:], ((0, 0), (0, LANES - ngate)))[None]
    gates = _mod_matmul(x, sc, sh, w_gate, 0, 0, LANES, name="nsa_in_gates")
    q, kvc, ks, vs, kw, vw = _nsa_prep(proj, cos2, sin2, dq, dkv)
    groups = kvc.reshape(2, hk, s // CMP_STRIDE, CMP_STRIDE * HEAD_DIM)
    cmp = _nsa_compress(groups, cmp_pe, cmp_w1, cmp_w2)

    nc = s // CMP_STRIDE
    n_slc = s // SLC_LEN
    nbp = -(-n_slc // LANES) * LANES
    c_start = CMP_STRIDE * jnp.arange(nc)
    s_start = SLC_LEN * jnp.arange(nbp)
    overlap = jnp.clip(jnp.minimum(c_start[:, None] + CMP_LEN, s_start[None, :] + SLC_LEN)
                       - jnp.maximum(c_start[:, None], s_start[None, :]), 0, None).astype(F32) / CMP_LEN
    o_c, sel = _nsa_cmp_select(q, cmp, overlap.astype(BF16))
    o_s = _masked_attn(q, ks, vs, sel, tq=512, tk=1024, sub=512)
    o_w = _masked_attn(q, kw, vw, None, tq=256, tk=256, sub=256)
    return _nsa_combine(o_c, o_s, o_w, gates)


def kernel(x, c, positions, mod_w, mod_b, ln_g, ln_b, ffn_w_gu, ffn_w_down, gdn_w_in, gdn_conv_w,
           gdn_a_log, gdn_dt_bias, gdn_norm_w, gdn_w_out, nsa_w_in, nsa_cmp_pe, nsa_cmp_w1,
           nsa_cmp_w2, nsa_w_out):
    bsz, s, d = x.shape
    assert bsz == 1
    depth = mod_w.shape[0]
    alpha = (2 * depth) ** 0.25

    inv = ROPE_THETA ** (-jnp.arange(0, HEAD_DIM, 2, dtype=F32) / HEAD_DIM)
    ang = positions[0].astype(F32)[:, None] * inv
    cos, sin = jnp.cos(ang), jnp.sin(ang)
    cos2 = jnp.concatenate([cos, cos], axis=-1)
    sin2 = jnp.concatenate([-sin, sin], axis=-1)

    mod = _modulation(c, mod_w, mod_b)
    gdn_w_out, nsa_w_out, ffn_w_down = [w.astype(BF16) for w in (gdn_w_out, nsa_w_out, ffn_w_down)]
    xs = x[0]
    for i in range(depth):
        sh1, sc1, ga1, sh2, sc2, ga2 = [mod[i, :, r * d:(r + 1) * d] for r in range(6)]
        j = i // N_MIXERS
        if i % N_MIXERS == 0:
            y = _gated_deltanet(xs, sc1, sh1, gdn_w_in, gdn_conv_w, j, gdn_a_log[j],
                                gdn_dt_bias[j], gdn_norm_w[j])
            w_out = gdn_w_out
        else:
            y = _nsa_attention(xs, sc1, sh1, cos2, sin2, nsa_w_in, j, nsa_cmp_pe[j],
                               nsa_cmp_w1[j], nsa_cmp_w2[j])
            w_out = nsa_w_out
        xs = _matmul_ln(y, w_out, j, xs, ga1, ln_g[i, 0:1], ln_b[i, 0:1], alpha=alpha)
        a = _swiglu_up(xs, sc2, sh2, ffn_w_gu, i)
        xs = _matmul_ln(a, ffn_w_down, i, xs, ga2, ln_g[i, 1:2], ln_b[i, 1:2], alpha=alpha)
    return xs[None]
```

```python
import functools

import jax
import jax.numpy as jnp
from jax import lax
from jax.experimental import pallas as pl
from jax.experimental.pallas import tpu as pltpu

F32 = jnp.float32
BF16 = jnp.bfloat16
HIGHEST = lax.Precision.HIGHEST

HEAD_DIM = 128
N_MIXERS = 2
GDN_CONV = 4
GDN_BLOCK = 128
NSA_GROUP = 4
N_BRANCH = 3
CMP_LEN = 32
CMP_STRIDE = 16
SLC_LEN = 64
SLC_TOP = 16
WINDOW = 512
ROPE_THETA = 10000.0
LN_EPS = 1e-5
NORM_EPS = 1e-6

LOG2E = 1.4426950408889634
MATMUL_ROWS = 1024
LANES = 128
VMEM_LIMIT_BYTES = 56 * 1024 * 1024
NEG_BIG = -1e30
MASK_NEG = -2.0 ** 100


def _params(*sem):
    return pltpu.CompilerParams(dimension_semantics=sem, vmem_limit_bytes=VMEM_LIMIT_BYTES)


def _silu(v):
    return v * jax.nn.sigmoid(v)


def _dot(a, b):
    return jnp.dot(a.astype(BF16), b.astype(BF16), preferred_element_type=F32)


def _dot_nt(a, b):
    return lax.dot_general(a.astype(BF16), b.astype(BF16), (((1,), (1,)), ((), ())),
                           preferred_element_type=F32)


def _split(a):
    hi = a.astype(BF16)
    lo = (a - hi.astype(F32)).astype(BF16)
    return hi, lo


def _dot3(a, b):
    ah, al = _split(a)
    bh, bl = _split(b)
    d = functools.partial(jnp.dot, preferred_element_type=F32)
    return d(ah, bh) + (d(ah, bl) + d(al, bh))


def _dot_sel(a, onehot):
    ah, al = _split(a)
    e = onehot.astype(BF16)
    d = functools.partial(jnp.dot, preferred_element_type=F32)
    return d(ah, e) + d(al, e)


def _sel_dot(onehot, b):
    bh, bl = _split(b)
    e = onehot.astype(BF16)
    d = functools.partial(jnp.dot, preferred_element_type=F32)
    return d(e, bh) + d(e, bl)


def _mod_kernel(c_ref, w_ref, b_ref, o_ref):
    cond = _silu(c_ref[...])
    o_ref[0] = _dot3(cond, w_ref[0]) + b_ref[0]


def _modulation(c, mod_w, mod_b):
    depth, d, n = mod_w.shape
    tn = 1536
    c8 = jnp.broadcast_to(c[:1], (8, d))
    out = pl.pallas_call(
        _mod_kernel,
        grid=(depth, n // tn),
        in_specs=[pl.BlockSpec((8, d), lambda l, j: (0, 0)),
                  pl.BlockSpec((1, d, tn), lambda l, j: (l, 0, j)),
                  pl.BlockSpec((1, 1, tn), lambda l, j: (l, 0, j))],
        out_specs=pl.BlockSpec((1, 8, tn), lambda l, j: (l, 0, j)),
        out_shape=jax.ShapeDtypeStruct((depth, 8, n), F32),
        compiler_params=_params("parallel", "parallel"),
        name="modulation",
    )(c8, mod_w, mod_b.reshape(depth, 1, n))
    return out[:, 0:1, :]


def _mod_matmul_kernel(x_ref, sc_ref, sh_ref, w_ref, o_ref, h_scr):
    @pl.when(pl.program_id(1) == 0)
    def _():
        h_scr[...] = (x_ref[...] * (1.0 + sc_ref[...]) + sh_ref[...]).astype(BF16)

    o_ref[...] = jnp.dot(h_scr[...], w_ref[...].astype(BF16),
                         preferred_element_type=F32).astype(o_ref.dtype)


def _mod_matmul(x, sc, sh, w, layer, col0, n, *, tm=MATMUL_ROWS, tn=512, name="mod_matmul"):
    m, k = x.shape
    tm = min(tm, m)
    tn = min(tn, n)
    cb0 = col0 // tn
    assert col0 % tn == 0 and n % tn == 0 and m % tm == 0
    return pl.pallas_call(
        _mod_matmul_kernel,
        grid=(m // tm, n // tn),
        in_specs=[pl.BlockSpec((tm, k), lambda i, j: (i, 0)),
                  pl.BlockSpec((1, k), lambda i, j: (0, 0)),
                  pl.BlockSpec((1, k), lambda i, j: (0, 0)),
                  pl.BlockSpec((None, k, tn), lambda i, j: (layer, 0, j + cb0))],
        out_specs=pl.BlockSpec((tm, tn), lambda i, j: (i, j)),
        out_shape=jax.ShapeDtypeStruct((m, n), F32),
        scratch_shapes=[pltpu.VMEM((tm, k), BF16)],
        compiler_params=_params("parallel", "arbitrary"),
        name=name,
    )(x, sc, sh, w)


def _swiglu_up_kernel(x_ref, sc_ref, sh_ref, wg_ref, wu_ref, o_ref, h_scr):
    @pl.when(pl.program_id(1) == 0)
    def _():
        h_scr[...] = (x_ref[...] * (1.0 + sc_ref[...]) + sh_ref[...]).astype(BF16)

    h = h_scr[...]
    g = jnp.dot(h, wg_ref[...].astype(BF16), preferred_element_type=F32)
    u = jnp.dot(h, wu_ref[...].astype(BF16), preferred_element_type=F32)
    o_ref[...] = (_silu(g) * u).astype(o_ref.dtype)


def _swiglu_up(x, sc, sh, w_gu, layer, *, tm=MATMUL_ROWS, tn=512):
    m, k = x.shape
    tm = min(tm, m)
    dff = w_gu.shape[2] // 2
    nb = dff // tn
    assert dff % tn == 0 and m % tm == 0
    return pl.pallas_call(
        _swiglu_up_kernel,
        grid=(m // tm, nb),
        in_specs=[pl.BlockSpec((tm, k), lambda i, j: (i, 0)),
                  pl.BlockSpec((1, k), lambda i, j: (0, 0)),
                  pl.BlockSpec((1, k), lambda i, j: (0, 0)),
                  pl.BlockSpec((None, k, tn), lambda i, j: (layer, 0, j)),
                  pl.BlockSpec((None, k, tn), lambda i, j: (layer, 0, j + nb))],
        out_specs=pl.BlockSpec((tm, tn), lambda i, j: (i, j)),
        out_shape=jax.ShapeDtypeStruct((m, dff), BF16),
        scratch_shapes=[pltpu.VMEM((tm, k), BF16)],
        compiler_params=_params("parallel", "arbitrary"),
        name="swiglu_up",
    )(x, sc, sh, w_gu, w_gu)


LN_ROWS = 256
LN_MAX_TK = 1536


def _matmul_ln_kernel(a_ref, w_ref, x_ref, ga_ref, g_ref, b_ref, o_ref, *, alpha):
    kk = pl.program_id(1)
    part = jnp.dot(a_ref[...].astype(BF16), w_ref[...].astype(BF16), preferred_element_type=F32)

    @pl.when(kk == 0)
    def _():
        o_ref[...] = part

    @pl.when(kk > 0)
    def _():
        o_ref[...] += part

    @pl.when(kk == pl.num_programs(1) - 1)
    def _():
        for r0 in range(0, o_ref.shape[0], LN_ROWS):
            rows = slice(r0, r0 + LN_ROWS)
            r = alpha * x_ref[rows, :] + (1.0 + ga_ref[...]) * o_ref[rows, :]
            mu = jnp.mean(r, axis=-1, keepdims=True)
            d = r - mu
            var = jnp.mean(d * d, axis=-1, keepdims=True)
            o_ref[rows, :] = d * lax.rsqrt(var + LN_EPS) * g_ref[...] + b_ref[...]


def _matmul_ln(a, w, layer, x, ga, g, b, *, alpha, tm=512):
    m, k = a.shape
    n = w.shape[2]
    tm = min(tm, m)
    tk = next(t for t in range(min(k, LN_MAX_TK), 0, -LANES) if k % t == 0)
    assert m % tm == 0 and k % tk == 0 and tm % LN_ROWS == 0
    return pl.pallas_call(
        functools.partial(_matmul_ln_kernel, alpha=alpha),
        grid=(m // tm, k // tk),
        in_specs=[pl.BlockSpec((tm, tk), lambda i, kk: (i, kk)),
                  pl.BlockSpec((None, tk, n), lambda i, kk: (layer, kk, 0)),
                  pl.BlockSpec((tm, n), lambda i, kk: (i, 0)),
                  pl.BlockSpec((1, n), lambda i, kk: (0, 0)),
                  pl.BlockSpec((1, n), lambda i, kk: (0, 0)),
                  pl.BlockSpec((1, n), lambda i, kk: (0, 0))],
        out_specs=pl.BlockSpec((tm, n), lambda i, kk: (i, 0)),
        out_shape=jax.ShapeDtypeStruct((m, n), F32),
        compiler_params=_params("parallel", "arbitrary"),
        name="matmul_ln",
    )(a, w, x, ga, g, b)


def _gdn_conv_kernel(xp_ref, xc_ref, w_ref, o_ref, buf, *, normalize, scale):
    ts, tc = xc_ref.shape
    buf[0:8, :] = jnp.where(pl.program_id(0) > 0, xp_ref[...], 0.0)
    buf[8:, :] = xc_ref[...]
    w = w_ref[...]
    y = w[0:1, :] * buf[5:5 + ts, :]
    for i in range(1, GDN_CONV):
        y = y + w[i:i + 1, :] * buf[5 + i:5 + i + ts, :]
    y = _silu(y)
    if normalize:
        for hh in range(tc // HEAD_DIM):
            seg = y[:, hh * HEAD_DIM:(hh + 1) * HEAD_DIM]
            ss = jnp.sum(seg * seg, axis=-1, keepdims=True)
            o_ref[:, hh * HEAD_DIM:(hh + 1) * HEAD_DIM] = seg * (lax.rsqrt(ss + NORM_EPS) * scale)
    else:
        o_ref[...] = y


def _gdn_conv(proj, conv_w, layer, col0, ncols, *, normalize, scale, ts=512, tc=512):
    s = proj.shape[0]
    cb0 = col0 // tc
    hb = ts // 8
    return pl.pallas_call(
        functools.partial(_gdn_conv_kernel, normalize=normalize, scale=scale),
        grid=(s // ts, ncols // tc),
        in_specs=[pl.BlockSpec((8, tc), lambda i, j: (jnp.maximum(i * hb - 1, 0), j + cb0)),
                  pl.BlockSpec((ts, tc), lambda i, j: (i, j + cb0)),
                  pl.BlockSpec((None, GDN_CONV, tc), lambda i, j: (layer, 0, j + cb0))],
        out_specs=pl.BlockSpec((ts, tc), lambda i, j: (i, j)),
        out_shape=jax.ShapeDtypeStruct((s, ncols), F32),
        scratch_shapes=[pltpu.VMEM((ts + 8, tc), F32)],
        compiler_params=_params("parallel", "parallel"),
        name="gdn_conv",
    )(proj, proj, conv_w)


def _gdn_gates_kernel(ba_ref, alog_ref, dtb_ref, out_ref):
    ts = ba_ref.shape[0]
    nh = ba_ref.shape[1] // 2
    x = ba_ref[...]
    beta = jax.nn.sigmoid(x[:, :nh])
    z = x[:, nh:] + dtb_ref[...]
    softplus = jnp.maximum(z, 0.0) + jnp.log1p(jnp.exp(-jnp.abs(z)))
    g = -jnp.exp(alog_ref[...]) * softplus
    ri = lax.broadcasted_iota(jnp.int32, (ts, ts), 0)
    ci = lax.broadcasted_iota(jnp.int32, (ts, ts), 1)
    same = (ri // GDN_BLOCK) == (ci // GDN_BLOCK)
    gc = _sel_dot(jnp.where(same & (ci <= ri), 1.0, 0.0), g)
    gl = _sel_dot(jnp.where(same, 1.0, 0.0), g)
    out_ref[...] = jnp.concatenate([beta, gc, gl, jnp.zeros((ts, LANES - 3 * nh), F32)], axis=1)


def _gdn_gates(ba, a_log, dt_bias, *, ts=512):
    s, two_h = ba.shape
    nh = two_h // 2
    assert 3 * nh <= LANES
    return pl.pallas_call(
        _gdn_gates_kernel,
        grid=(s // ts,),
        in_specs=[pl.BlockSpec((ts, two_h), lambda i: (i, 0)),
                  pl.BlockSpec((1, nh), lambda i: (0, 0)),
                  pl.BlockSpec((1, nh), lambda i: (0, 0))],
        out_specs=pl.BlockSpec((ts, LANES), lambda i: (i, 0)),
        out_shape=jax.ShapeDtypeStruct((s, LANES), F32),
        compiler_params=_params("parallel"),
        name="gdn_gates",
    )(ba, a_log.reshape(1, nh), dt_bias.reshape(1, nh))


def _bdot(a, b):
    return jnp.einsum("hij,hjk->hik", a.astype(BF16), b.astype(BF16), preferred_element_type=F32)


def _bdot3(a, b):
    ah, al = _split(a)
    bh, bl = _split(b)
    return jnp.einsum("hij,hjk->hik", jnp.concatenate([ah, ah, al], axis=2),
                      jnp.concatenate([bh, bl, bh], axis=1), preferred_element_type=F32)


def _unit_lower_inverse(low, ri, ci):
    n = low.shape[-1]

    def blk(b):
        return ((ri // b) == (ci // b))[None]

    eye = jnp.where(ri == ci, 1.0, 0.0)[None]
    n1 = jnp.where(blk(16), -low, 0.0)
    n2 = _bdot(n1, n1)
    n4 = _bdot(n2, n2)
    n8 = _bdot(n4, n4)
    t = eye + n1
    t = t + _bdot(n2, t)
    t = t + _bdot(n4, t)
    t = t + _bdot(n8, t)
    b = 32
    while b <= n:
        off = jnp.where(blk(b) & jnp.logical_not(blk(b // 2)), low, 0.0)
        t = t - _bdot(t, _bdot(off, t))
        b *= 2
    return t


def _gdn_chunk_kernel(q_ref, k_ref, v_ref, gates_ref, gct_ref, o_ref, state, *, heads):
    hg = pl.program_id(0)
    rows = q_ref.shape[0]
    nh = gct_ref.shape[0]
    dh = HEAD_DIM
    rep = rows // dh

    @pl.when(pl.program_id(1) == 0)
    def _():
        state[...] = jnp.zeros_like(state)

    ri = lax.broadcasted_iota(jnp.int32, (rows, rows), 0)
    ci = lax.broadcasted_iota(jnp.int32, (rows, rows), 1)
    lower = (ci <= ri)[None]
    strict = (ci < ri)[None]
    dot = functools.partial(jnp.dot, preferred_element_type=F32)

    gates_hi, gates_lo = _split(gates_ref[...])
    width = heads * 3 * dh
    prow = lax.broadcasted_iota(jnp.int32, (LANES, width), 0)
    pcol = lax.broadcasted_iota(jnp.int32, (LANES, width), 1)
    pick = jnp.where(prow == ((pcol // dh) % 3) * nh + hg * heads + pcol // (3 * dh), 1.0, 0.0).astype(BF16)
    cols = dot(jnp.concatenate([gates_hi, gates_lo], axis=1),
               jnp.concatenate([pick, pick], axis=0))

    def per_head(j):
        return jnp.stack([cols[:, (3 * hl + j) * dh:(3 * hl + j + 1) * dh] for hl in range(heads)])

    beta, gcol, glast = per_head(0), per_head(1), per_head(2)
    gct_hi, gct_lo = _split(gct_ref[...])
    trow = lax.broadcasted_iota(jnp.int32, (heads * rows, nh), 0) // rows
    tcol = lax.broadcasted_iota(jnp.int32, (heads * rows, nh), 1)
    pick_t = jnp.where(tcol == hg * heads + trow, 1.0, 0.0).astype(BF16)
    g_j = dot(jnp.concatenate([pick_t, pick_t], axis=1),
              jnp.concatenate([gct_hi, gct_lo], axis=0)).reshape(heads, rows, rows)
    g_i = jnp.concatenate([gcol] * rep, axis=2)
    beta_i = jnp.concatenate([beta] * rep, axis=2)
    decay = jnp.where(lower, jnp.exp(jnp.where(lower, g_i - g_j, 0.0)), 0.0)

    def pairs(ref):
        return [ref[:, p * dh:(p + 1) * dh] for p in range(heads // 2)]

    def nt(a, b):
        return jnp.einsum("pid,pjd->pij", a.astype(BF16), b.astype(BF16), preferred_element_type=F32)

    def both(x):
        return jnp.stack([x[hl // 2] for hl in range(heads)])

    qp, kp = jnp.stack(pairs(q_ref)), jnp.stack(pairs(k_ref))
    q, k = both(qp), both(kp)
    kk, qk = both(nt(kp, kp)), both(nt(qp, kp))
    v = jnp.stack([v_ref[:, hl * dh:(hl + 1) * dh] for hl in range(heads)])

    low = jnp.where(strict, kk * beta_i * decay, 0.0)
    t0 = _unit_lower_inverse(low, ri, ci)
    rhs = jnp.concatenate([v * beta, k * (beta * jnp.exp(gcol))], axis=2)
    x = _bdot(t0, rhs)
    x = x + _bdot(t0, rhs - x - _bdot3(low, x))
    u, w = x[:, :, :dh], x[:, :, dh:]
    a_x = _bdot(jnp.where(lower, qk * decay, 0.0), x)
    q_eff = q * jnp.exp(gcol) - a_x[:, :, dh:]
    k_g = k * jnp.exp(glast - gcol)

    s = state[...]
    o = _bdot(q_eff, s) + a_x[:, :, :dh]
    v_new = u - _bdot(w, s)
    state[...] = s * jnp.exp(glast[:, 0:1, :]) + jnp.einsum(
        "hck,hcv->hkv", k_g.astype(BF16), v_new.astype(BF16), preferred_element_type=F32)
    for hl in range(heads):
        o_ref[:, hl * dh:(hl + 1) * dh] = o[hl]


def _gdn_chunk(q, k, v, gates, gct, *, heads=16):
    rows = GDN_BLOCK
    s, dv_total = v.shape
    nh = dv_total // HEAD_DIM
    assert nh == 2 * (q.shape[1] // HEAD_DIM) and heads % 2 == 0 and nh % heads == 0
    qk_spec = pl.BlockSpec((rows, heads // 2 * HEAD_DIM), lambda h, i: (i, h))
    v_spec = pl.BlockSpec((rows, heads * HEAD_DIM), lambda h, i: (i, h))
    return pl.pallas_call(
        functools.partial(_gdn_chunk_kernel, heads=heads),
        grid=(nh // heads, s // rows),
        in_specs=[qk_spec, qk_spec, v_spec,
                  pl.BlockSpec((rows, LANES), lambda h, i: (i, 0)),
                  pl.BlockSpec((nh, rows), lambda h, i: (0, i))],
        out_specs=v_spec,
        out_shape=jax.ShapeDtypeStruct((s, dv_total), F32),
        scratch_shapes=[pltpu.VMEM((heads, HEAD_DIM, HEAD_DIM), F32)],
        compiler_params=_params("parallel", "arbitrary"),
        name="gdn_chunk",
    )(q, k, v, gates, gct)


def _gdn_post_kernel(o_ref, z_ref, nw_ref, out_ref):
    nw = nw_ref[...]
    for hh in range(o_ref.shape[1] // HEAD_DIM):
        sl = slice(hh * HEAD_DIM, (hh + 1) * HEAD_DIM)
        o = o_ref[:, sl]
        ms = jnp.mean(o * o, axis=-1, keepdims=True)
        out_ref[:, sl] = ((o * lax.rsqrt(ms + NORM_EPS)) * nw * _silu(z_ref[:, sl])).astype(out_ref.dtype)


def _gdn_post(o, proj, z_col0, norm_w, *, ts=256):
    s, dv = o.shape
    zb = z_col0 // dv
    return pl.pallas_call(
        _gdn_post_kernel,
        grid=(s // ts,),
        in_specs=[pl.BlockSpec((ts, dv), lambda i: (i, 0)),
                  pl.BlockSpec((ts, dv), lambda i: (i, zb)),
                  pl.BlockSpec((1, HEAD_DIM), lambda i: (0, 0))],
        out_specs=pl.BlockSpec((ts, dv), lambda i: (i, 0)),
        out_shape=jax.ShapeDtypeStruct((s, dv), BF16),
        compiler_params=_params("parallel"),
        name="gdn_post",
    )(o, proj, norm_w.reshape(1, HEAD_DIM))


def _gated_deltanet(x, sc, sh, w_in, conv_w, layer, a_log, dt_bias, norm_w):
    nvh = a_log.shape[0]
    dv = nvh * HEAD_DIM
    conv_ch = conv_w.shape[2]
    dqk = (conv_ch - dv) // 2
    n_main = conv_ch + dv
    w_main = w_in[layer, :, :n_main].astype(BF16)[None]
    proj = _mod_matmul(x, sc, sh, w_main, 0, 0, n_main, name="gdn_in_proj")
    ba = _mod_matmul(x, sc, sh, w_in[layer, :, n_main:][None], 0, 0, 2 * nvh, name="gdn_in_gates")
    q = _gdn_conv(proj, conv_w, layer, 0, dqk, normalize=True, scale=HEAD_DIM ** -0.5)
    k = _gdn_conv(proj, conv_w, layer, dqk, dqk, normalize=True, scale=1.0)
    v = _gdn_conv(proj, conv_w, layer, 2 * dqk, dv, normalize=False, scale=1.0)
    gates = _gdn_gates(ba, a_log, dt_bias)
    o = _gdn_chunk(q, k, v, gates, gates[:, nvh:2 * nvh].T)
    return _gdn_post(o, proj, conv_ch, norm_w)


def _rope(x, c, sg):
    return x * c + pltpu.roll(x, HEAD_DIM // 2, 1) * sg


def _nsa_prep_kernel(p_ref, c_ref, s_ref, q_ref, kvc_ref, ks_ref, vs_ref, kw_ref, vw_ref, *, dq, dkv):
    c = c_ref[...]
    sg = s_ref[...]
    dh = HEAD_DIM

    def head(col0, hh, rope, mul):
        x = p_ref[:, col0 + hh * dh:col0 + (hh + 1) * dh]
        if rope:
            x = _rope(x, c, sg)
        return x if mul == 1.0 else x * mul

    for hh in range(dq // dh):
        q_ref[:, hh * dh:(hh + 1) * dh] = head(0, hh, True, dh ** -0.5 * LOG2E).astype(q_ref.dtype)
    for hh in range(dkv // dh):
        sl = slice(hh * dh, (hh + 1) * dh)
        kvc_ref[0, hh] = head(dq, hh, True, 1.0)
        kvc_ref[1, hh] = head(dq + dkv, hh, False, 1.0)
        ks_ref[:, sl] = head(dq + 2 * dkv, hh, True, 1.0).astype(ks_ref.dtype)
        vs_ref[:, sl] = head(dq + 3 * dkv, hh, False, 1.0).astype(vs_ref.dtype)
        kw_ref[:, sl] = head(dq + 4 * dkv, hh, True, 1.0).astype(kw_ref.dtype)
        vw_ref[:, sl] = head(dq + 5 * dkv, hh, False, 1.0).astype(vw_ref.dtype)


def _nsa_prep(proj, cos2, sin2, dq, dkv, *, ts=256):
    s, n = proj.shape
    hk = dkv // HEAD_DIM
    kv_spec = pl.BlockSpec((ts, dkv), lambda i: (i, 0))
    tab = pl.BlockSpec((ts, HEAD_DIM), lambda i: (i, 0))
    kv16 = jax.ShapeDtypeStruct((s, dkv), BF16)
    return pl.pallas_call(
        functools.partial(_nsa_prep_kernel, dq=dq, dkv=dkv),
        grid=(s // ts,),
        in_specs=[pl.BlockSpec((ts, n), lambda i: (i, 0)), tab, tab],
        out_specs=[pl.BlockSpec((ts, dq), lambda i: (i, 0)),
                   pl.BlockSpec((2, hk, ts, HEAD_DIM), lambda i: (0, 0, i, 0)),
                   kv_spec, kv_spec, kv_spec, kv_spec],
        out_shape=[jax.ShapeDtypeStruct((s, dq), BF16),
                   jax.ShapeDtypeStruct((2, hk, s, HEAD_DIM), F32),
                   kv16, kv16, kv16, kv16],
        compiler_params=_params("parallel"),
        name="nsa_prep",
    )(proj, cos2, sin2)


def _nsa_compress_kernel(x_ref, pe_ref, w1_ref, w2_ref, o_ref):
    x = x_ref[0, 0]
    n = x.shape[0]
    a = _dot(x + pe_ref[0, 0:1, :], w1_ref[0, 0])
    b = _dot(x + pe_ref[0, 1:2, :], w1_ref[0, 1])
    y = a + pltpu.roll(b, n - 1, 0)
    o_ref[0, 0] = _dot(_silu(y), w2_ref[0])


def _nsa_compress(x, pe, w1, w2):
    _, hk, ng, gw = x.shape
    half = CMP_LEN // 2
    pe2 = pe.reshape(2, 2, half * HEAD_DIM)
    w1r = w1.reshape(2, 2, half * HEAD_DIM, HEAD_DIM)
    return pl.pallas_call(
        _nsa_compress_kernel,
        grid=(2, hk),
        in_specs=[pl.BlockSpec((1, 1, ng, gw), lambda t, h: (t, h, 0, 0)),
                  pl.BlockSpec((1, 2, gw), lambda t, h: (t, 0, 0)),
                  pl.BlockSpec((1, 2, gw, HEAD_DIM), lambda t, h: (t, 0, 0, 0)),
                  pl.BlockSpec((1, HEAD_DIM, HEAD_DIM), lambda t, h: (t, 0, 0))],
        out_specs=pl.BlockSpec((1, 1, ng, HEAD_DIM), lambda t, h: (t, h, 0, 0)),
        out_shape=jax.ShapeDtypeStruct((2, hk, ng, HEAD_DIM), F32),
        compiler_params=_params("parallel", "parallel"),
        name="nsa_compress",
    )(x, pe2, w1r, w2)


def _nsa_cmp_select_kernel(q_ref, kc_ref, vc_ref, ov_ref, o_ref, sel_ref):
    tq = q_ref.shape[0]
    nc = kc_ref.shape[2]
    nbp = ov_ref.shape[1]
    t0 = pl.program_id(0) * tq
    tpos = t0 + lax.broadcasted_iota(jnp.int32, (tq, nc), 0)
    nidx = lax.broadcasted_iota(jnp.int32, (tq, nc), 1)
    valid = (nidx * CMP_STRIDE + (CMP_LEN - 1)) <= tpos
    kc = kc_ref[0, 0].astype(BF16)
    vc = vc_ref[0, 0].astype(BF16)
    psum = jnp.zeros((tq, nc), F32)
    q4 = jnp.concatenate([q_ref[:, g * HEAD_DIM:(g + 1) * HEAD_DIM] for g in range(NSA_GROUP)], axis=0)
    s_all = _dot_nt(q4, kc)
    probs = []
    for g in range(NSA_GROUP):
        s = jnp.where(valid, s_all[g * tq:(g + 1) * tq, :], NEG_BIG)
        m = jnp.max(s, axis=-1, keepdims=True)
        p = jnp.where(valid, jnp.exp2(s - m), 0.0)
        den = jnp.sum(p, axis=-1, keepdims=True)
        p = p / jnp.where(den > 0.0, den, 1.0)
        probs.append(p.astype(BF16))
        psum = psum + p
    o_all = jnp.dot(jnp.concatenate(probs, axis=0), vc, preferred_element_type=F32)
    for g in range(NSA_GROUP):
        o_ref[:, g * HEAD_DIM:(g + 1) * HEAD_DIM] = o_all[g * tq:(g + 1) * tq, :]
    imp = _dot_sel(psum, ov_ref[...])
    imp_t = imp.T
    j = lax.broadcasted_iota(jnp.int32, (nbp, tq), 0)
    cur = (t0 + lax.broadcasted_iota(jnp.int32, (nbp, tq), 1)) // SLC_LEN
    forced = (j == 0) | (j == cur) | (j == cur - 1)
    work = jnp.where(j <= cur, jnp.where(forced, jnp.inf, imp_t), -jnp.inf)
    jf = j.astype(F32)
    for _ in range(SLC_TOP):
        m = jnp.max(work, axis=0, keepdims=True)
        first = jnp.min(jnp.where(work == m, jf, float(nbp)), axis=0, keepdims=True)
        work = jnp.where(jf == first, -jnp.inf, work)
    sel_ref[0] = jnp.where((j <= cur) & (work == -jnp.inf), 0.0, MASK_NEG).T.astype(sel_ref.dtype)


def _nsa_cmp_select(q, cmp, ov, *, tq=512):
    s, dq = q.shape
    _, hk, nc, _ = cmp.shape
    nbp = ov.shape[1]
    gw = NSA_GROUP * HEAD_DIM
    return pl.pallas_call(
        _nsa_cmp_select_kernel,
        grid=(s // tq, hk),
        in_specs=[pl.BlockSpec((tq, gw), lambda i, h: (i, h)),
                  pl.BlockSpec((1, 1, nc, HEAD_DIM), lambda i, h: (0, h, 0, 0)),
                  pl.BlockSpec((1, 1, nc, HEAD_DIM), lambda i, h: (1, h, 0, 0)),
                  pl.BlockSpec((nc, nbp), lambda i, h: (0, 0))],
        out_specs=[pl.BlockSpec((tq, gw), lambda i, h: (i, h)),
                   pl.BlockSpec((1, tq, nbp), lambda i, h: (h, i, 0))],
        out_shape=[jax.ShapeDtypeStruct((s, dq), F32),
                   jax.ShapeDtypeStruct((hk, s, nbp), BF16)],
        compiler_params=_params("parallel", "parallel"),
        name="nsa_cmp_select",
    )(q, cmp, cmp, ov)


ATTN_ROWS = 64


def _masked_attn_kernel(*refs, tq, nk, sub, selected):
    qi_ref, kj_ref, first_ref, last_ref, q_ref = refs[:5]
    k_refs, v_refs = refs[5:5 + nk], refs[5 + nk:5 + 2 * nk]
    rest = refs[5 + 2 * nk:]
    if selected:
        sel_ref, expand_ref = rest[:2]
        rest = rest[2:]
    o_ref, q4, bias, s_scr, p_scr, a_scr, m_scr, acc = rest
    t = pl.program_id(1)
    qi = qi_ref[t]
    kj = kj_ref[t]
    dh = HEAD_DIM
    tk = nk * k_refs[0].shape[0]

    @pl.when(first_ref[t] == 1)
    def _():
        for g in range(NSA_GROUP):
            q4[g * tq:(g + 1) * tq, :] = q_ref[:, g * dh:(g + 1) * dh]
        m_scr[...] = jnp.full_like(m_scr, NEG_BIG)
        acc[...] = jnp.zeros_like(acc)

    k0 = kj * k_refs[0].shape[0]

    def causal():
        tpos = qi * tq + lax.broadcasted_iota(jnp.int32, (tq, tk), 0)
        kpos = k0 + lax.broadcasted_iota(jnp.int32, (tq, tk), 1)
        return tpos, kpos

    if selected:
        picked = jnp.dot(sel_ref[0], expand_ref[...], preferred_element_type=F32)
        on_diagonal = k0 + tk - 1 > qi * tq

        @pl.when(on_diagonal)
        def _():
            tpos, kpos = causal()
            bias[...] = picked + jnp.where(kpos <= tpos, 0.0, MASK_NEG)

        @pl.when(jnp.logical_not(on_diagonal))
        def _():
            bias[...] = picked
    else:
        tpos, kpos = causal()
        bias[...] = jnp.where((kpos <= tpos) & (kpos > tpos - WINDOW) & (kpos >= 0), 0.0, MASK_NEG)

    tiles = [(r, u * sub) for r in range(nk) for u in range(k_refs[0].shape[0] // sub)]
    for n, (r, off) in enumerate(tiles):
        s_scr[:, n * sub:(n + 1) * sub] = lax.dot_general(
            q4[...], k_refs[r][off:off + sub, :], (((1,), (1,)), ((), ())), preferred_element_type=F32)
    rb = ATTN_ROWS
    for u, (r, off) in enumerate(tiles):
        cols = slice(u * sub, (u + 1) * sub)
        for r0 in range(0, NSA_GROUP * tq, rb):
            rows = slice(r0, r0 + rb)
            brow = r0 % tq
            s = s_scr[rows, cols] + bias[brow:brow + rb, cols]
            m_prev = m_scr[rows, :]
            m_new = jnp.maximum(m_prev, jnp.max(s, axis=-1, keepdims=True))
            a_scr[u, rows, :] = jnp.exp2(m_prev - m_new)
            p_scr[rows, cols] = jnp.exp2(s - jnp.concatenate([m_new] * (sub // LANES), axis=1)).astype(BF16)
            m_scr[rows, :] = m_new
        v_one = jnp.concatenate([v_refs[r][off:off + sub, :], jnp.ones((sub, dh), BF16)], axis=1)
        alpha = jnp.concatenate([a_scr[u]] * 2, axis=1)
        acc[...] = alpha * acc[...] + jnp.dot(p_scr[:, cols], v_one, preferred_element_type=F32)

    @pl.when(last_ref[t] == 1)
    def _():
        for g in range(NSA_GROUP):
            rows = slice(g * tq, (g + 1) * tq)
            o_ref[:, g * dh:(g + 1) * dh] = acc[rows, :dh] / acc[rows, dh:]


def _masked_attn(q, k, v, sel, expand, *, tq, tk, sub):
    s, dq = q.shape
    hk = k.shape[1] // HEAD_DIM
    gw = NSA_GROUP * HEAD_DIM
    selected = sel is not None
    if selected:
        nk = 1
        pairs = []
        for i in range(s // tq):
            hi = (i * tq + tq - 1) // tk
            pairs += [(i, j, int(j == 0), int(j == hi)) for j in range(hi + 1)]
    else:
        assert tq % tk == 0
        nback = -(-(WINDOW - 1) // tk)
        nk = nback + tq // tk
        pairs = [(i, i * (tq // tk) - nback, 1, 1) for i in range(s // tq)]
    sched = [jnp.asarray([p[c] for p in pairs], jnp.int32) for c in range(4)]

    def kv_spec(r):
        return pl.BlockSpec((tk, HEAD_DIM), lambda h, t, qi, kj, fi, la: (jnp.maximum(kj[t] + r, 0), h))

    in_specs = [pl.BlockSpec((tq, gw), lambda h, t, qi, kj, fi, la: (qi[t], h))]
    in_specs += [kv_spec(r) for r in range(nk)] * 2
    args = [q] + [k] * nk + [v] * nk
    if selected:
        in_specs.append(pl.BlockSpec((1, tq, sel.shape[2]), lambda h, t, qi, kj, fi, la: (h, qi[t], 0)))
        in_specs.append(pl.BlockSpec((expand.shape[0], tk), lambda h, t, qi, kj, fi, la: (0, kj[t])))
        args += [sel, expand]
    rows = NSA_GROUP * tq
    tk = nk * tk
    return pl.pallas_call(
        functools.partial(_masked_attn_kernel, tq=tq, nk=nk, sub=sub, selected=selected),
        grid_spec=pltpu.PrefetchScalarGridSpec(
            num_scalar_prefetch=4,
            grid=(hk, len(pairs)),
            in_specs=in_specs,
            out_specs=pl.BlockSpec((tq, gw), lambda h, t, qi, kj, fi, la: (qi[t], h)),
            scratch_shapes=[pltpu.VMEM((rows, HEAD_DIM), BF16),
                            pltpu.VMEM((tq, tk), F32),
                            pltpu.VMEM((rows, tk), F32),
                            pltpu.VMEM((rows, tk), BF16),
                            pltpu.VMEM((tk // sub, rows, LANES), F32),
                            pltpu.VMEM((rows, LANES), F32),
                            pltpu.VMEM((rows, 2 * HEAD_DIM), F32)]),
        out_shape=jax.ShapeDtypeStruct((s, dq), F32),
        compiler_params=_params("parallel", "arbitrary"),
        name="nsa_selected_attn" if selected else "nsa_window_attn",
    )(*sched, *args)


def _nsa_combine_kernel(oc_ref, os_ref, ow_ref, g_ref, out_ref):
    gates = jax.nn.sigmoid(g_ref[...])
    ng, dq = g_ref.shape[1], oc_ref.shape[1]
    row = lax.broadcasted_iota(jnp.int32, (ng, dq), 0)
    head3 = (lax.broadcasted_iota(jnp.int32, (ng, dq), 1) // HEAD_DIM) * N_BRANCH
    acc = None
    for b, ref in enumerate((oc_ref, os_ref, ow_ref)):
        gb = _dot_sel(gates, jnp.where(row == head3 + b, 1.0, 0.0))
        term = gb * ref[...]
        acc = term if acc is None else acc + term
    out_ref[...] = acc.astype(out_ref.dtype)


def _nsa_combine(oc, osel, ow, gates, *, ts=256):
    s, dq = oc.shape
    spec = pl.BlockSpec((ts, dq), lambda i: (i, 0))
    return pl.pallas_call(
        _nsa_combine_kernel,
        grid=(s // ts,),
        in_specs=[spec, spec, spec, pl.BlockSpec((ts, gates.shape[1]), lambda i: (i, 0))],
        out_specs=spec,
        out_shape=jax.ShapeDtypeStruct((s, dq), BF16),
        compiler_params=_params("parallel"),
        name="nsa_combine",
    )(oc, osel, ow, gates)


def _nsa_attention(x, sc, sh, cos2, sin2, w_in, layer, cmp_pe, cmp_w1, cmp_w2):
    s, d = x.shape
    dq = d
    hk = d // HEAD_DIM // NSA_GROUP
    dkv = hk * HEAD_DIM
    n_main = dq + 6 * dkv
    ngate = w_in.shape[2] - n_main
    w_main = w_in[layer, :, :n_main].astype(BF16)[None]
    proj = _mod_matmul(x, sc, sh, w_main, 0, 0, n_main, name="nsa_in_proj")
    w_gate = jnp.pad(w_in[layer, :, n_main:], ((0, 0), (0, LANES - ngate)))[None]
    gates = _mod_matmul(x, sc, sh, w_gate, 0, 0, LANES, name="nsa_in_gates")
    q, kvc, ks, vs, kw, vw = _nsa_prep(proj, cos2, sin2, dq, dkv)
    groups = kvc.reshape(2, hk, s // CMP_STRIDE, CMP_STRIDE * HEAD_DIM)
    cmp = _nsa_compress(groups, cmp_pe, cmp_w1, cmp_w2)

    nc = s // CMP_STRIDE
    n_slc = s // SLC_LEN
    nbp = -(-n_slc // LANES) * LANES
    c_start = CMP_STRIDE * jnp.arange(nc)
    s_start = SLC_LEN * jnp.arange(nbp)
    overlap = jnp.clip(jnp.minimum(c_start[:, None] + CMP_LEN, s_start[None, :] + SLC_LEN)
                       - jnp.maximum(c_start[:, None], s_start[None, :]), 0, None).astype(F32) / CMP_LEN
    o_c, sel = _nsa_cmp_select(q, cmp, overlap.astype(BF16))
    expand = (jnp.arange(nbp)[:, None] == jnp.arange(s)[None, :] // SLC_LEN).astype(BF16)
    o_s = _masked_attn(q, ks, vs, sel, expand, tq=512, tk=1024, sub=512)
    o_w = _masked_attn(q, kw, vw, None, None, tq=256, tk=256, sub=256)
    return _nsa_combine(o_c, o_s, o_w, gates)


def kernel(x, c, positions, mod_w, mod_b, ln_g, ln_b, ffn_w_gu, ffn_w_down, gdn_w_in, gdn_conv_w,
           gdn_a_log, gdn_dt_bias, gdn_norm_w, gdn_w_out, nsa_w_in, nsa_cmp_pe, nsa_cmp_w1,
           nsa_cmp_w2, nsa_w_out):
    bsz, s, d = x.shape
    assert bsz == 1
    depth = mod_w.shape[0]
    alpha = (2 * depth) ** 0.25

    inv = ROPE_THETA ** (-jnp.arange(0, HEAD_DIM, 2, dtype=F32) / HEAD_DIM)
    ang = positions[0].astype(F32)[:, None] * inv
    cos, sin = jnp.cos(ang), jnp.sin(ang)
    cos2 = jnp.concatenate([cos, cos], axis=-1)
    sin2 = jnp.concatenate([-sin, sin], axis=-1)

    mod = _modulation(c, mod_w, mod_b)
    gdn_w_out, nsa_w_out, ffn_w_down = [w.astype(BF16) for w in (gdn_w_out, nsa_w_out, ffn_w_down)]
    xs = x[0]
    for i in range(depth):
        sh1, sc1, ga1, sh2, sc2, ga2 = [mod[i, :, r * d:(r + 1) * d] for r in range(6)]
        j = i // N_MIXERS
        if i % N_MIXERS == 0:
            y = _gated_deltanet(xs, sc1, sh1, gdn_w_in, gdn_conv_w, j, gdn_a_log[j],
                                gdn_dt_bias[j], gdn_norm_w[j])
            w_out = gdn_w_out
        else:
            y = _nsa_attention(xs, sc1, sh1, cos2, sin2, nsa_w_in, j, nsa_cmp_pe[j],
                               nsa_cmp_w1[j], nsa_cmp_w2[j])
            w_out = nsa_w_out
        xs = _matmul_ln(y, w_out, j, xs, ga1, ln_g[i, 0:1], ln_b[i, 0:1], alpha=alpha)
        a = _swiglu_up(xs, sc2, sh2, ffn_w_gu, i)
        xs = _matmul_ln(a, ffn_w_down, i, xs, ga2, ln_g[i, 1:2], ln_b[i, 1:2], alpha=alpha)
    return xs[None]
```

```python
import functools

import jax
import jax.numpy as jnp
from jax import lax
from jax.experimental import pallas as pl
from jax.experimental.pallas import tpu as pltpu

F32 = jnp.float32
BF16 = jnp.bfloat16
HIGHEST = lax.Precision.HIGHEST

HEAD_DIM = 128
N_MIXERS = 2
GDN_CONV = 4
GDN_BLOCK = 128
NSA_GROUP = 4
N_BRANCH = 3
CMP_LEN = 32
CMP_STRIDE = 16
SLC_LEN = 64
SLC_TOP = 16
WINDOW = 512
ROPE_THETA = 10000.0
LN_EPS = 1e-5
NORM_EPS = 1e-6

LOG2E = 1.4426950408889634
MATMUL_ROWS = 1024
LANES = 128
VMEM_LIMIT_BYTES = 56 * 1024 * 1024
NEG_BIG = -1e30
MASK_NEG = -2.0 ** 100


def _params(*sem):
    return pltpu.CompilerParams(dimension_semantics=sem, vmem_limit_bytes=VMEM_LIMIT_BYTES)


def _silu(v):
    return v * jax.nn.sigmoid(v)


def _dot(a, b):
    return jnp.dot(a.astype(BF16), b.astype(BF16), preferred_element_type=F32)


def _dot_nt(a, b):
    return lax.dot_general(a.astype(BF16), b.astype(BF16), (((1,), (1,)), ((), ())),
                           preferred_element_type=F32)


def _split(a):
    hi = a.astype(BF16)
    lo = (a - hi.astype(F32)).astype(BF16)
    return hi, lo


def _dot3(a, b):
    ah, al = _split(a)
    bh, bl = _split(b)
    d = functools.partial(jnp.dot, preferred_element_type=F32)
    return d(ah, bh) + (d(ah, bl) + d(al, bh))


def _dot_sel(a, onehot):
    ah, al = _split(a)
    e = onehot.astype(BF16)
    d = functools.partial(jnp.dot, preferred_element_type=F32)
    return d(ah, e) + d(al, e)


def _sel_dot(onehot, b):
    bh, bl = _split(b)
    e = onehot.astype(BF16)
    d = functools.partial(jnp.dot, preferred_element_type=F32)
    return d(e, bh) + d(e, bl)


def _mod_kernel(c_ref, w_ref, b_ref, o_ref):
    cond = _silu(c_ref[...])
    o_ref[0] = _dot3(cond, w_ref[0]) + b_ref[0]


def _modulation(c, mod_w, mod_b):
    depth, d, n = mod_w.shape
    tn = 1536
    c8 = jnp.broadcast_to(c[:1], (8, d))
    out = pl.pallas_call(
        _mod_kernel,
        grid=(depth, n // tn),
        in_specs=[pl.BlockSpec((8, d), lambda l, j: (0, 0)),
                  pl.BlockSpec((1, d, tn), lambda l, j: (l, 0, j)),
                  pl.BlockSpec((1, 1, tn), lambda l, j: (l, 0, j))],
        out_specs=pl.BlockSpec((1, 8, tn), lambda l, j: (l, 0, j)),
        out_shape=jax.ShapeDtypeStruct((depth, 8, n), F32),
        compiler_params=_params("parallel", "parallel"),
        name="modulation",
    )(c8, mod_w, mod_b.reshape(depth, 1, n))
    return out[:, 0:1, :]


def _mod_matmul_kernel(x_ref, sc_ref, sh_ref, w_ref, o_ref, h_scr):
    @pl.when(pl.program_id(1) == 0)
    def _():
        h_scr[...] = (x_ref[...] * (1.0 + sc_ref[...]) + sh_ref[...]).astype(BF16)

    o_ref[...] = jnp.dot(h_scr[...], w_ref[...].astype(BF16),
                         preferred_element_type=F32).astype(o_ref.dtype)


def _mod_matmul(x, sc, sh, w, layer, col0, n, *, tm=MATMUL_ROWS, tn=512, name="mod_matmul"):
    m, k = x.shape
    tm = min(tm, m)
    tn = min(tn, n)
    cb0 = col0 // tn
    assert col0 % tn == 0 and n % tn == 0 and m % tm == 0
    return pl.pallas_call(
        _mod_matmul_kernel,
        grid=(m // tm, n // tn),
        in_specs=[pl.BlockSpec((tm, k), lambda i, j: (i, 0)),
                  pl.BlockSpec((1, k), lambda i, j: (0, 0)),
                  pl.BlockSpec((1, k), lambda i, j: (0, 0)),
                  pl.BlockSpec((None, k, tn), lambda i, j: (layer, 0, j + cb0))],
        out_specs=pl.BlockSpec((tm, tn), lambda i, j: (i, j)),
        out_shape=jax.ShapeDtypeStruct((m, n), F32),
        scratch_shapes=[pltpu.VMEM((tm, k), BF16)],
        compiler_params=_params("parallel", "arbitrary"),
        name=name,
    )(x, sc, sh, w)


def _swiglu_up_kernel(x_ref, sc_ref, sh_ref, wg_ref, wu_ref, o_ref, h_scr):
    @pl.when(pl.program_id(1) == 0)
    def _():
        h_scr[...] = (x_ref[...] * (1.0 + sc_ref[...]) + sh_ref[...]).astype(BF16)

    h = h_scr[...]
    g = jnp.dot(h, wg_ref[...].astype(BF16), preferred_element_type=F32)
    u = jnp.dot(h, wu_ref[...].astype(BF16), preferred_element_type=F32)
    o_ref[...] = (_silu(g) * u).astype(o_ref.dtype)


def _swiglu_up(x, sc, sh, w_gu, layer, *, tm=MATMUL_ROWS, tn=512):
    m, k = x.shape
    tm = min(tm, m)
    dff = w_gu.shape[2] // 2
    nb = dff // tn
    assert dff % tn == 0 and m % tm == 0
    return pl.pallas_call(
        _swiglu_up_kernel,
        grid=(m // tm, nb),
        in_specs=[pl.BlockSpec((tm, k), lambda i, j: (i, 0)),
                  pl.BlockSpec((1, k), lambda i, j: (0, 0)),
                  pl.BlockSpec((1, k), lambda i, j: (0, 0)),
                  pl.BlockSpec((None, k, tn), lambda i, j: (layer, 0, j)),
                  pl.BlockSpec((None, k, tn), lambda i, j: (layer, 0, j + nb))],
        out_specs=pl.BlockSpec((tm, tn), lambda i, j: (i, j)),
        out_shape=jax.ShapeDtypeStruct((m, dff), BF16),
        scratch_shapes=[pltpu.VMEM((tm, k), BF16)],
        compiler_params=_params("parallel", "arbitrary"),
        name="swiglu_up",
    )(x, sc, sh, w_gu, w_gu)


LN_ROWS = 256
LN_MAX_TK = 1536


def _matmul_ln_kernel(a_ref, w_ref, x_ref, ga_ref, g_ref, b_ref, o_ref, *, alpha):
    kk = pl.program_id(1)
    part = jnp.dot(a_ref[...].astype(BF16), w_ref[...].astype(BF16), preferred_element_type=F32)

    @pl.when(kk == 0)
    def _():
        o_ref[...] = part

    @pl.when(kk > 0)
    def _():
        o_ref[...] += part

    @pl.when(kk == pl.num_programs(1) - 1)
    def _():
        for r0 in range(0, o_ref.shape[0], LN_ROWS):
            rows = slice(r0, r0 + LN_ROWS)
            r = alpha * x_ref[rows, :] + (1.0 + ga_ref[...]) * o_ref[rows, :]
            mu = jnp.mean(r, axis=-1, keepdims=True)
            d = r - mu
            var = jnp.mean(d * d, axis=-1, keepdims=True)
            o_ref[rows, :] = d * lax.rsqrt(var + LN_EPS) * g_ref[...] + b_ref[...]


def _matmul_ln(a, w, layer, x, ga, g, b, *, alpha, tm=512):
    m, k = a.shape
    n = w.shape[2]
    tm = min(tm, m)
    tk = next(t for t in range(min(k, LN_MAX_TK), 0, -LANES) if k % t == 0)
    assert m % tm == 0 and k % tk == 0 and tm % LN_ROWS == 0
    return pl.pallas_call(
        functools.partial(_matmul_ln_kernel, alpha=alpha),
        grid=(m // tm, k // tk),
        in_specs=[pl.BlockSpec((tm, tk), lambda i, kk: (i, kk)),
                  pl.BlockSpec((None, tk, n), lambda i, kk: (layer, kk, 0)),
                  pl.BlockSpec((tm, n), lambda i, kk: (i, 0)),
                  pl.BlockSpec((1, n), lambda i, kk: (0, 0)),
                  pl.BlockSpec((1, n), lambda i, kk: (0, 0)),
                  pl.BlockSpec((1, n), lambda i, kk: (0, 0))],
        out_specs=pl.BlockSpec((tm, n), lambda i, kk: (i, 0)),
        out_shape=jax.ShapeDtypeStruct((m, n), F32),
        compiler_params=_params("parallel", "arbitrary"),
        name="matmul_ln",
    )(a, w, x, ga, g, b)


def _gdn_conv_kernel(xp_ref, xc_ref, w_ref, o_ref, buf, *, normalize, scale):
    ts, tc = xc_ref.shape
    buf[0:8, :] = jnp.where(pl.program_id(0) > 0, xp_ref[...], 0.0)
    buf[8:, :] = xc_ref[...]
    w = w_ref[...]
    y = w[0:1, :] * buf[5:5 + ts, :]
    for i in range(1, GDN_CONV):
        y = y + w[i:i + 1, :] * buf[5 + i:5 + i + ts, :]
    y = _silu(y)
    if normalize:
        for hh in range(tc // HEAD_DIM):
            seg = y[:, hh * HEAD_DIM:(hh + 1) * HEAD_DIM]
            ss = jnp.sum(seg * seg, axis=-1, keepdims=True)
            o_ref[:, hh * HEAD_DIM:(hh + 1) * HEAD_DIM] = seg * (lax.rsqrt(ss + NORM_EPS) * scale)
    else:
        o_ref[...] = y


def _gdn_conv(proj, conv_w, layer, col0, ncols, *, normalize, scale, ts=512, tc=512):
    s = proj.shape[0]
    cb0 = col0 // tc
    hb = ts // 8
    return pl.pallas_call(
        functools.partial(_gdn_conv_kernel, normalize=normalize, scale=scale),
        grid=(s // ts, ncols // tc),
        in_specs=[pl.BlockSpec((8, tc), lambda i, j: (jnp.maximum(i * hb - 1, 0), j + cb0)),
                  pl.BlockSpec((ts, tc), lambda i, j: (i, j + cb0)),
                  pl.BlockSpec((None, GDN_CONV, tc), lambda i, j: (layer, 0, j + cb0))],
        out_specs=pl.BlockSpec((ts, tc), lambda i, j: (i, j)),
        out_shape=jax.ShapeDtypeStruct((s, ncols), F32),
        scratch_shapes=[pltpu.VMEM((ts + 8, tc), F32)],
        compiler_params=_params("parallel", "parallel"),
        name="gdn_conv",
    )(proj, proj, conv_w)


def _gdn_gates_kernel(ba_ref, alog_ref, dtb_ref, out_ref):
    ts = ba_ref.shape[0]
    nh = ba_ref.shape[1] // 2
    x = ba_ref[...]
    beta = jax.nn.sigmoid(x[:, :nh])
    z = x[:, nh:] + dtb_ref[...]
    softplus = jnp.maximum(z, 0.0) + jnp.log1p(jnp.exp(-jnp.abs(z)))
    g = -jnp.exp(alog_ref[...]) * softplus
    ri = lax.broadcasted_iota(jnp.int32, (ts, ts), 0)
    ci = lax.broadcasted_iota(jnp.int32, (ts, ts), 1)
    same = (ri // GDN_BLOCK) == (ci // GDN_BLOCK)
    gc = _sel_dot(jnp.where(same & (ci <= ri), 1.0, 0.0), g)
    gl = _sel_dot(jnp.where(same, 1.0, 0.0), g)
    out_ref[...] = jnp.concatenate([beta, gc, gl, jnp.zeros((ts, LANES - 3 * nh), F32)], axis=1)


def _gdn_gates(ba, a_log, dt_bias, *, ts=512):
    s, two_h = ba.shape
    nh = two_h // 2
    assert 3 * nh <= LANES
    return pl.pallas_call(
        _gdn_gates_kernel,
        grid=(s // ts,),
        in_specs=[pl.BlockSpec((ts, two_h), lambda i: (i, 0)),
                  pl.BlockSpec((1, nh), lambda i: (0, 0)),
                  pl.BlockSpec((1, nh), lambda i: (0, 0))],
        out_specs=pl.BlockSpec((ts, LANES), lambda i: (i, 0)),
        out_shape=jax.ShapeDtypeStruct((s, LANES), F32),
        compiler_params=_params("parallel"),
        name="gdn_gates",
    )(ba, a_log.reshape(1, nh), dt_bias.reshape(1, nh))


def _bdot(a, b):
    return jnp.einsum("hij,hjk->hik", a.astype(BF16), b.astype(BF16), preferred_element_type=F32)


def _bdot3(a, b):
    ah, al = _split(a)
    bh, bl = _split(b)
    return jnp.einsum("hij,hjk->hik", jnp.concatenate([ah, ah, al], axis=2),
                      jnp.concatenate([bh, bl, bh], axis=1), preferred_element_type=F32)


def _unit_lower_inverse(low, ri, ci):
    n = low.shape[-1]

    def blk(b):
        return ((ri // b) == (ci // b))[None]

    eye = jnp.where(ri == ci, 1.0, 0.0)[None]
    n1 = jnp.where(blk(16), -low, 0.0)
    n2 = _bdot(n1, n1)
    n4 = _bdot(n2, n2)
    n8 = _bdot(n4, n4)
    t = eye + n1
    t = t + _bdot(n2, t)
    t = t + _bdot(n4, t)
    t = t + _bdot(n8, t)
    b = 32
    while b <= n:
        off = jnp.where(blk(b) & jnp.logical_not(blk(b // 2)), low, 0.0)
        t = t - _bdot(t, _bdot(off, t))
        b *= 2
    return t


def _gdn_chunk_kernel(q_ref, k_ref, v_ref, gates_ref, gct_ref, o_ref, state, *, heads):
    hg = pl.program_id(0)
    rows = q_ref.shape[0]
    nh = gct_ref.shape[0]
    dh = HEAD_DIM
    rep = rows // dh

    @pl.when(pl.program_id(1) == 0)
    def _():
        state[...] = jnp.zeros_like(state)

    ri = lax.broadcasted_iota(jnp.int32, (rows, rows), 0)
    ci = lax.broadcasted_iota(jnp.int32, (rows, rows), 1)
    lower = (ci <= ri)[None]
    strict = (ci < ri)[None]
    dot = functools.partial(jnp.dot, preferred_element_type=F32)

    gates_hi, gates_lo = _split(gates_ref[...])
    width = heads * 3 * dh
    prow = lax.broadcasted_iota(jnp.int32, (LANES, width), 0)
    pcol = lax.broadcasted_iota(jnp.int32, (LANES, width), 1)
    pick = jnp.where(prow == ((pcol // dh) % 3) * nh + hg * heads + pcol // (3 * dh), 1.0, 0.0).astype(BF16)
    cols = dot(jnp.concatenate([gates_hi, gates_lo], axis=1),
               jnp.concatenate([pick, pick], axis=0))

    def per_head(j):
        return jnp.stack([cols[:, (3 * hl + j) * dh:(3 * hl + j + 1) * dh] for hl in range(heads)])

    beta, gcol, glast = per_head(0), per_head(1), per_head(2)
    gct_hi, gct_lo = _split(gct_ref[...])
    trow = lax.broadcasted_iota(jnp.int32, (heads * rows, nh), 0) // rows
    tcol = lax.broadcasted_iota(jnp.int32, (heads * rows, nh), 1)
    pick_t = jnp.where(tcol == hg * heads + trow, 1.0, 0.0).astype(BF16)
    g_j = dot(jnp.concatenate([pick_t, pick_t], axis=1),
              jnp.concatenate([gct_hi, gct_lo], axis=0)).reshape(heads, rows, rows)
    g_i = jnp.concatenate([gcol] * rep, axis=2)
    beta_i = jnp.concatenate([beta] * rep, axis=2)
    decay = jnp.where(lower, jnp.exp(jnp.where(lower, g_i - g_j, 0.0)), 0.0)

    def pairs(ref):
        return [ref[:, p * dh:(p + 1) * dh] for p in range(heads // 2)]

    def nt(a, b):
        return jnp.einsum("pid,pjd->pij", a.astype(BF16), b.astype(BF16), preferred_element_type=F32)

    def both(x):
        return jnp.stack([x[hl // 2] for hl in range(heads)])

    qp, kp = jnp.stack(pairs(q_ref)), jnp.stack(pairs(k_ref))
    q, k = both(qp), both(kp)
    kk, qk = both(nt(kp, kp)), both(nt(qp, kp))
    v = jnp.stack([v_ref[:, hl * dh:(hl + 1) * dh] for hl in range(heads)])

    low = jnp.where(strict, kk * beta_i * decay, 0.0)
    t0 = _unit_lower_inverse(low, ri, ci)
    rhs = jnp.concatenate([v * beta, k * (beta * jnp.exp(gcol))], axis=2)
    x = _bdot(t0, rhs)
    x = x + _bdot(t0, rhs - x - _bdot3(low, x))
    u, w = x[:, :, :dh], x[:, :, dh:]
    a_x = _bdot(jnp.where(lower, qk * decay, 0.0), x)
    q_eff = q * jnp.exp(gcol) - a_x[:, :, dh:]
    k_g = k * jnp.exp(glast - gcol)

    s = state[...]
    o = _bdot(q_eff, s) + a_x[:, :, :dh]
    v_new = u - _bdot(w, s)
    state[...] = s * jnp.exp(glast[:, 0:1, :]) + jnp.einsum(
        "hck,hcv->hkv", k_g.astype(BF16), v_new.astype(BF16), preferred_element_type=F32)
    for hl in range(heads):
        o_ref[:, hl * dh:(hl + 1) * dh] = o[hl]


def _gdn_chunk(q, k, v, gates, gct, *, heads=16):
    rows = GDN_BLOCK
    s, dv_total = v.shape
    nh = dv_total // HEAD_DIM
    assert nh == 2 * (q.shape[1] // HEAD_DIM) and heads % 2 == 0 and nh % heads == 0
    qk_spec = pl.BlockSpec((rows, heads // 2 * HEAD_DIM), lambda h, i: (i, h))
    v_spec = pl.BlockSpec((rows, heads * HEAD_DIM), lambda h, i: (i, h))
    return pl.pallas_call(
        functools.partial(_gdn_chunk_kernel, heads=heads),
        grid=(nh // heads, s // rows),
        in_specs=[qk_spec, qk_spec, v_spec,
                  pl.BlockSpec((rows, LANES), lambda h, i: (i, 0)),
                  pl.BlockSpec((nh, rows), lambda h, i: (0, i))],
        out_specs=v_spec,
        out_shape=jax.ShapeDtypeStruct((s, dv_total), F32),
        scratch_shapes=[pltpu.VMEM((heads, HEAD_DIM, HEAD_DIM), F32)],
        compiler_params=_params("parallel", "arbitrary"),
        name="gdn_chunk",
    )(q, k, v, gates, gct)


def _gdn_post_kernel(o_ref, z_ref, nw_ref, out_ref):
    nw = nw_ref[...]
    for hh in range(o_ref.shape[1] // HEAD_DIM):
        sl = slice(hh * HEAD_DIM, (hh + 1) * HEAD_DIM)
        o = o_ref[:, sl]
        ms = jnp.mean(o * o, axis=-1, keepdims=True)
        out_ref[:, sl] = ((o * lax.rsqrt(ms + NORM_EPS)) * nw * _silu(z_ref[:, sl])).astype(out_ref.dtype)


def _gdn_post(o, proj, z_col0, norm_w, *, ts=256):
    s, dv = o.shape
    zb = z_col0 // dv
    return pl.pallas_call(
        _gdn_post_kernel,
        grid=(s // ts,),
        in_specs=[pl.BlockSpec((ts, dv), lambda i: (i, 0)),
                  pl.BlockSpec((ts, dv), lambda i: (i, zb)),
                  pl.BlockSpec((1, HEAD_DIM), lambda i: (0, 0))],
        out_specs=pl.BlockSpec((ts, dv), lambda i: (i, 0)),
        out_shape=jax.ShapeDtypeStruct((s, dv), BF16),
        compiler_params=_params("parallel"),
        name="gdn_post",
    )(o, proj, norm_w.reshape(1, HEAD_DIM))


def _gated_deltanet(x, sc, sh, w_in, conv_w, layer, a_log, dt_bias, norm_w):
    nvh = a_log.shape[0]
    dv = nvh * HEAD_DIM
    conv_ch = conv_w.shape[2]
    dqk = (conv_ch - dv) // 2
    n_main = conv_ch + dv
    w_main = w_in[layer, :, :n_main][None]
    proj = _mod_matmul(x, sc, sh, w_main, 0, 0, n_main, name="gdn_in_proj")
    ba = _mod_matmul(x, sc, sh, w_in[layer, :, n_main:][None], 0, 0, 2 * nvh, name="gdn_in_gates")
    q = _gdn_conv(proj, conv_w, layer, 0, dqk, normalize=True, scale=HEAD_DIM ** -0.5)
    k = _gdn_conv(proj, conv_w, layer, dqk, dqk, normalize=True, scale=1.0)
    v = _gdn_conv(proj, conv_w, layer, 2 * dqk, dv, normalize=False, scale=1.0)
    gates = _gdn_gates(ba, a_log, dt_bias)
    o = _gdn_chunk(q, k, v, gates, gates[:, nvh:2 * nvh].T)
    return _gdn_post(o, proj, conv_ch, norm_w)


def _rope(x, c, sg):
    return x * c + pltpu.roll(x, HEAD_DIM // 2, 1) * sg


def _nsa_prep_kernel(p_ref, c_ref, s_ref, q_ref, kvc_ref, ks_ref, vs_ref, kw_ref, vw_ref, *, dq, dkv):
    c = c_ref[...]
    sg = s_ref[...]
    dh = HEAD_DIM

    def head(col0, hh, rope, mul):
        x = p_ref[:, col0 + hh * dh:col0 + (hh + 1) * dh]
        if rope:
            x = _rope(x, c, sg)
        return x if mul == 1.0 else x * mul

    for hh in range(dq // dh):
        q_ref[:, hh * dh:(hh + 1) * dh] = head(0, hh, True, dh ** -0.5 * LOG2E).astype(q_ref.dtype)
    for hh in range(dkv // dh):
        sl = slice(hh * dh, (hh + 1) * dh)
        kvc_ref[0, hh] = head(dq, hh, True, 1.0)
        kvc_ref[1, hh] = head(dq + dkv, hh, False, 1.0)
        ks_ref[:, sl] = head(dq + 2 * dkv, hh, True, 1.0).astype(ks_ref.dtype)
        vs_ref[:, sl] = head(dq + 3 * dkv, hh, False, 1.0).astype(vs_ref.dtype)
        kw_ref[:, sl] = head(dq + 4 * dkv, hh, True, 1.0).astype(kw_ref.dtype)
        vw_ref[:, sl] = head(dq + 5 * dkv, hh, False, 1.0).astype(vw_ref.dtype)


def _nsa_prep(proj, cos2, sin2, dq, dkv, *, ts=256):
    s, n = proj.shape
    hk = dkv // HEAD_DIM
    kv_spec = pl.BlockSpec((ts, dkv), lambda i: (i, 0))
    tab = pl.BlockSpec((ts, HEAD_DIM), lambda i: (i, 0))
    kv16 = jax.ShapeDtypeStruct((s, dkv), BF16)
    return pl.pallas_call(
        functools.partial(_nsa_prep_kernel, dq=dq, dkv=dkv),
        grid=(s // ts,),
        in_specs=[pl.BlockSpec((ts, n), lambda i: (i, 0)), tab, tab],
        out_specs=[pl.BlockSpec((ts, dq), lambda i: (i, 0)),
                   pl.BlockSpec((2, hk, ts, HEAD_DIM), lambda i: (0, 0, i, 0)),
                   kv_spec, kv_spec, kv_spec, kv_spec],
        out_shape=[jax.ShapeDtypeStruct((s, dq), BF16),
                   jax.ShapeDtypeStruct((2, hk, s, HEAD_DIM), F32),
                   kv16, kv16, kv16, kv16],
        compiler_params=_params("parallel"),
        name="nsa_prep",
    )(proj, cos2, sin2)


def _nsa_compress_kernel(x_ref, pe_ref, w1_ref, w2_ref, o_ref):
    x = x_ref[0, 0]
    n = x.shape[0]
    a = _dot(x + pe_ref[0, 0:1, :], w1_ref[0, 0])
    b = _dot(x + pe_ref[0, 1:2, :], w1_ref[0, 1])
    y = a + pltpu.roll(b, n - 1, 0)
    o_ref[0, 0] = _dot(_silu(y), w2_ref[0])


def _nsa_compress(x, pe, w1, w2):
    _, hk, ng, gw = x.shape
    half = CMP_LEN // 2
    pe2 = pe.reshape(2, 2, half * HEAD_DIM)
    w1r = w1.reshape(2, 2, half * HEAD_DIM, HEAD_DIM)
    return pl.pallas_call(
        _nsa_compress_kernel,
        grid=(2, hk),
        in_specs=[pl.BlockSpec((1, 1, ng, gw), lambda t, h: (t, h, 0, 0)),
                  pl.BlockSpec((1, 2, gw), lambda t, h: (t, 0, 0)),
                  pl.BlockSpec((1, 2, gw, HEAD_DIM), lambda t, h: (t, 0, 0, 0)),
                  pl.BlockSpec((1, HEAD_DIM, HEAD_DIM), lambda t, h: (t, 0, 0))],
        out_specs=pl.BlockSpec((1, 1, ng, HEAD_DIM), lambda t, h: (t, h, 0, 0)),
        out_shape=jax.ShapeDtypeStruct((2, hk, ng, HEAD_DIM), F32),
        compiler_params=_params("parallel", "parallel"),
        name="nsa_compress",
    )(x, pe2, w1r, w2)


def _nsa_cmp_select_kernel(q_ref, kc_ref, vc_ref, ov_ref, o_ref, sel_ref):
    tq = q_ref.shape[0]
    nc = kc_ref.shape[2]
    nbp = ov_ref.shape[1]
    t0 = pl.program_id(0) * tq
    tpos = t0 + lax.broadcasted_iota(jnp.int32, (tq, nc), 0)
    nidx = lax.broadcasted_iota(jnp.int32, (tq, nc), 1)
    valid = (nidx * CMP_STRIDE + (CMP_LEN - 1)) <= tpos
    kc = kc_ref[0, 0].astype(BF16)
    vc = vc_ref[0, 0].astype(BF16)
    psum = jnp.zeros((tq, nc), F32)
    q4 = jnp.concatenate([q_ref[:, g * HEAD_DIM:(g + 1) * HEAD_DIM] for g in range(NSA_GROUP)], axis=0)
    s_all = _dot_nt(q4, kc)
    probs = []
    for g in range(NSA_GROUP):
        s = jnp.where(valid, s_all[g * tq:(g + 1) * tq, :], NEG_BIG)
        m = jnp.max(s, axis=-1, keepdims=True)
        p = jnp.where(valid, jnp.exp2(s - m), 0.0)
        den = jnp.sum(p, axis=-1, keepdims=True)
        p = p / jnp.where(den > 0.0, den, 1.0)
        probs.append(p.astype(BF16))
        psum = psum + p
    o_all = jnp.dot(jnp.concatenate(probs, axis=0), vc, preferred_element_type=F32)
    for g in range(NSA_GROUP):
        o_ref[:, g * HEAD_DIM:(g + 1) * HEAD_DIM] = o_all[g * tq:(g + 1) * tq, :]
    imp = _dot_sel(psum, ov_ref[...])
    imp_t = imp.T
    j = lax.broadcasted_iota(jnp.int32, (nbp, tq), 0)
    cur = (t0 + lax.broadcasted_iota(jnp.int32, (nbp, tq), 1)) // SLC_LEN
    forced = (j == 0) | (j == cur) | (j == cur - 1)
    work = jnp.where(j <= cur, jnp.where(forced, jnp.inf, imp_t), -jnp.inf)
    jf = j.astype(F32)
    for _ in range(SLC_TOP):
        m = jnp.max(work, axis=0, keepdims=True)
        first = jnp.min(jnp.where(work == m, jf, float(nbp)), axis=0, keepdims=True)
        work = jnp.where(jf == first, -jnp.inf, work)
    sel_ref[0] = jnp.where((j <= cur) & (work == -jnp.inf), 0.0, MASK_NEG).T.astype(sel_ref.dtype)


def _nsa_cmp_select(q, cmp, ov, *, tq=512):
    s, dq = q.shape
    _, hk, nc, _ = cmp.shape
    nbp = ov.shape[1]
    gw = NSA_GROUP * HEAD_DIM
    return pl.pallas_call(
        _nsa_cmp_select_kernel,
        grid=(s // tq, hk),
        in_specs=[pl.BlockSpec((tq, gw), lambda i, h: (i, h)),
                  pl.BlockSpec((1, 1, nc, HEAD_DIM), lambda i, h: (0, h, 0, 0)),
                  pl.BlockSpec((1, 1, nc, HEAD_DIM), lambda i, h: (1, h, 0, 0)),
                  pl.BlockSpec((nc, nbp), lambda i, h: (0, 0))],
        out_specs=[pl.BlockSpec((tq, gw), lambda i, h: (i, h)),
                   pl.BlockSpec((1, tq, nbp), lambda i, h: (h, i, 0))],
        out_shape=[jax.ShapeDtypeStruct((s, dq), F32),
                   jax.ShapeDtypeStruct((hk, s, nbp), BF16)],
        compiler_params=_params("parallel", "parallel"),
        name="nsa_cmp_select",
    )(q, cmp, cmp, ov)


ATTN_ROWS = 64


def _masked_attn_kernel(*refs, tq, nk, sub, selected):
    qi_ref, kj_ref, first_ref, last_ref, q_ref = refs[:5]
    k_refs, v_refs = refs[5:5 + nk], refs[5 + nk:5 + 2 * nk]
    rest = refs[5 + 2 * nk:]
    if selected:
        sel_ref, expand_ref = rest[:2]
        rest = rest[2:]
    o_ref, q4, bias, s_scr, p_scr, a_scr, m_scr, acc = rest
    t = pl.program_id(1)
    qi = qi_ref[t]
    kj = kj_ref[t]
    dh = HEAD_DIM
    tk = nk * k_refs[0].shape[0]

    @pl.when(first_ref[t] == 1)
    def _():
        for g in range(NSA_GROUP):
            q4[g * tq:(g + 1) * tq, :] = q_ref[:, g * dh:(g + 1) * dh]
        m_scr[...] = jnp.full_like(m_scr, NEG_BIG)
        acc[...] = jnp.zeros_like(acc)

    k0 = kj * k_refs[0].shape[0]

    def causal():
        tpos = qi * tq + lax.broadcasted_iota(jnp.int32, (tq, tk), 0)
        kpos = k0 + lax.broadcasted_iota(jnp.int32, (tq, tk), 1)
        return tpos, kpos

    if selected:
        bias[...] = jnp.dot(sel_ref[0], expand_ref[...], preferred_element_type=F32).astype(BF16)

        @pl.when(k0 + tk - 1 > qi * tq)
        def _():
            tpos, kpos = causal()
            bias[...] = bias[...] + jnp.where(kpos <= tpos, 0.0, MASK_NEG).astype(BF16)
    else:
        tpos, kpos = causal()
        bias[...] = jnp.where((kpos <= tpos) & (kpos > tpos - WINDOW) & (kpos >= 0),
                              0.0, MASK_NEG).astype(BF16)

    tiles = [(r, u * sub) for r in range(nk) for u in range(k_refs[0].shape[0] // sub)]
    for n, (r, off) in enumerate(tiles):
        s_scr[:, n * sub:(n + 1) * sub] = lax.dot_general(
            q4[...], k_refs[r][off:off + sub, :], (((1,), (1,)), ((), ())),
            preferred_element_type=F32).astype(BF16)
    rb = ATTN_ROWS
    for u, (r, off) in enumerate(tiles):
        cols = slice(u * sub, (u + 1) * sub)
        for r0 in range(0, NSA_GROUP * tq, rb):
            rows = slice(r0, r0 + rb)
            brow = r0 % tq
            s = s_scr[rows, cols] + bias[brow:brow + rb, cols]
            m_prev = m_scr[rows, :]
            m_new = jnp.maximum(m_prev, jnp.max(s, axis=-1, keepdims=True).astype(F32))
            a_scr[u, rows, :] = jnp.exp2(m_prev - m_new)
            shift = jnp.concatenate([m_new.astype(BF16)] * (sub // LANES), axis=1)
            p_scr[rows, cols] = jnp.exp2(s - shift)
            m_scr[rows, :] = m_new
        v_one = jnp.concatenate([v_refs[r][off:off + sub, :], jnp.ones((sub, dh), BF16)], axis=1)
        alpha = jnp.concatenate([a_scr[u]] * 2, axis=1)
        acc[...] = alpha * acc[...] + jnp.dot(p_scr[:, cols], v_one, preferred_element_type=F32)

    @pl.when(last_ref[t] == 1)
    def _():
        for g in range(NSA_GROUP):
            rows = slice(g * tq, (g + 1) * tq)
            o_ref[:, g * dh:(g + 1) * dh] = acc[rows, :dh] / acc[rows, dh:]


def _masked_attn(q, k, v, sel, expand, *, tq, tk, sub):
    s, dq = q.shape
    hk = k.shape[1] // HEAD_DIM
    gw = NSA_GROUP * HEAD_DIM
    selected = sel is not None
    if selected:
        nk = 1
        pairs = []
        for i in range(s // tq):
            hi = (i * tq + tq - 1) // tk
            pairs += [(i, j, int(j == 0), int(j == hi)) for j in range(hi + 1)]
    else:
        assert tq % tk == 0
        nback = -(-(WINDOW - 1) // tk)
        nk = nback + tq // tk
        pairs = [(i, i * (tq // tk) - nback, 1, 1) for i in range(s // tq)]
    sched = [jnp.asarray([p[c] for p in pairs], jnp.int32) for c in range(4)]

    def kv_spec(r):
        return pl.BlockSpec((tk, HEAD_DIM), lambda h, t, qi, kj, fi, la: (jnp.maximum(kj[t] + r, 0), h))

    in_specs = [pl.BlockSpec((tq, gw), lambda h, t, qi, kj, fi, la: (qi[t], h))]
    in_specs += [kv_spec(r) for r in range(nk)] * 2
    args = [q] + [k] * nk + [v] * nk
    if selected:
        in_specs.append(pl.BlockSpec((1, tq, sel.shape[2]), lambda h, t, qi, kj, fi, la: (h, qi[t], 0)))
        in_specs.append(pl.BlockSpec((expand.shape[0], tk), lambda h, t, qi, kj, fi, la: (0, kj[t])))
        args += [sel, expand]
    rows = NSA_GROUP * tq
    tk = nk * tk
    return pl.pallas_call(
        functools.partial(_masked_attn_kernel, tq=tq, nk=nk, sub=sub, selected=selected),
        grid_spec=pltpu.PrefetchScalarGridSpec(
            num_scalar_prefetch=4,
            grid=(hk, len(pairs)),
            in_specs=in_specs,
            out_specs=pl.BlockSpec((tq, gw), lambda h, t, qi, kj, fi, la: (qi[t], h)),
            scratch_shapes=[pltpu.VMEM((rows, HEAD_DIM), BF16),
                            pltpu.VMEM((tq, tk), BF16),
                            pltpu.VMEM((rows, tk), BF16),
                            pltpu.VMEM((rows, tk), BF16),
                            pltpu.VMEM((tk // sub, rows, LANES), F32),
                            pltpu.VMEM((rows, LANES), F32),
                            pltpu.VMEM((rows, 2 * HEAD_DIM), F32)]),
        out_shape=jax.ShapeDtypeStruct((s, dq), F32),
        compiler_params=_params("parallel", "arbitrary"),
        name="nsa_selected_attn" if selected else "nsa_window_attn",
    )(*sched, *args)


def _nsa_combine_kernel(oc_ref, os_ref, ow_ref, g_ref, out_ref):
    gates = jax.nn.sigmoid(g_ref[...])
    ng, dq = g_ref.shape[1], oc_ref.shape[1]
    row = lax.broadcasted_iota(jnp.int32, (ng, dq), 0)
    head3 = (lax.broadcasted_iota(jnp.int32, (ng, dq), 1) // HEAD_DIM) * N_BRANCH
    acc = None
    for b, ref in enumerate((oc_ref, os_ref, ow_ref)):
        gb = _dot_sel(gates, jnp.where(row == head3 + b, 1.0, 0.0))
        term = gb * ref[...]
        acc = term if acc is None else acc + term
    out_ref[...] = acc.astype(out_ref.dtype)


def _nsa_combine(oc, osel, ow, gates, *, ts=256):
    s, dq = oc.shape
    spec = pl.BlockSpec((ts, dq), lambda i: (i, 0))
    return pl.pallas_call(
        _nsa_combine_kernel,
        grid=(s // ts,),
        in_specs=[spec, spec, spec, pl.BlockSpec((ts, gates.shape[1]), lambda i: (i, 0))],
        out_specs=spec,
        out_shape=jax.ShapeDtypeStruct((s, dq), BF16),
        compiler_params=_params("parallel"),
        name="nsa_combine",
    )(oc, osel, ow, gates)


def _nsa_attention(x, sc, sh, cos2, sin2, w_in, layer, cmp_pe, cmp_w1, cmp_w2):
    s, d = x.shape
    dq = d
    hk = d // HEAD_DIM // NSA_GROUP
    dkv = hk * HEAD_DIM
    n_main = dq + 6 * dkv
    ngate = w_in.shape[2] - n_main
    w_main = w_in[layer, :, :n_main][None]
    proj = _mod_matmul(x, sc, sh, w_main, 0, 0, n_main, name="nsa_in_proj")
    w_gate = jnp.pad(w_in[layer, :, n_main:], ((0, 0), (0, LANES - ngate)))[None]
    gates = _mod_matmul(x, sc, sh, w_gate, 0, 0, LANES, name="nsa_in_gates")
    q, kvc, ks, vs, kw, vw = _nsa_prep(proj, cos2, sin2, dq, dkv)
    groups = kvc.reshape(2, hk, s // CMP_STRIDE, CMP_STRIDE * HEAD_DIM)
    cmp = _nsa_compress(groups, cmp_pe, cmp_w1, cmp_w2)

    nc = s // CMP_STRIDE
    n_slc = s // SLC_LEN
    nbp = -(-n_slc // LANES) * LANES
    c_start = CMP_STRIDE * jnp.arange(nc)
    s_start = SLC_LEN * jnp.arange(nbp)
    overlap = jnp.clip(jnp.minimum(c_start[:, None] + CMP_LEN, s_start[None, :] + SLC_LEN)
                       - jnp.maximum(c_start[:, None], s_start[None, :]), 0, None).astype(F32) / CMP_LEN
    o_c, sel = _nsa_cmp_select(q, cmp, overlap.astype(BF16))
    expand = (jnp.arange(nbp)[:, None] == jnp.arange(s)[None, :] // SLC_LEN).astype(BF16)
    o_s = _masked_attn(q, ks, vs, sel, expand, tq=512, tk=1024, sub=512)
    o_w = _masked_attn(q, kw, vw, None, None, tq=256, tk=256, sub=256)
    return _nsa_combine(o_c, o_s, o_w, gates)


def kernel(x, c, positions, mod_w, mod_b, ln_g, ln_b, ffn_w_gu, ffn_w_down, gdn_w_in, gdn_conv_w,
           gdn_a_log, gdn_dt_bias, gdn_norm_w, gdn_w_out, nsa_w_in, nsa_cmp_pe, nsa_cmp_w1,
           nsa_cmp_w2, nsa_w_out):
    bsz, s, d = x.shape
    assert bsz == 1
    depth = mod_w.shape[0]
    alpha = (2 * depth) ** 0.25

    inv = ROPE_THETA ** (-jnp.arange(0, HEAD_DIM, 2, dtype=F32) / HEAD_DIM)
    ang = positions[0].astype(F32)[:, None] * inv
    cos, sin = jnp.cos(ang), jnp.sin(ang)
    cos2 = jnp.concatenate([cos, cos], axis=-1)
    sin2 = jnp.concatenate([-sin, sin], axis=-1)

    mod = _modulation(c, mod_w, mod_b)
    gdn_w_out, nsa_w_out, ffn_w_down = [w.astype(BF16) for w in (gdn_w_out, nsa_w_out, ffn_w_down)]
    xs = x[0]
    for i in range(depth):
        sh1, sc1, ga1, sh2, sc2, ga2 = [mod[i, :, r * d:(r + 1) * d] for r in range(6)]
        j = i // N_MIXERS
        if i % N_MIXERS == 0:
            y = _gated_deltanet(xs, sc1, sh1, gdn_w_in, gdn_conv_w, j, gdn_a_log[j],
                                gdn_dt_bias[j], gdn_norm_w[j])
            w_out = gdn_w_out
        else:
            y = _nsa_attention(xs, sc1, sh1, cos2, sin2, nsa_w_in, j, nsa_cmp_pe[j],
                               nsa_cmp_w1[j], nsa_cmp_w2[j])
            w_out = nsa_w_out
        xs = _matmul_ln(y, w_out, j, xs, ga1, ln_g[i, 0:1], ln_b[i, 0:1], alpha=alpha)
        a = _swiglu_up(xs, sc2, sh2, ffn_w_gu, i)
        xs = _matmul_ln(a, ffn_w_down, i, xs, ga2, ln_g[i, 1:2], ln_b[i, 1:2], alpha=alpha)
    return xs[None]
```

```python
import functools

import jax
import jax.numpy as jnp
from jax import lax
from jax.experimental import pallas as pl
from jax.experimental.pallas import tpu as pltpu

F32 = jnp.float32
BF16 = jnp.bfloat16
HIGHEST = lax.Precision.HIGHEST

HEAD_DIM = 128
N_MIXERS = 2
GDN_CONV = 4
GDN_BLOCK = 128
NSA_GROUP = 4
N_BRANCH = 3
CMP_LEN = 32
CMP_STRIDE = 16
SLC_LEN = 64
SLC_TOP = 16
WINDOW = 512
ROPE_THETA = 10000.0
LN_EPS = 1e-5
NORM_EPS = 1e-6

LOG2E = 1.4426950408889634
MATMUL_ROWS = 1024
LANES = 128
CONV_HALO_ROWS = 16
VMEM_LIMIT_BYTES = 56 * 1024 * 1024
NEG_BIG = -1e30
MASK_NEG = -2.0 ** 100


def _params(*sem):
    return pltpu.CompilerParams(dimension_semantics=sem, vmem_limit_bytes=VMEM_LIMIT_BYTES)


def _silu(v):
    return v * jax.nn.sigmoid(v)


def _dot(a, b):
    return jnp.dot(a.astype(BF16), b.astype(BF16), preferred_element_type=F32)


def _dot_nt(a, b):
    return lax.dot_general(a.astype(BF16), b.astype(BF16), (((1,), (1,)), ((), ())),
                           preferred_element_type=F32)


def _split(a):
    hi = a.astype(BF16)
    lo = (a - hi.astype(F32)).astype(BF16)
    return hi, lo


def _dot3(a, b):
    ah, al = _split(a)
    bh, bl = _split(b)
    d = functools.partial(jnp.dot, preferred_element_type=F32)
    return d(ah, bh) + (d(ah, bl) + d(al, bh))


def _dot_sel(a, onehot):
    ah, al = _split(a)
    e = onehot.astype(BF16)
    d = functools.partial(jnp.dot, preferred_element_type=F32)
    return d(ah, e) + d(al, e)


def _sel_dot(onehot, b):
    bh, bl = _split(b)
    e = onehot.astype(BF16)
    d = functools.partial(jnp.dot, preferred_element_type=F32)
    return d(e, bh) + d(e, bl)


def _mod_kernel(c_ref, w_ref, b_ref, o_ref):
    cond = _silu(c_ref[...])
    o_ref[0] = _dot3(cond, w_ref[0]) + b_ref[0]


def _modulation(c, mod_w, mod_b):
    depth, d, n = mod_w.shape
    tn = 1536
    c8 = jnp.broadcast_to(c[:1], (8, d))
    out = pl.pallas_call(
        _mod_kernel,
        grid=(depth, n // tn),
        in_specs=[pl.BlockSpec((8, d), lambda l, j: (0, 0)),
                  pl.BlockSpec((1, d, tn), lambda l, j: (l, 0, j)),
                  pl.BlockSpec((1, 1, tn), lambda l, j: (l, 0, j))],
        out_specs=pl.BlockSpec((1, 8, tn), lambda l, j: (l, 0, j)),
        out_shape=jax.ShapeDtypeStruct((depth, 8, n), F32),
        compiler_params=_params("parallel", "parallel"),
        name="modulation",
    )(c8, mod_w, mod_b.reshape(depth, 1, n))
    return out[:, 0:1, :]


def _mod_matmul_kernel(x_ref, sc_ref, sh_ref, w_ref, o_ref, h_scr):
    @pl.when(pl.program_id(1) == 0)
    def _():
        h_scr[...] = (x_ref[...] * (1.0 + sc_ref[...]) + sh_ref[...]).astype(BF16)

    o_ref[...] = jnp.dot(h_scr[...], w_ref[...].astype(BF16),
                         preferred_element_type=F32).astype(o_ref.dtype)


def _mod_matmul(x, sc, sh, w, *, out_dtype=F32, tm=MATMUL_ROWS, tn=512, name="mod_matmul"):
    m, k = x.shape
    n = w.shape[1]
    tm = min(tm, m)
    tn = min(tn, n)
    assert n % tn == 0 and m % tm == 0
    return pl.pallas_call(
        _mod_matmul_kernel,
        grid=(m // tm, n // tn),
        in_specs=[pl.BlockSpec((tm, k), lambda i, j: (i, 0)),
                  pl.BlockSpec((1, k), lambda i, j: (0, 0)),
                  pl.BlockSpec((1, k), lambda i, j: (0, 0)),
                  pl.BlockSpec((k, tn), lambda i, j: (0, j))],
        out_specs=pl.BlockSpec((tm, tn), lambda i, j: (i, j)),
        out_shape=jax.ShapeDtypeStruct((m, n), out_dtype),
        scratch_shapes=[pltpu.VMEM((tm, k), BF16)],
        compiler_params=_params("parallel", "arbitrary"),
        name=name,
    )(x, sc, sh, w)


def _swiglu_up_kernel(x_ref, sc_ref, sh_ref, wg_ref, wu_ref, o_ref, h_scr):
    @pl.when(pl.program_id(1) == 0)
    def _():
        h_scr[...] = (x_ref[...] * (1.0 + sc_ref[...]) + sh_ref[...]).astype(BF16)

    h = h_scr[...]
    g = jnp.dot(h, wg_ref[...].astype(BF16), preferred_element_type=F32)
    u = jnp.dot(h, wu_ref[...].astype(BF16), preferred_element_type=F32)
    o_ref[...] = (_silu(g) * u).astype(o_ref.dtype)


def _swiglu_up(x, sc, sh, w_gu, layer, *, tm=MATMUL_ROWS, tn=512):
    m, k = x.shape
    tm = min(tm, m)
    dff = w_gu.shape[2] // 2
    nb = dff // tn
    assert dff % tn == 0 and m % tm == 0
    return pl.pallas_call(
        _swiglu_up_kernel,
        grid=(m // tm, nb),
        in_specs=[pl.BlockSpec((tm, k), lambda i, j: (i, 0)),
                  pl.BlockSpec((1, k), lambda i, j: (0, 0)),
                  pl.BlockSpec((1, k), lambda i, j: (0, 0)),
                  pl.BlockSpec((None, k, tn), lambda i, j: (layer, 0, j)),
                  pl.BlockSpec((None, k, tn), lambda i, j: (layer, 0, j + nb))],
        out_specs=pl.BlockSpec((tm, tn), lambda i, j: (i, j)),
        out_shape=jax.ShapeDtypeStruct((m, dff), BF16),
        scratch_shapes=[pltpu.VMEM((tm, k), BF16)],
        compiler_params=_params("parallel", "arbitrary"),
        name="swiglu_up",
    )(x, sc, sh, w_gu, w_gu)


LN_ROWS = 256
LN_MAX_TK = 1536


def _matmul_ln_kernel(a_ref, w_ref, x_ref, ga_ref, g_ref, b_ref, o_ref, *, alpha):
    kk = pl.program_id(1)
    part = jnp.dot(a_ref[...].astype(BF16), w_ref[...].astype(BF16), preferred_element_type=F32)

    @pl.when(kk == 0)
    def _():
        o_ref[...] = part

    @pl.when(kk > 0)
    def _():
        o_ref[...] += part

    @pl.when(kk == pl.num_programs(1) - 1)
    def _():
        for r0 in range(0, o_ref.shape[0], LN_ROWS):
            rows = slice(r0, r0 + LN_ROWS)
            r = alpha * x_ref[rows, :] + (1.0 + ga_ref[...]) * o_ref[rows, :]
            mu = jnp.mean(r, axis=-1, keepdims=True)
            d = r - mu
            var = jnp.mean(d * d, axis=-1, keepdims=True)
            o_ref[rows, :] = d * lax.rsqrt(var + LN_EPS) * g_ref[...] + b_ref[...]


def _matmul_ln(a, w, layer, x, ga, g, b, *, alpha, tm=512):
    m, k = a.shape
    n = w.shape[2]
    tm = min(tm, m)
    tk = next(t for t in range(min(k, LN_MAX_TK), 0, -LANES) if k % t == 0)
    assert m % tm == 0 and k % tk == 0 and tm % LN_ROWS == 0
    return pl.pallas_call(
        functools.partial(_matmul_ln_kernel, alpha=alpha),
        grid=(m // tm, k // tk),
        in_specs=[pl.BlockSpec((tm, tk), lambda i, kk: (i, kk)),
                  pl.BlockSpec((None, tk, n), lambda i, kk: (layer, kk, 0)),
                  pl.BlockSpec((tm, n), lambda i, kk: (i, 0)),
                  pl.BlockSpec((1, n), lambda i, kk: (0, 0)),
                  pl.BlockSpec((1, n), lambda i, kk: (0, 0)),
                  pl.BlockSpec((1, n), lambda i, kk: (0, 0))],
        out_specs=pl.BlockSpec((tm, n), lambda i, kk: (i, 0)),
        out_shape=jax.ShapeDtypeStruct((m, n), F32),
        compiler_params=_params("parallel", "arbitrary"),
        name="matmul_ln",
    )(a, w, x, ga, g, b)


def _gdn_conv_kernel(xp_ref, xc_ref, w_ref, o_ref, buf, *, normalize, scale):
    ts, tc = xc_ref.shape
    halo = xp_ref.shape[0]
    buf[0:halo, :] = jnp.where(pl.program_id(0) > 0, xp_ref[...].astype(F32), 0.0)
    buf[halo:, :] = xc_ref[...].astype(F32)
    w = w_ref[...]
    first = halo - (GDN_CONV - 1)
    y = w[0:1, :] * buf[first:first + ts, :]
    for i in range(1, GDN_CONV):
        y = y + w[i:i + 1, :] * buf[first + i:first + i + ts, :]
    y = _silu(y)
    if normalize:
        for hh in range(tc // HEAD_DIM):
            seg = y[:, hh * HEAD_DIM:(hh + 1) * HEAD_DIM]
            ss = jnp.sum(seg * seg, axis=-1, keepdims=True)
            o_ref[:, hh * HEAD_DIM:(hh + 1) * HEAD_DIM] = seg * (lax.rsqrt(ss + NORM_EPS) * scale)
    else:
        o_ref[...] = y


def _gdn_conv(proj, conv_w, layer, col0, ncols, *, normalize, scale, ts=512, tc=512):
    s = proj.shape[0]
    cb0 = col0 // tc
    halo = CONV_HALO_ROWS
    hb = ts // halo
    return pl.pallas_call(
        functools.partial(_gdn_conv_kernel, normalize=normalize, scale=scale),
        grid=(s // ts, ncols // tc),
        in_specs=[pl.BlockSpec((halo, tc), lambda i, j: (jnp.maximum(i * hb - 1, 0), j + cb0)),
                  pl.BlockSpec((ts, tc), lambda i, j: (i, j + cb0)),
                  pl.BlockSpec((None, GDN_CONV, tc), lambda i, j: (layer, 0, j + cb0))],
        out_specs=pl.BlockSpec((ts, tc), lambda i, j: (i, j)),
        out_shape=jax.ShapeDtypeStruct((s, ncols), F32),
        scratch_shapes=[pltpu.VMEM((ts + halo, tc), F32)],
        compiler_params=_params("parallel", "parallel"),
        name="gdn_conv",
    )(proj, proj, conv_w)


def _gdn_gates_kernel(ba_ref, alog_ref, dtb_ref, out_ref):
    ts = ba_ref.shape[0]
    nh = ba_ref.shape[1] // 2
    x = ba_ref[...]
    beta = jax.nn.sigmoid(x[:, :nh])
    z = x[:, nh:] + dtb_ref[...]
    softplus = jnp.maximum(z, 0.0) + jnp.log1p(jnp.exp(-jnp.abs(z)))
    g = -jnp.exp(alog_ref[...]) * softplus
    ri = lax.broadcasted_iota(jnp.int32, (ts, ts), 0)
    ci = lax.broadcasted_iota(jnp.int32, (ts, ts), 1)
    same = (ri // GDN_BLOCK) == (ci // GDN_BLOCK)
    gc = _sel_dot(jnp.where(same & (ci <= ri), 1.0, 0.0), g)
    gl = _sel_dot(jnp.where(same, 1.0, 0.0), g)
    out_ref[...] = jnp.concatenate([beta, gc, gl, jnp.zeros((ts, LANES - 3 * nh), F32)], axis=1)


def _gdn_gates(ba, a_log, dt_bias, *, ts=512):
    s, two_h = ba.shape
    nh = two_h // 2
    assert 3 * nh <= LANES
    return pl.pallas_call(
        _gdn_gates_kernel,
        grid=(s // ts,),
        in_specs=[pl.BlockSpec((ts, two_h), lambda i: (i, 0)),
                  pl.BlockSpec((1, nh), lambda i: (0, 0)),
                  pl.BlockSpec((1, nh), lambda i: (0, 0))],
        out_specs=pl.BlockSpec((ts, LANES), lambda i: (i, 0)),
        out_shape=jax.ShapeDtypeStruct((s, LANES), F32),
        compiler_params=_params("parallel"),
        name="gdn_gates",
    )(ba, a_log.reshape(1, nh), dt_bias.reshape(1, nh))


def _bdot(a, b):
    return jnp.einsum("hij,hjk->hik", a.astype(BF16), b.astype(BF16), preferred_element_type=F32)


def _bdot3(a, b):
    ah, al = _split(a)
    bh, bl = _split(b)
    return jnp.einsum("hij,hjk->hik", jnp.concatenate([ah, ah, al], axis=2),
                      jnp.concatenate([bh, bl, bh], axis=1), preferred_element_type=F32)


def _unit_lower_inverse(low, ri, ci):
    n = low.shape[-1]

    def blk(b):
        return ((ri // b) == (ci // b))[None]

    eye = jnp.where(ri == ci, 1.0, 0.0)[None]
    n1 = jnp.where(blk(16), -low, 0.0)
    n2 = _bdot(n1, n1)
    n4 = _bdot(n2, n2)
    n8 = _bdot(n4, n4)
    t = eye + n1
    t = t + _bdot(n2, t)
    t = t + _bdot(n4, t)
    t = t + _bdot(n8, t)
    b = 32
    while b <= n:
        off = jnp.where(blk(b) & jnp.logical_not(blk(b // 2)), low, 0.0)
        t = t - _bdot(t, _bdot(off, t))
        b *= 2
    return t


def _gdn_chunk_kernel(q_ref, k_ref, v_ref, gates_ref, gct_ref, o_ref, state, *, heads):
    hg = pl.program_id(0)
    rows = q_ref.shape[0]
    nh = gct_ref.shape[0]
    dh = HEAD_DIM
    rep = rows // dh

    @pl.when(pl.program_id(1) == 0)
    def _():
        state[...] = jnp.zeros_like(state)

    ri = lax.broadcasted_iota(jnp.int32, (rows, rows), 0)
    ci = lax.broadcasted_iota(jnp.int32, (rows, rows), 1)
    lower = (ci <= ri)[None]
    strict = (ci < ri)[None]
    dot = functools.partial(jnp.dot, preferred_element_type=F32)

    gates_hi, gates_lo = _split(gates_ref[...])
    width = heads * 3 * dh
    prow = lax.broadcasted_iota(jnp.int32, (LANES, width), 0)
    pcol = lax.broadcasted_iota(jnp.int32, (LANES, width), 1)
    pick = jnp.where(prow == ((pcol // dh) % 3) * nh + hg * heads + pcol // (3 * dh), 1.0, 0.0).astype(BF16)
    cols = dot(jnp.concatenate([gates_hi, gates_lo], axis=1),
               jnp.concatenate([pick, pick], axis=0))

    def per_head(j):
        return jnp.stack([cols[:, (3 * hl + j) * dh:(3 * hl + j + 1) * dh] for hl in range(heads)])

    beta, gcol, glast = per_head(0), per_head(1), per_head(2)
    gct_hi, gct_lo = _split(gct_ref[...])
    trow = lax.broadcasted_iota(jnp.int32, (heads * rows, nh), 0) // rows
    tcol = lax.broadcasted_iota(jnp.int32, (heads * rows, nh), 1)
    pick_t = jnp.where(tcol == hg * heads + trow, 1.0, 0.0).astype(BF16)
    g_j = dot(jnp.concatenate([pick_t, pick_t], axis=1),
              jnp.concatenate([gct_hi, gct_lo], axis=0)).reshape(heads, rows, rows)
    g_i = jnp.concatenate([gcol] * rep, axis=2)
    beta_i = jnp.concatenate([beta] * rep, axis=2)
    decay = jnp.where(lower, jnp.exp(jnp.where(lower, g_i - g_j, 0.0)), 0.0)

    def pairs(ref):
        return [ref[:, p * dh:(p + 1) * dh] for p in range(heads // 2)]

    def nt(a, b):
        return jnp.einsum("pid,pjd->pij", a.astype(BF16), b.astype(BF16), preferred_element_type=F32)

    def both(x):
        return jnp.stack([x[hl // 2] for hl in range(heads)])

    qp, kp = jnp.stack(pairs(q_ref)), jnp.stack(pairs(k_ref))
    q, k = both(qp), both(kp)
    kk, qk = both(nt(kp, kp)), both(nt(qp, kp))
    v = jnp.stack([v_ref[:, hl * dh:(hl + 1) * dh] for hl in range(heads)])

    low = jnp.where(strict, kk * beta_i * decay, 0.0)
    t0 = _unit_lower_inverse(low, ri, ci)
    rhs = jnp.concatenate([v * beta, k * (beta * jnp.exp(gcol))], axis=2)
    x = _bdot(t0, rhs)
    x = x + _bdot(t0, rhs - x - _bdot3(low, x))
    u, w = x[:, :, :dh], x[:, :, dh:]
    a_x = _bdot(jnp.where(lower, qk * decay, 0.0), x)
    q_eff = q * jnp.exp(gcol) - a_x[:, :, dh:]
    k_g = k * jnp.exp(glast - gcol)

    s = state[...]
    o = _bdot(q_eff, s) + a_x[:, :, :dh]
    v_new = u - _bdot(w, s)
    state[...] = s * jnp.exp(glast[:, 0:1, :]) + jnp.einsum(
        "hck,hcv->hkv", k_g.astype(BF16), v_new.astype(BF16), preferred_element_type=F32)
    for hl in range(heads):
        o_ref[:, hl * dh:(hl + 1) * dh] = o[hl]


def _gdn_chunk(q, k, v, gates, gct, *, heads=16):
    rows = GDN_BLOCK
    s, dv_total = v.shape
    nh = dv_total // HEAD_DIM
    assert nh == 2 * (q.shape[1] // HEAD_DIM) and heads % 2 == 0 and nh % heads == 0
    qk_spec = pl.BlockSpec((rows, heads // 2 * HEAD_DIM), lambda h, i: (i, h))
    v_spec = pl.BlockSpec((rows, heads * HEAD_DIM), lambda h, i: (i, h))
    return pl.pallas_call(
        functools.partial(_gdn_chunk_kernel, heads=heads),
        grid=(nh // heads, s // rows),
        in_specs=[qk_spec, qk_spec, v_spec,
                  pl.BlockSpec((rows, LANES), lambda h, i: (i, 0)),
                  pl.BlockSpec((nh, rows), lambda h, i: (0, i))],
        out_specs=v_spec,
        out_shape=jax.ShapeDtypeStruct((s, dv_total), F32),
        scratch_shapes=[pltpu.VMEM((heads, HEAD_DIM, HEAD_DIM), F32)],
        compiler_params=_params("parallel", "arbitrary"),
        name="gdn_chunk",
    )(q, k, v, gates, gct)


def _gdn_post_kernel(o_ref, z_ref, nw_ref, out_ref):
    nw = nw_ref[...]
    for hh in range(o_ref.shape[1] // HEAD_DIM):
        sl = slice(hh * HEAD_DIM, (hh + 1) * HEAD_DIM)
        o = o_ref[:, sl]
        ms = jnp.mean(o * o, axis=-1, keepdims=True)
        out_ref[:, sl] = ((o * lax.rsqrt(ms + NORM_EPS)) * nw * _silu(z_ref[:, sl].astype(F32))).astype(out_ref.dtype)


def _gdn_post(o, proj, z_col0, norm_w, *, ts=256):
    s, dv = o.shape
    zb = z_col0 // dv
    return pl.pallas_call(
        _gdn_post_kernel,
        grid=(s // ts,),
        in_specs=[pl.BlockSpec((ts, dv), lambda i: (i, 0)),
                  pl.BlockSpec((ts, dv), lambda i: (i, zb)),
                  pl.BlockSpec((1, HEAD_DIM), lambda i: (0, 0))],
        out_specs=pl.BlockSpec((ts, dv), lambda i: (i, 0)),
        out_shape=jax.ShapeDtypeStruct((s, dv), BF16),
        compiler_params=_params("parallel"),
        name="gdn_post",
    )(o, proj, norm_w.reshape(1, HEAD_DIM))


def _gated_deltanet(x, sc, sh, w_in, conv_w, layer, a_log, dt_bias, norm_w):
    nvh = a_log.shape[0]
    dv = nvh * HEAD_DIM
    conv_ch = conv_w.shape[2]
    dqk = (conv_ch - dv) // 2
    n_main = conv_ch + dv
    proj = _mod_matmul(x, sc, sh, w_in[layer][:, :n_main], out_dtype=BF16, name="gdn_in_proj")
    ba = _mod_matmul(x, sc, sh, w_in[layer][:, n_main:], name="gdn_in_gates")
    q = _gdn_conv(proj, conv_w, layer, 0, dqk, normalize=True, scale=HEAD_DIM ** -0.5)
    k = _gdn_conv(proj, conv_w, layer, dqk, dqk, normalize=True, scale=1.0)
    v = _gdn_conv(proj, conv_w, layer, 2 * dqk, dv, normalize=False, scale=1.0)
    gates = _gdn_gates(ba, a_log, dt_bias)
    o = _gdn_chunk(q, k, v, gates, gates[:, nvh:2 * nvh].T)
    return _gdn_post(o, proj, conv_ch, norm_w)


def _rope(x, c, sg):
    return x * c + pltpu.roll(x, HEAD_DIM // 2, 1) * sg


def _nsa_prep_kernel(p_ref, c_ref, s_ref, q_ref, kvc_ref, ks_ref, vs_ref, kw_ref, vw_ref, *, dq, dkv):
    c = c_ref[...]
    sg = s_ref[...]
    dh = HEAD_DIM

    def head(col0, hh, rope, mul):
        x = p_ref[:, col0 + hh * dh:col0 + (hh + 1) * dh]
        if rope:
            x = _rope(x, c, sg)
        return x if mul == 1.0 else x * mul

    for hh in range(dq // dh):
        q_ref[:, hh * dh:(hh + 1) * dh] = head(0, hh, True, dh ** -0.5 * LOG2E).astype(q_ref.dtype)
    for hh in range(dkv // dh):
        sl = slice(hh * dh, (hh + 1) * dh)
        kvc_ref[0, hh] = head(dq, hh, True, 1.0)
        kvc_ref[1, hh] = head(dq + dkv, hh, False, 1.0)
        ks_ref[:, sl] = head(dq + 2 * dkv, hh, True, 1.0).astype(ks_ref.dtype)
        vs_ref[:, sl] = head(dq + 3 * dkv, hh, False, 1.0).astype(vs_ref.dtype)
        kw_ref[:, sl] = head(dq + 4 * dkv, hh, True, 1.0).astype(kw_ref.dtype)
        vw_ref[:, sl] = head(dq + 5 * dkv, hh, False, 1.0).astype(vw_ref.dtype)


def _nsa_prep(proj, cos2, sin2, dq, dkv, *, ts=256):
    s, n = proj.shape
    hk = dkv // HEAD_DIM
    kv_spec = pl.BlockSpec((ts, dkv), lambda i: (i, 0))
    tab = pl.BlockSpec((ts, HEAD_DIM), lambda i: (i, 0))
    kv16 = jax.ShapeDtypeStruct((s, dkv), BF16)
    return pl.pallas_call(
        functools.partial(_nsa_prep_kernel, dq=dq, dkv=dkv),
        grid=(s // ts,),
        in_specs=[pl.BlockSpec((ts, n), lambda i: (i, 0)), tab, tab],
        out_specs=[pl.BlockSpec((ts, dq), lambda i: (i, 0)),
                   pl.BlockSpec((2, hk, ts, HEAD_DIM), lambda i: (0, 0, i, 0)),
                   kv_spec, kv_spec, kv_spec, kv_spec],
        out_shape=[jax.ShapeDtypeStruct((s, dq), BF16),
                   jax.ShapeDtypeStruct((2, hk, s, HEAD_DIM), F32),
                   kv16, kv16, kv16, kv16],
        compiler_params=_params("parallel"),
        name="nsa_prep",
    )(proj, cos2, sin2)


def _nsa_compress_kernel(x_ref, pe_ref, w1_ref, w2_ref, o_ref):
    x = x_ref[0, 0]
    n = x.shape[0]
    a = _dot(x + pe_ref[0, 0:1, :], w1_ref[0, 0])
    b = _dot(x + pe_ref[0, 1:2, :], w1_ref[0, 1])
    y = a + pltpu.roll(b, n - 1, 0)
    o_ref[0, 0] = _dot(_silu(y), w2_ref[0])


def _nsa_compress(x, pe, w1, w2):
    _, hk, ng, gw = x.shape
    half = CMP_LEN // 2
    pe2 = pe.reshape(2, 2, half * HEAD_DIM)
    w1r = w1.reshape(2, 2, half * HEAD_DIM, HEAD_DIM)
    return pl.pallas_call(
        _nsa_compress_kernel,
        grid=(2, hk),
        in_specs=[pl.BlockSpec((1, 1, ng, gw), lambda t, h: (t, h, 0, 0)),
                  pl.BlockSpec((1, 2, gw), lambda t, h: (t, 0, 0)),
                  pl.BlockSpec((1, 2, gw, HEAD_DIM), lambda t, h: (t, 0, 0, 0)),
                  pl.BlockSpec((1, HEAD_DIM, HEAD_DIM), lambda t, h: (t, 0, 0))],
        out_specs=pl.BlockSpec((1, 1, ng, HEAD_DIM), lambda t, h: (t, h, 0, 0)),
        out_shape=jax.ShapeDtypeStruct((2, hk, ng, HEAD_DIM), F32),
        compiler_params=_params("parallel", "parallel"),
        name="nsa_compress",
    )(x, pe2, w1r, w2)


def _nsa_cmp_select_kernel(q_ref, kc_ref, vc_ref, ov_ref, o_ref, sel_ref):
    tq = q_ref.shape[0]
    nc = kc_ref.shape[2]
    nbp = ov_ref.shape[1]
    t0 = pl.program_id(0) * tq
    tpos = t0 + lax.broadcasted_iota(jnp.int32, (tq, nc), 0)
    nidx = lax.broadcasted_iota(jnp.int32, (tq, nc), 1)
    valid = (nidx * CMP_STRIDE + (CMP_LEN - 1)) <= tpos
    kc = kc_ref[0, 0].astype(BF16)
    vc = vc_ref[0, 0].astype(BF16)
    psum = jnp.zeros((tq, nc), F32)
    q4 = jnp.concatenate([q_ref[:, g * HEAD_DIM:(g + 1) * HEAD_DIM] for g in range(NSA_GROUP)], axis=0)
    s_all = _dot_nt(q4, kc)
    probs = []
    for g in range(NSA_GROUP):
        s = jnp.where(valid, s_all[g * tq:(g + 1) * tq, :], NEG_BIG)
        m = jnp.max(s, axis=-1, keepdims=True)
        p = jnp.where(valid, jnp.exp2(s - m), 0.0)
        den = jnp.sum(p, axis=-1, keepdims=True)
        p = p / jnp.where(den > 0.0, den, 1.0)
        probs.append(p.astype(BF16))
        psum = psum + p
    o_all = jnp.dot(jnp.concatenate(probs, axis=0), vc, preferred_element_type=F32)
    for g in range(NSA_GROUP):
        o_ref[:, g * HEAD_DIM:(g + 1) * HEAD_DIM] = o_all[g * tq:(g + 1) * tq, :].astype(o_ref.dtype)
    imp = _dot_sel(psum, ov_ref[...])
    imp_t = imp.T
    j = lax.broadcasted_iota(jnp.int32, (nbp, tq), 0)
    cur = (t0 + lax.broadcasted_iota(jnp.int32, (nbp, tq), 1)) // SLC_LEN
    forced = (j == 0) | (j == cur) | (j == cur - 1)
    work = jnp.where(j <= cur, jnp.where(forced, jnp.inf, imp_t), -jnp.inf)
    jf = j.astype(F32)
    for _ in range(SLC_TOP):
        m = jnp.max(work, axis=0, keepdims=True)
        first = jnp.min(jnp.where(work == m, jf, float(nbp)), axis=0, keepdims=True)
        work = jnp.where(jf == first, -jnp.inf, work)
    sel_ref[0] = jnp.where((j <= cur) & (work == -jnp.inf), 0.0, MASK_NEG).T.astype(sel_ref.dtype)


def _nsa_cmp_select(q, cmp, ov, *, tq=512):
    s, dq = q.shape
    _, hk, nc, _ = cmp.shape
    nbp = ov.shape[1]
    gw = NSA_GROUP * HEAD_DIM
    return pl.pallas_call(
        _nsa_cmp_select_kernel,
        grid=(s // tq, hk),
        in_specs=[pl.BlockSpec((tq, gw), lambda i, h: (i, h)),
                  pl.BlockSpec((1, 1, nc, HEAD_DIM), lambda i, h: (0, h, 0, 0)),
                  pl.BlockSpec((1, 1, nc, HEAD_DIM), lambda i, h: (1, h, 0, 0)),
                  pl.BlockSpec((nc, nbp), lambda i, h: (0, 0))],
        out_specs=[pl.BlockSpec((tq, gw), lambda i, h: (i, h)),
                   pl.BlockSpec((1, tq, nbp), lambda i, h: (h, i, 0))],
        out_shape=[jax.ShapeDtypeStruct((s, dq), BF16),
                   jax.ShapeDtypeStruct((hk, s, nbp), BF16)],
        compiler_params=_params("parallel", "parallel"),
        name="nsa_cmp_select",
    )(q, cmp, cmp, ov)


ATTN_ROWS = 64


def _masked_attn_kernel(*refs, tq, nk, sub, selected):
    qi_ref, kj_ref, first_ref, last_ref, q_ref = refs[:5]
    k_refs, v_refs = refs[5:5 + nk], refs[5 + nk:5 + 2 * nk]
    rest = refs[5 + 2 * nk:]
    if selected:
        sel_ref, expand_ref = rest[:2]
        rest = rest[2:]
    o_ref, q4, bias, s_scr, p_scr, a_scr, m_scr, acc = rest
    t = pl.program_id(1)
    qi = qi_ref[t]
    kj = kj_ref[t]
    dh = HEAD_DIM
    tk = nk * k_refs[0].shape[0]

    @pl.when(first_ref[t] == 1)
    def _():
        for g in range(NSA_GROUP):
            q4[g * tq:(g + 1) * tq, :] = q_ref[:, g * dh:(g + 1) * dh]
        m_scr[...] = jnp.full_like(m_scr, NEG_BIG)
        acc[...] = jnp.zeros_like(acc)

    k0 = kj * k_refs[0].shape[0]

    def causal():
        tpos = qi * tq + lax.broadcasted_iota(jnp.int32, (tq, tk), 0)
        kpos = k0 + lax.broadcasted_iota(jnp.int32, (tq, tk), 1)
        return tpos, kpos

    if selected:
        bias[...] = jnp.dot(sel_ref[0], expand_ref[...], preferred_element_type=F32).astype(BF16)

        @pl.when(k0 + tk - 1 > qi * tq)
        def _():
            tpos, kpos = causal()
            bias[...] = bias[...] + jnp.where(kpos <= tpos, 0.0, MASK_NEG).astype(BF16)
    else:
        tpos, kpos = causal()
        bias[...] = jnp.where((kpos <= tpos) & (kpos > tpos - WINDOW) & (kpos >= 0),
                              0.0, MASK_NEG).astype(BF16)

    tiles = [(r, u * sub) for r in range(nk) for u in range(k_refs[0].shape[0] // sub)]
    for n, (r, off) in enumerate(tiles):
        s_scr[:, n * sub:(n + 1) * sub] = lax.dot_general(
            q4[...], k_refs[r][off:off + sub, :], (((1,), (1,)), ((), ())),
            preferred_element_type=F32).astype(BF16)
    rb = ATTN_ROWS
    for u, (r, off) in enumerate(tiles):
        cols = slice(u * sub, (u + 1) * sub)
        for r0 in range(0, NSA_GROUP * tq, rb):
            rows = slice(r0, r0 + rb)
            brow = r0 % tq
            s = s_scr[rows, cols] + bias[brow:brow + rb, cols]
            m_prev = m_scr[rows, :]
            m_new = jnp.maximum(m_prev, jnp.max(s, axis=-1, keepdims=True).astype(F32))
            a_scr[u, rows, :] = jnp.exp2(m_prev - m_new)
            shift = jnp.concatenate([m_new.astype(BF16)] * (sub // LANES), axis=1)
            p_scr[rows, cols] = jnp.exp2(s - shift)
            m_scr[rows, :] = m_new
        v_one = jnp.concatenate([v_refs[r][off:off + sub, :], jnp.ones((sub, dh), BF16)], axis=1)
        alpha = jnp.concatenate([a_scr[u]] * 2, axis=1)
        acc[...] = alpha * acc[...] + jnp.dot(p_scr[:, cols], v_one, preferred_element_type=F32)

    @pl.when(last_ref[t] == 1)
    def _():
        for g in range(NSA_GROUP):
            rows = slice(g * tq, (g + 1) * tq)
            o_ref[:, g * dh:(g + 1) * dh] = (acc[rows, :dh] / acc[rows, dh:]).astype(o_ref.dtype)


def _masked_attn(q, k, v, sel, expand, *, tq, tk, sub):
    s, dq = q.shape
    hk = k.shape[1] // HEAD_DIM
    gw = NSA_GROUP * HEAD_DIM
    selected = sel is not None
    if selected:
        nk = 1
        pairs = []
        for i in range(s // tq):
            hi = (i * tq + tq - 1) // tk
            pairs += [(i, j, int(j == 0), int(j == hi)) for j in range(hi + 1)]
    else:
        assert tq % tk == 0
        nback = -(-(WINDOW - 1) // tk)
        nk = nback + tq // tk
        pairs = [(i, i * (tq // tk) - nback, 1, 1) for i in range(s // tq)]
    sched = [jnp.asarray([p[c] for p in pairs], jnp.int32) for c in range(4)]

    def kv_spec(r):
        return pl.BlockSpec((tk, HEAD_DIM), lambda h, t, qi, kj, fi, la: (jnp.maximum(kj[t] + r, 0), h))

    in_specs = [pl.BlockSpec((tq, gw), lambda h, t, qi, kj, fi, la: (qi[t], h))]
    in_specs += [kv_spec(r) for r in range(nk)] * 2
    args = [q] + [k] * nk + [v] * nk
    if selected:
        in_specs.append(pl.BlockSpec((1, tq, sel.shape[2]), lambda h, t, qi, kj, fi, la: (h, qi[t], 0)))
        in_specs.append(pl.BlockSpec((expand.shape[0], tk), lambda h, t, qi, kj, fi, la: (0, kj[t])))
        args += [sel, expand]
    rows = NSA_GROUP * tq
    tk = nk * tk
    return pl.pallas_call(
        functools.partial(_masked_attn_kernel, tq=tq, nk=nk, sub=sub, selected=selected),
        grid_spec=pltpu.PrefetchScalarGridSpec(
            num_scalar_prefetch=4,
            grid=(hk, len(pairs)),
            in_specs=in_specs,
            out_specs=pl.BlockSpec((tq, gw), lambda h, t, qi, kj, fi, la: (qi[t], h)),
            scratch_shapes=[pltpu.VMEM((rows, HEAD_DIM), BF16),
                            pltpu.VMEM((tq, tk), BF16),
                            pltpu.VMEM((rows, tk), BF16),
                            pltpu.VMEM((rows, tk), BF16),
                            pltpu.VMEM((tk // sub, rows, LANES), F32),
                            pltpu.VMEM((rows, LANES), F32),
                            pltpu.VMEM((rows, 2 * HEAD_DIM), F32)]),
        out_shape=jax.ShapeDtypeStruct((s, dq), BF16),
        compiler_params=_params("parallel", "arbitrary"),
        name="nsa_selected_attn" if selected else "nsa_window_attn",
    )(*sched, *args)


def _nsa_combine_kernel(oc_ref, os_ref, ow_ref, g_ref, out_ref):
    gates = jax.nn.sigmoid(g_ref[...])
    ng, dq = g_ref.shape[1], oc_ref.shape[1]
    row = lax.broadcasted_iota(jnp.int32, (ng, dq), 0)
    head3 = (lax.broadcasted_iota(jnp.int32, (ng, dq), 1) // HEAD_DIM) * N_BRANCH
    acc = None
    for b, ref in enumerate((oc_ref, os_ref, ow_ref)):
        gb = _dot_sel(gates, jnp.where(row == head3 + b, 1.0, 0.0))
        term = gb * ref[...]
        acc = term if acc is None else acc + term
    out_ref[...] = acc.astype(out_ref.dtype)


def _nsa_combine(oc, osel, ow, gates, *, ts=256):
    s, dq = oc.shape
    spec = pl.BlockSpec((ts, dq), lambda i: (i, 0))
    return pl.pallas_call(
        _nsa_combine_kernel,
        grid=(s // ts,),
        in_specs=[spec, spec, spec, pl.BlockSpec((ts, gates.shape[1]), lambda i: (i, 0))],
        out_specs=spec,
        out_shape=jax.ShapeDtypeStruct((s, dq), BF16),
        compiler_params=_params("parallel"),
        name="nsa_combine",
    )(oc, osel, ow, gates)


def _nsa_attention(x, sc, sh, cos2, sin2, w_in, layer, cmp_pe, cmp_w1, cmp_w2):
    s, d = x.shape
    dq = d
    hk = d // HEAD_DIM // NSA_GROUP
    dkv = hk * HEAD_DIM
    n_main = dq + 6 * dkv
    ngate = w_in.shape[2] - n_main
    proj = _mod_matmul(x, sc, sh, w_in[layer][:, :n_main], name="nsa_in_proj")
    w_gate = jnp.pad(w_in[layer][:, n_main:], ((0, 0), (0, LANES - ngate)))
    gates = _mod_matmul(x, sc, sh, w_gate, name="nsa_in_gates")
    q, kvc, ks, vs, kw, vw = _nsa_prep(proj, cos2, sin2, dq, dkv)
    groups = kvc.reshape(2, hk, s // CMP_STRIDE, CMP_STRIDE * HEAD_DIM)
    cmp = _nsa_compress(groups, cmp_pe, cmp_w1, cmp_w2)

    nc = s // CMP_STRIDE
    n_slc = s // SLC_LEN
    nbp = -(-n_slc // LANES) * LANES
    c_start = CMP_STRIDE * jnp.arange(nc)
    s_start = SLC_LEN * jnp.arange(nbp)
    overlap = jnp.clip(jnp.minimum(c_start[:, None] + CMP_LEN, s_start[None, :] + SLC_LEN)
                       - jnp.maximum(c_start[:, None], s_start[None, :]), 0, None).astype(F32) / CMP_LEN
    o_c, sel = _nsa_cmp_select(q, cmp, overlap.astype(BF16))
    expand = (jnp.arange(nbp)[:, None] == jnp.arange(s)[None, :] // SLC_LEN).astype(BF16)
    o_s = _masked_attn(q, ks, vs, sel, expand, tq=512, tk=1024, sub=512)
    o_w = _masked_attn(q, kw, vw, None, None, tq=256, tk=256, sub=256)
    return _nsa_combine(o_c, o_s, o_w, gates)


def kernel(x, c, positions, mod_w, mod_b, ln_g, ln_b, ffn_w_gu, ffn_w_down, gdn_w_in, gdn_conv_w,
           gdn_a_log, gdn_dt_bias, gdn_norm_w, gdn_w_out, nsa_w_in, nsa_cmp_pe, nsa_cmp_w1,
           nsa_cmp_w2, nsa_w_out):
    bsz, s, d = x.shape
    assert bsz == 1
    depth = mod_w.shape[0]
    alpha = (2 * depth) ** 0.25

    inv = ROPE_THETA ** (-jnp.arange(0, HEAD_DIM, 2, dtype=F32) / HEAD_DIM)
    ang = positions[0].astype(F32)[:, None] * inv
    cos, sin = jnp.cos(ang), jnp.sin(ang)
    cos2 = jnp.concatenate([cos, cos], axis=-1)
    sin2 = jnp.concatenate([-sin, sin], axis=-1)

    mod = _modulation(c, mod_w, mod_b)
    gdn_w_out, nsa_w_out, ffn_w_down = [w.astype(BF16) for w in (gdn_w_out, nsa_w_out, ffn_w_down)]
    xs = x[0]
    for i in range(depth):
        sh1, sc1, ga1, sh2, sc2, ga2 = [mod[i, :, r * d:(r + 1) * d] for r in range(6)]
        j = i // N_MIXERS
        if i % N_MIXERS == 0:
            y = _gated_deltanet(xs, sc1, sh1, gdn_w_in, gdn_conv_w, j, gdn_a_log[j],
                                gdn_dt_bias[j], gdn_norm_w[j])
            w_out = gdn_w_out
        else:
            y = _nsa_attention(xs, sc1, sh1, cos2, sin2, nsa_w_in, j, nsa_cmp_pe[j],
                               nsa_cmp_w1[j], nsa_cmp_w2[j])
            w_out = nsa_w_out
        xs = _matmul_ln(y, w_out, j, xs, ga1, ln_g[i, 0:1], ln_b[i, 0:1], alpha=alpha)
        a = _swiglu_up(xs, sc2, sh2, ffn_w_gu, i)
        xs = _matmul_ln(a, ffn_w_down, i, xs, ga2, ln_g[i, 1:2], ln_b[i, 1:2], alpha=alpha)
    return xs[None]
```

```python
import functools

import jax
import jax.numpy as jnp
from jax import lax
from jax.experimental import pallas as pl
from jax.experimental.pallas import tpu as pltpu

F32 = jnp.float32
BF16 = jnp.bfloat16
HIGHEST = lax.Precision.HIGHEST

HEAD_DIM = 128
N_MIXERS = 2
GDN_CONV = 4
GDN_BLOCK = 128
NSA_GROUP = 4
N_BRANCH = 3
CMP_LEN = 32
CMP_STRIDE = 16
SLC_LEN = 64
SLC_TOP = 16
WINDOW = 512
ROPE_THETA = 10000.0
LN_EPS = 1e-5
NORM_EPS = 1e-6

LOG2E = 1.4426950408889634
MATMUL_ROWS = 1024
LANES = 128
CONV_HALO_ROWS = 16
VMEM_LIMIT_BYTES = 56 * 1024 * 1024
NEG_BIG = -1e30
MASK_NEG = -2.0 ** 100


def _params(*sem):
    return pltpu.CompilerParams(dimension_semantics=sem, vmem_limit_bytes=VMEM_LIMIT_BYTES)


def _silu(v):
    return v * jax.nn.sigmoid(v)


def _dot(a, b):
    return jnp.dot(a.astype(BF16), b.astype(BF16), preferred_element_type=F32)


def _dot_nt(a, b):
    return lax.dot_general(a.astype(BF16), b.astype(BF16), (((1,), (1,)), ((), ())),
                           preferred_element_type=F32)


def _split(a):
    hi = a.astype(BF16)
    lo = (a - hi.astype(F32)).astype(BF16)
    return hi, lo


def _dot3(a, b):
    ah, al = _split(a)
    bh, bl = _split(b)
    d = functools.partial(jnp.dot, preferred_element_type=F32)
    return d(ah, bh) + (d(ah, bl) + d(al, bh))


def _dot_sel(a, onehot):
    ah, al = _split(a)
    e = onehot.astype(BF16)
    d = functools.partial(jnp.dot, preferred_element_type=F32)
    return d(ah, e) + d(al, e)


def _sel_dot(onehot, b):
    bh, bl = _split(b)
    e = onehot.astype(BF16)
    d = functools.partial(jnp.dot, preferred_element_type=F32)
    return d(e, bh) + d(e, bl)


def _mod_kernel(c_ref, w_ref, b_ref, o_ref):
    cond = _silu(c_ref[...])
    o_ref[0] = _dot3(cond, w_ref[0]) + b_ref[0]


def _modulation(c, mod_w, mod_b):
    depth, d, n = mod_w.shape
    tn = 1536
    c8 = jnp.broadcast_to(c[:1], (8, d))
    out = pl.pallas_call(
        _mod_kernel,
        grid=(depth, n // tn),
        in_specs=[pl.BlockSpec((8, d), lambda l, j: (0, 0)),
                  pl.BlockSpec((1, d, tn), lambda l, j: (l, 0, j)),
                  pl.BlockSpec((1, 1, tn), lambda l, j: (l, 0, j))],
        out_specs=pl.BlockSpec((1, 8, tn), lambda l, j: (l, 0, j)),
        out_shape=jax.ShapeDtypeStruct((depth, 8, n), F32),
        compiler_params=_params("parallel", "parallel"),
        name="modulation",
    )(c8, mod_w, mod_b.reshape(depth, 1, n))
    return out[:, 0:1, :]


def _mod_matmul_kernel(x_ref, sc_ref, sh_ref, w_ref, wg_ref, o_ref, og_ref, h_scr):
    @pl.when(pl.program_id(1) == 0)
    def _():
        h_scr[...] = (x_ref[...] * (1.0 + sc_ref[...]) + sh_ref[...]).astype(BF16)
        og_ref[...] = jnp.dot(h_scr[...], wg_ref[...].astype(BF16), preferred_element_type=F32)

    o_ref[...] = jnp.dot(h_scr[...], w_ref[...].astype(BF16),
                         preferred_element_type=F32).astype(o_ref.dtype)


def _mod_matmul(x, sc, sh, w, w_gate, *, out_dtype=F32, tm=MATMUL_ROWS, tn=512, name="mod_matmul"):
    m, k = x.shape
    n = w.shape[1]
    ng = w_gate.shape[1]
    tm = min(tm, m)
    tn = min(tn, n)
    assert n % tn == 0 and m % tm == 0
    return pl.pallas_call(
        _mod_matmul_kernel,
        grid=(m // tm, n // tn),
        in_specs=[pl.BlockSpec((tm, k), lambda i, j: (i, 0)),
                  pl.BlockSpec((1, k), lambda i, j: (0, 0)),
                  pl.BlockSpec((1, k), lambda i, j: (0, 0)),
                  pl.BlockSpec((k, tn), lambda i, j: (0, j)),
                  pl.BlockSpec((k, ng), lambda i, j: (0, 0))],
        out_specs=[pl.BlockSpec((tm, tn), lambda i, j: (i, j)),
                   pl.BlockSpec((tm, ng), lambda i, j: (i, 0))],
        out_shape=[jax.ShapeDtypeStruct((m, n), out_dtype), jax.ShapeDtypeStruct((m, ng), F32)],
        scratch_shapes=[pltpu.VMEM((tm, k), BF16)],
        compiler_params=_params("parallel", "arbitrary"),
        name=name,
    )(x, sc, sh, w, w_gate)


def _swiglu_up_kernel(x_ref, sc_ref, sh_ref, wg_ref, wu_ref, o_ref, h_scr):
    @pl.when(pl.program_id(1) == 0)
    def _():
        h_scr[...] = (x_ref[...] * (1.0 + sc_ref[...]) + sh_ref[...]).astype(BF16)

    h = h_scr[...]
    g = jnp.dot(h, wg_ref[...].astype(BF16), preferred_element_type=F32)
    u = jnp.dot(h, wu_ref[...].astype(BF16), preferred_element_type=F32)
    o_ref[...] = (_silu(g) * u).astype(o_ref.dtype)


def _swiglu_up(x, sc, sh, w_gu, layer, *, tm=MATMUL_ROWS, tn=512):
    m, k = x.shape
    tm = min(tm, m)
    dff = w_gu.shape[2] // 2
    nb = dff // tn
    assert dff % tn == 0 and m % tm == 0
    return pl.pallas_call(
        _swiglu_up_kernel,
        grid=(m // tm, nb),
        in_specs=[pl.BlockSpec((tm, k), lambda i, j: (i, 0)),
                  pl.BlockSpec((1, k), lambda i, j: (0, 0)),
                  pl.BlockSpec((1, k), lambda i, j: (0, 0)),
                  pl.BlockSpec((None, k, tn), lambda i, j: (layer, 0, j)),
                  pl.BlockSpec((None, k, tn), lambda i, j: (layer, 0, j + nb))],
        out_specs=pl.BlockSpec((tm, tn), lambda i, j: (i, j)),
        out_shape=jax.ShapeDtypeStruct((m, dff), BF16),
        scratch_shapes=[pltpu.VMEM((tm, k), BF16)],
        compiler_params=_params("parallel", "arbitrary"),
        name="swiglu_up",
    )(x, sc, sh, w_gu, w_gu)


LN_ROWS = 256
LN_MAX_TK = 1536


def _matmul_ln_kernel(a_ref, w_ref, x_ref, ga_ref, g_ref, b_ref, o_ref, *, alpha):
    kk = pl.program_id(1)
    part = jnp.dot(a_ref[...].astype(BF16), w_ref[...].astype(BF16), preferred_element_type=F32)

    @pl.when(kk == 0)
    def _():
        o_ref[...] = part

    @pl.when(kk > 0)
    def _():
        o_ref[...] += part

    @pl.when(kk == pl.num_programs(1) - 1)
    def _():
        for r0 in range(0, o_ref.shape[0], LN_ROWS):
            rows = slice(r0, r0 + LN_ROWS)
            r = alpha * x_ref[rows, :] + (1.0 + ga_ref[...]) * o_ref[rows, :]
            mu = jnp.mean(r, axis=-1, keepdims=True)
            d = r - mu
            var = jnp.mean(d * d, axis=-1, keepdims=True)
            o_ref[rows, :] = d * lax.rsqrt(var + LN_EPS) * g_ref[...] + b_ref[...]


def _matmul_ln(a, w, layer, x, ga, g, b, *, alpha, tm=512):
    m, k = a.shape
    n = w.shape[2]
    tm = min(tm, m)
    tk = next(t for t in range(min(k, LN_MAX_TK), 0, -LANES) if k % t == 0)
    assert m % tm == 0 and k % tk == 0 and tm % LN_ROWS == 0
    return pl.pallas_call(
        functools.partial(_matmul_ln_kernel, alpha=alpha),
        grid=(m // tm, k // tk),
        in_specs=[pl.BlockSpec((tm, tk), lambda i, kk: (i, kk)),
                  pl.BlockSpec((None, tk, n), lambda i, kk: (layer, kk, 0)),
                  pl.BlockSpec((tm, n), lambda i, kk: (i, 0)),
                  pl.BlockSpec((1, n), lambda i, kk: (0, 0)),
                  pl.BlockSpec((1, n), lambda i, kk: (0, 0)),
                  pl.BlockSpec((1, n), lambda i, kk: (0, 0))],
        out_specs=pl.BlockSpec((tm, n), lambda i, kk: (i, 0)),
        out_shape=jax.ShapeDtypeStruct((m, n), F32),
        compiler_params=_params("parallel", "arbitrary"),
        name="matmul_ln",
    )(a, w, x, ga, g, b)


def _gdn_conv_kernel(xp_ref, xc_ref, w_ref, o_ref, buf, *, normalize, scale):
    ts, tc = xc_ref.shape
    halo = xp_ref.shape[0]
    buf[0:halo, :] = jnp.where(pl.program_id(0) > 0, xp_ref[...].astype(F32), 0.0)
    buf[halo:, :] = xc_ref[...].astype(F32)
    w = w_ref[...]
    first = halo - (GDN_CONV - 1)
    y = w[0:1, :] * buf[first:first + ts, :]
    for i in range(1, GDN_CONV):
        y = y + w[i:i + 1, :] * buf[first + i:first + i + ts, :]
    y = _silu(y)
    if normalize:
        for hh in range(tc // HEAD_DIM):
            seg = y[:, hh * HEAD_DIM:(hh + 1) * HEAD_DIM]
            ss = jnp.sum(seg * seg, axis=-1, keepdims=True)
            o_ref[:, hh * HEAD_DIM:(hh + 1) * HEAD_DIM] = seg * (lax.rsqrt(ss + NORM_EPS) * scale)
    else:
        o_ref[...] = y


def _gdn_conv(proj, conv_w, layer, col0, ncols, *, normalize, scale, ts=512, tc=512):
    s = proj.shape[0]
    cb0 = col0 // tc
    halo = CONV_HALO_ROWS
    hb = ts // halo
    return pl.pallas_call(
        functools.partial(_gdn_conv_kernel, normalize=normalize, scale=scale),
        grid=(s // ts, ncols // tc),
        in_specs=[pl.BlockSpec((halo, tc), lambda i, j: (jnp.maximum(i * hb - 1, 0), j + cb0)),
                  pl.BlockSpec((ts, tc), lambda i, j: (i, j + cb0)),
                  pl.BlockSpec((None, GDN_CONV, tc), lambda i, j: (layer, 0, j + cb0))],
        out_specs=pl.BlockSpec((ts, tc), lambda i, j: (i, j)),
        out_shape=jax.ShapeDtypeStruct((s, ncols), F32),
        scratch_shapes=[pltpu.VMEM((ts + halo, tc), F32)],
        compiler_params=_params("parallel", "parallel"),
        name="gdn_conv",
    )(proj, proj, conv_w)


def _gdn_gates_kernel(ba_ref, alog_ref, dtb_ref, out_ref):
    ts = ba_ref.shape[0]
    nh = ba_ref.shape[1] // 2
    x = ba_ref[...]
    beta = jax.nn.sigmoid(x[:, :nh])
    z = x[:, nh:] + dtb_ref[...]
    softplus = jnp.maximum(z, 0.0) + jnp.log1p(jnp.exp(-jnp.abs(z)))
    g = -jnp.exp(alog_ref[...]) * softplus
    ri = lax.broadcasted_iota(jnp.int32, (ts, ts), 0)
    ci = lax.broadcasted_iota(jnp.int32, (ts, ts), 1)
    same = (ri // GDN_BLOCK) == (ci // GDN_BLOCK)
    gc = _sel_dot(jnp.where(same & (ci <= ri), 1.0, 0.0), g)
    gl = _sel_dot(jnp.where(same, 1.0, 0.0), g)
    out_ref[...] = jnp.concatenate([beta, gc, gl, jnp.zeros((ts, LANES - 3 * nh), F32)], axis=1)


def _gdn_gates(ba, a_log, dt_bias, *, ts=512):
    s, two_h = ba.shape
    nh = two_h // 2
    assert 3 * nh <= LANES
    return pl.pallas_call(
        _gdn_gates_kernel,
        grid=(s // ts,),
        in_specs=[pl.BlockSpec((ts, two_h), lambda i: (i, 0)),
                  pl.BlockSpec((1, nh), lambda i: (0, 0)),
                  pl.BlockSpec((1, nh), lambda i: (0, 0))],
        out_specs=pl.BlockSpec((ts, LANES), lambda i: (i, 0)),
        out_shape=jax.ShapeDtypeStruct((s, LANES), F32),
        compiler_params=_params("parallel"),
        name="gdn_gates",
    )(ba, a_log.reshape(1, nh), dt_bias.reshape(1, nh))


def _bdot(a, b):
    return jnp.einsum("hij,hjk->hik", a.astype(BF16), b.astype(BF16), preferred_element_type=F32)


def _bdot3(a, b):
    ah, al = _split(a)
    bh, bl = _split(b)
    return jnp.einsum("hij,hjk->hik", jnp.concatenate([ah, ah, al], axis=2),
                      jnp.concatenate([bh, bl, bh], axis=1), preferred_element_type=F32)


def _unit_lower_inverse(low, ri, ci):
    n = low.shape[-1]

    def blk(b):
        return ((ri // b) == (ci // b))[None]

    eye = jnp.where(ri == ci, 1.0, 0.0)[None]
    n1 = jnp.where(blk(16), -low, 0.0)
    n2 = _bdot(n1, n1)
    n4 = _bdot(n2, n2)
    n8 = _bdot(n4, n4)
    t = eye + n1
    t = t + _bdot(n2, t)
    t = t + _bdot(n4, t)
    t = t + _bdot(n8, t)
    b = 32
    while b <= n:
        off = jnp.where(blk(b) & jnp.logical_not(blk(b // 2)), low, 0.0)
        t = t - _bdot(t, _bdot(off, t))
        b *= 2
    return t


def _gdn_chunk_kernel(q_ref, k_ref, v_ref, gates_ref, gct_ref, o_ref, state, *, heads):
    hg = pl.program_id(0)
    rows = q_ref.shape[0]
    nh = gct_ref.shape[0]
    dh = HEAD_DIM
    rep = rows // dh

    @pl.when(pl.program_id(1) == 0)
    def _():
        state[...] = jnp.zeros_like(state)

    ri = lax.broadcasted_iota(jnp.int32, (rows, rows), 0)
    ci = lax.broadcasted_iota(jnp.int32, (rows, rows), 1)
    lower = (ci <= ri)[None]
    strict = (ci < ri)[None]
    dot = functools.partial(jnp.dot, preferred_element_type=F32)

    gates_hi, gates_lo = _split(gates_ref[...])
    width = heads * 3 * dh
    prow = lax.broadcasted_iota(jnp.int32, (LANES, width), 0)
    pcol = lax.broadcasted_iota(jnp.int32, (LANES, width), 1)
    pick = jnp.where(prow == ((pcol // dh) % 3) * nh + hg * heads + pcol // (3 * dh), 1.0, 0.0).astype(BF16)
    cols = dot(jnp.concatenate([gates_hi, gates_lo], axis=1),
               jnp.concatenate([pick, pick], axis=0))

    def per_head(j):
        return jnp.stack([cols[:, (3 * hl + j) * dh:(3 * hl + j + 1) * dh] for hl in range(heads)])

    beta, gcol, glast = per_head(0), per_head(1), per_head(2)
    gct_hi, gct_lo = _split(gct_ref[...])
    trow = lax.broadcasted_iota(jnp.int32, (heads * rows, nh), 0) // rows
    tcol = lax.broadcasted_iota(jnp.int32, (heads * rows, nh), 1)
    pick_t = jnp.where(tcol == hg * heads + trow, 1.0, 0.0).astype(BF16)
    g_j = dot(jnp.concatenate([pick_t, pick_t], axis=1),
              jnp.concatenate([gct_hi, gct_lo], axis=0)).reshape(heads, rows, rows)
    g_i = jnp.concatenate([gcol] * rep, axis=2)
    beta_i = jnp.concatenate([beta] * rep, axis=2)
    decay = jnp.where(lower, jnp.exp(jnp.where(lower, g_i - g_j, 0.0)), 0.0)

    def pairs(ref):
        return [ref[:, p * dh:(p + 1) * dh] for p in range(heads // 2)]

    def nt(a, b):
        return jnp.einsum("pid,pjd->pij", a.astype(BF16), b.astype(BF16), preferred_element_type=F32)

    def both(x):
        return jnp.stack([x[hl // 2] for hl in range(heads)])

    qp, kp = jnp.stack(pairs(q_ref)), jnp.stack(pairs(k_ref))
    q, k = both(qp), both(kp)
    kk, qk = both(nt(kp, kp)), both(nt(qp, kp))
    v = jnp.stack([v_ref[:, hl * dh:(hl + 1) * dh] for hl in range(heads)])

    low = jnp.where(strict, kk * beta_i * decay, 0.0)
    t0 = _unit_lower_inverse(low, ri, ci)
    rhs = jnp.concatenate([v * beta, k * (beta * jnp.exp(gcol))], axis=2)
    x = _bdot(t0, rhs)
    x = x + _bdot(t0, rhs - x - _bdot3(low, x))
    u, w = x[:, :, :dh], x[:, :, dh:]
    a_x = _bdot(jnp.where(lower, qk * decay, 0.0), x)
    q_eff = q * jnp.exp(gcol) - a_x[:, :, dh:]
    k_g = k * jnp.exp(glast - gcol)

    s = state[...]
    o = _bdot(q_eff, s) + a_x[:, :, :dh]
    v_new = u - _bdot(w, s)
    state[...] = s * jnp.exp(glast[:, 0:1, :]) + jnp.einsum(
        "hck,hcv->hkv", k_g.astype(BF16), v_new.astype(BF16), preferred_element_type=F32)
    for hl in range(heads):
        o_ref[:, hl * dh:(hl + 1) * dh] = o[hl]


def _gdn_chunk(q, k, v, gates, gct, *, heads=16):
    rows = GDN_BLOCK
    s, dv_total = v.shape
    nh = dv_total // HEAD_DIM
    assert nh == 2 * (q.shape[1] // HEAD_DIM) and heads % 2 == 0 and nh % heads == 0
    qk_spec = pl.BlockSpec((rows, heads // 2 * HEAD_DIM), lambda h, i: (i, h))
    v_spec = pl.BlockSpec((rows, heads * HEAD_DIM), lambda h, i: (i, h))
    return pl.pallas_call(
        functools.partial(_gdn_chunk_kernel, heads=heads),
        grid=(nh // heads, s // rows),
        in_specs=[qk_spec, qk_spec, v_spec,
                  pl.BlockSpec((rows, LANES), lambda h, i: (i, 0)),
                  pl.BlockSpec((nh, rows), lambda h, i: (0, i))],
        out_specs=v_spec,
        out_shape=jax.ShapeDtypeStruct((s, dv_total), F32),
        scratch_shapes=[pltpu.VMEM((heads, HEAD_DIM, HEAD_DIM), F32)],
        compiler_params=_params("parallel", "arbitrary"),
        name="gdn_chunk",
    )(q, k, v, gates, gct)


def _gdn_post_kernel(o_ref, z_ref, nw_ref, out_ref):
    nw = nw_ref[...]
    for hh in range(o_ref.shape[1] // HEAD_DIM):
        sl = slice(hh * HEAD_DIM, (hh + 1) * HEAD_DIM)
        o = o_ref[:, sl]
        ms = jnp.mean(o * o, axis=-1, keepdims=True)
        out_ref[:, sl] = ((o * lax.rsqrt(ms + NORM_EPS)) * nw * _silu(z_ref[:, sl].astype(F32))).astype(out_ref.dtype)


def _gdn_post(o, proj, z_col0, norm_w, *, ts=256):
    s, dv = o.shape
    zb = z_col0 // dv
    return pl.pallas_call(
        _gdn_post_kernel,
        grid=(s // ts,),
        in_specs=[pl.BlockSpec((ts, dv), lambda i: (i, 0)),
                  pl.BlockSpec((ts, dv), lambda i: (i, zb)),
                  pl.BlockSpec((1, HEAD_DIM), lambda i: (0, 0))],
        out_specs=pl.BlockSpec((ts, dv), lambda i: (i, 0)),
        out_shape=jax.ShapeDtypeStruct((s, dv), BF16),
        compiler_params=_params("parallel"),
        name="gdn_post",
    )(o, proj, norm_w.reshape(1, HEAD_DIM))


def _gated_deltanet(x, sc, sh, w_in, conv_w, layer, a_log, dt_bias, norm_w):
    nvh = a_log.shape[0]
    dv = nvh * HEAD_DIM
    conv_ch = conv_w.shape[2]
    dqk = (conv_ch - dv) // 2
    n_main = conv_ch + dv
    proj, ba = _mod_matmul(x, sc, sh, w_in[layer][:, :n_main], w_in[layer][:, n_main:], out_dtype=BF16,
                           name="gdn_in_proj")
    q = _gdn_conv(proj, conv_w, layer, 0, dqk, normalize=True, scale=HEAD_DIM ** -0.5)
    k = _gdn_conv(proj, conv_w, layer, dqk, dqk, normalize=True, scale=1.0)
    v = _gdn_conv(proj, conv_w, layer, 2 * dqk, dv, normalize=False, scale=1.0)
    gates = _gdn_gates(ba, a_log, dt_bias)
    o = _gdn_chunk(q, k, v, gates, gates[:, nvh:2 * nvh].T)
    return _gdn_post(o, proj, conv_ch, norm_w)


def _rope(x, c, sg):
    return x * c + pltpu.roll(x, HEAD_DIM // 2, 1) * sg


def _nsa_prep_kernel(p_ref, c_ref, s_ref, q_ref, kvc_ref, ks_ref, vs_ref, kw_ref, vw_ref, *, dq, dkv):
    c = c_ref[...]
    sg = s_ref[...]
    dh = HEAD_DIM

    def head(col0, hh, rope, mul):
        x = p_ref[:, col0 + hh * dh:col0 + (hh + 1) * dh]
        if rope:
            x = _rope(x, c, sg)
        return x if mul == 1.0 else x * mul

    for hh in range(dq // dh):
        q_ref[:, hh * dh:(hh + 1) * dh] = head(0, hh, True, dh ** -0.5 * LOG2E).astype(q_ref.dtype)
    for hh in range(dkv // dh):
        sl = slice(hh * dh, (hh + 1) * dh)
        kvc_ref[0, hh] = head(dq, hh, True, 1.0)
        kvc_ref[1, hh] = head(dq + dkv, hh, False, 1.0)
        ks_ref[:, sl] = head(dq + 2 * dkv, hh, True, 1.0).astype(ks_ref.dtype)
        vs_ref[:, sl] = head(dq + 3 * dkv, hh, False, 1.0).astype(vs_ref.dtype)
        kw_ref[:, sl] = head(dq + 4 * dkv, hh, True, 1.0).astype(kw_ref.dtype)
        vw_ref[:, sl] = head(dq + 5 * dkv, hh, False, 1.0).astype(vw_ref.dtype)


def _nsa_prep(proj, cos2, sin2, dq, dkv, *, ts=256):
    s, n = proj.shape
    hk = dkv // HEAD_DIM
    kv_spec = pl.BlockSpec((ts, dkv), lambda i: (i, 0))
    tab = pl.BlockSpec((ts, HEAD_DIM), lambda i: (i, 0))
    kv16 = jax.ShapeDtypeStruct((s, dkv), BF16)
    return pl.pallas_call(
        functools.partial(_nsa_prep_kernel, dq=dq, dkv=dkv),
        grid=(s // ts,),
        in_specs=[pl.BlockSpec((ts, n), lambda i: (i, 0)), tab, tab],
        out_specs=[pl.BlockSpec((ts, dq), lambda i: (i, 0)),
                   pl.BlockSpec((2, hk, ts, HEAD_DIM), lambda i: (0, 0, i, 0)),
                   kv_spec, kv_spec, kv_spec, kv_spec],
        out_shape=[jax.ShapeDtypeStruct((s, dq), BF16),
                   jax.ShapeDtypeStruct((2, hk, s, HEAD_DIM), F32),
                   kv16, kv16, kv16, kv16],
        compiler_params=_params("parallel"),
        name="nsa_prep",
    )(proj, cos2, sin2)


def _nsa_compress_kernel(x_ref, pe_ref, w1_ref, w2_ref, o_ref):
    x = x_ref[0, 0]
    n = x.shape[0]
    a = _dot(x + pe_ref[0, 0:1, :], w1_ref[0, 0])
    b = _dot(x + pe_ref[0, 1:2, :], w1_ref[0, 1])
    y = a + pltpu.roll(b, n - 1, 0)
    o_ref[0, 0] = _dot(_silu(y), w2_ref[0])


def _nsa_compress(x, pe, w1, w2):
    _, hk, ng, gw = x.shape
    half = CMP_LEN // 2
    pe2 = pe.reshape(2, 2, half * HEAD_DIM)
    w1r = w1.reshape(2, 2, half * HEAD_DIM, HEAD_DIM)
    return pl.pallas_call(
        _nsa_compress_kernel,
        grid=(2, hk),
        in_specs=[pl.BlockSpec((1, 1, ng, gw), lambda t, h: (t, h, 0, 0)),
                  pl.BlockSpec((1, 2, gw), lambda t, h: (t, 0, 0)),
                  pl.BlockSpec((1, 2, gw, HEAD_DIM), lambda t, h: (t, 0, 0, 0)),
                  pl.BlockSpec((1, HEAD_DIM, HEAD_DIM), lambda t, h: (t, 0, 0))],
        out_specs=pl.BlockSpec((1, 1, ng, HEAD_DIM), lambda t, h: (t, h, 0, 0)),
        out_shape=jax.ShapeDtypeStruct((2, hk, ng, HEAD_DIM), F32),
        compiler_params=_params("parallel", "parallel"),
        name="nsa_compress",
    )(x, pe2, w1r, w2)


def _nsa_cmp_select_kernel(q_ref, kc_ref, vc_ref, ov_ref, o_ref, sel_ref):
    tq = q_ref.shape[0]
    nc = kc_ref.shape[2]
    nbp = ov_ref.shape[1]
    t0 = pl.program_id(0) * tq
    tpos = t0 + lax.broadcasted_iota(jnp.int32, (tq, nc), 0)
    nidx = lax.broadcasted_iota(jnp.int32, (tq, nc), 1)
    valid = (nidx * CMP_STRIDE + (CMP_LEN - 1)) <= tpos
    kc = kc_ref[0, 0].astype(BF16)
    vc = vc_ref[0, 0].astype(BF16)
    psum = jnp.zeros((tq, nc), F32)
    q4 = jnp.concatenate([q_ref[:, g * HEAD_DIM:(g + 1) * HEAD_DIM] for g in range(NSA_GROUP)], axis=0)
    s_all = _dot_nt(q4, kc)
    probs = []
    for g in range(NSA_GROUP):
        s = jnp.where(valid, s_all[g * tq:(g + 1) * tq, :], NEG_BIG)
        m = jnp.max(s, axis=-1, keepdims=True)
        p = jnp.where(valid, jnp.exp2(s - m), 0.0)
        den = jnp.sum(p, axis=-1, keepdims=True)
        p = p / jnp.where(den > 0.0, den, 1.0)
        probs.append(p.astype(BF16))
        psum = psum + p
    o_all = jnp.dot(jnp.concatenate(probs, axis=0), vc, preferred_element_type=F32)
    for g in range(NSA_GROUP):
        o_ref[:, g * HEAD_DIM:(g + 1) * HEAD_DIM] = o_all[g * tq:(g + 1) * tq, :].astype(o_ref.dtype)
    imp = _dot_sel(psum, ov_ref[...])
    imp_t = imp.T
    j = lax.broadcasted_iota(jnp.int32, (nbp, tq), 0)
    cur = (t0 + lax.broadcasted_iota(jnp.int32, (nbp, tq), 1)) // SLC_LEN
    forced = (j == 0) | (j == cur) | (j == cur - 1)
    work = jnp.where(j <= cur, jnp.where(forced, jnp.inf, imp_t), -jnp.inf)
    jf = j.astype(F32)
    for _ in range(SLC_TOP):
        m = jnp.max(work, axis=0, keepdims=True)
        first = jnp.min(jnp.where(work == m, jf, float(nbp)), axis=0, keepdims=True)
        work = jnp.where(jf == first, -jnp.inf, work)
    sel_ref[0] = jnp.where((j <= cur) & (work == -jnp.inf), 0.0, MASK_NEG).T.astype(sel_ref.dtype)


def _nsa_cmp_select(q, cmp, ov, *, tq=512):
    s, dq = q.shape
    _, hk, nc, _ = cmp.shape
    nbp = ov.shape[1]
    gw = NSA_GROUP * HEAD_DIM
    return pl.pallas_call(
        _nsa_cmp_select_kernel,
        grid=(s // tq, hk),
        in_specs=[pl.BlockSpec((tq, gw), lambda i, h: (i, h)),
                  pl.BlockSpec((1, 1, nc, HEAD_DIM), lambda i, h: (0, h, 0, 0)),
                  pl.BlockSpec((1, 1, nc, HEAD_DIM), lambda i, h: (1, h, 0, 0)),
                  pl.BlockSpec((nc, nbp), lambda i, h: (0, 0))],
        out_specs=[pl.BlockSpec((tq, gw), lambda i, h: (i, h)),
                   pl.BlockSpec((1, tq, nbp), lambda i, h: (h, i, 0))],
        out_shape=[jax.ShapeDtypeStruct((s, dq), BF16),
                   jax.ShapeDtypeStruct((hk, s, nbp), BF16)],
        compiler_params=_params("parallel", "parallel"),
        name="nsa_cmp_select",
    )(q, cmp, cmp, ov)


ATTN_ROWS = 64


def _masked_attn_kernel(*refs, tq, nk, sub, selected):
    qi_ref, kj_ref, first_ref, last_ref, q_ref = refs[:5]
    k_refs, v_refs = refs[5:5 + nk], refs[5 + nk:5 + 2 * nk]
    rest = refs[5 + 2 * nk:]
    if selected:
        sel_ref, expand_ref = rest[:2]
        rest = rest[2:]
    o_ref, q4, bias, s_scr, p_scr, a_scr, m_scr, acc = rest
    t = pl.program_id(1)
    qi = qi_ref[t]
    kj = kj_ref[t]
    dh = HEAD_DIM
    tk = nk * k_refs[0].shape[0]

    @pl.when(first_ref[t] == 1)
    def _():
        for g in range(NSA_GROUP):
            q4[g * tq:(g + 1) * tq, :] = q_ref[:, g * dh:(g + 1) * dh]
        m_scr[...] = jnp.full_like(m_scr, NEG_BIG)
        acc[...] = jnp.zeros_like(acc)

    k0 = kj * k_refs[0].shape[0]

    def causal():
        tpos = qi * tq + lax.broadcasted_iota(jnp.int32, (tq, tk), 0)
        kpos = k0 + lax.broadcasted_iota(jnp.int32, (tq, tk), 1)
        return tpos, kpos

    if selected:
        bias[...] = jnp.dot(sel_ref[0], expand_ref[...], preferred_element_type=F32).astype(BF16)

        @pl.when(k0 + tk - 1 > qi * tq)
        def _():
            tpos, kpos = causal()
            bias[...] = bias[...] + jnp.where(kpos <= tpos, 0.0, MASK_NEG).astype(BF16)
    else:
        tpos, kpos = causal()
        bias[...] = jnp.where((kpos <= tpos) & (kpos > tpos - WINDOW) & (kpos >= 0),
                              0.0, MASK_NEG).astype(BF16)

    tiles = [(r, u * sub) for r in range(nk) for u in range(k_refs[0].shape[0] // sub)]
    for n, (r, off) in enumerate(tiles):
        s_scr[:, n * sub:(n + 1) * sub] = lax.dot_general(
            q4[...], k_refs[r][off:off + sub, :], (((1,), (1,)), ((), ())),
            preferred_element_type=F32).astype(BF16)
    rb = ATTN_ROWS
    for u, (r, off) in enumerate(tiles):
        cols = slice(u * sub, (u + 1) * sub)
        for r0 in range(0, NSA_GROUP * tq, rb):
            rows = slice(r0, r0 + rb)
            brow = r0 % tq
            s = s_scr[rows, cols] + bias[brow:brow + rb, cols]
            m_prev = m_scr[rows, :]
            m_new = jnp.maximum(m_prev, jnp.max(s, axis=-1, keepdims=True).astype(F32))
            a_scr[u, rows, :] = jnp.exp2(m_prev - m_new)
            shift = jnp.concatenate([m_new.astype(BF16)] * (sub // LANES), axis=1)
            p_scr[rows, cols] = jnp.exp2(s - shift)
            m_scr[rows, :] = m_new
        v_one = jnp.concatenate([v_refs[r][off:off + sub, :], jnp.ones((sub, dh), BF16)], axis=1)
        alpha = jnp.concatenate([a_scr[u]] * 2, axis=1)
        acc[...] = alpha * acc[...] + jnp.dot(p_scr[:, cols], v_one, preferred_element_type=F32)

    @pl.when(last_ref[t] == 1)
    def _():
        for g in range(NSA_GROUP):
            rows = slice(g * tq, (g + 1) * tq)
            o_ref[:, g * dh:(g + 1) * dh] = (acc[rows, :dh] / acc[rows, dh:]).astype(o_ref.dtype)


def _masked_attn(q, k, v, sel, expand, *, tq, tk, sub):
    s, dq = q.shape
    hk = k.shape[1] // HEAD_DIM
    gw = NSA_GROUP * HEAD_DIM
    selected = sel is not None
    if selected:
        nk = 1
        pairs = []
        for i in range(s // tq):
            hi = (i * tq + tq - 1) // tk
            pairs += [(i, j, int(j == 0), int(j == hi)) for j in range(hi + 1)]
    else:
        assert tq % tk == 0
        nback = -(-(WINDOW - 1) // tk)
        nk = nback + tq // tk
        pairs = [(i, i * (tq // tk) - nback, 1, 1) for i in range(s // tq)]
    sched = [jnp.asarray([p[c] for p in pairs], jnp.int32) for c in range(4)]

    def kv_spec(r):
        return pl.BlockSpec((tk, HEAD_DIM), lambda h, t, qi, kj, fi, la: (jnp.maximum(kj[t] + r, 0), h))

    in_specs = [pl.BlockSpec((tq, gw), lambda h, t, qi, kj, fi, la: (qi[t], h))]
    in_specs += [kv_spec(r) for r in range(nk)] * 2
    args = [q] + [k] * nk + [v] * nk
    if selected:
        in_specs.append(pl.BlockSpec((1, tq, sel.shape[2]), lambda h, t, qi, kj, fi, la: (h, qi[t], 0)))
        in_specs.append(pl.BlockSpec((expand.shape[0], tk), lambda h, t, qi, kj, fi, la: (0, kj[t])))
        args += [sel, expand]
    rows = NSA_GROUP * tq
    tk = nk * tk
    return pl.pallas_call(
        functools.partial(_masked_attn_kernel, tq=tq, nk=nk, sub=sub, selected=selected),
        grid_spec=pltpu.PrefetchScalarGridSpec(
            num_scalar_prefetch=4,
            grid=(hk, len(pairs)),
            in_specs=in_specs,
            out_specs=pl.BlockSpec((tq, gw), lambda h, t, qi, kj, fi, la: (qi[t], h)),
            scratch_shapes=[pltpu.VMEM((rows, HEAD_DIM), BF16),
                            pltpu.VMEM((tq, tk), BF16),
                            pltpu.VMEM((rows, tk), BF16),
                            pltpu.VMEM((rows, tk), BF16),
                            pltpu.VMEM((tk // sub, rows, LANES), F32),
                            pltpu.VMEM((rows, LANES), F32),
                            pltpu.VMEM((rows, 2 * HEAD_DIM), F32)]),
        out_shape=jax.ShapeDtypeStruct((s, dq), BF16),
        compiler_params=_params("parallel", "arbitrary"),
        name="nsa_selected_attn" if selected else "nsa_window_attn",
    )(*sched, *args)


def _nsa_combine_kernel(oc_ref, os_ref, ow_ref, g_ref, out_ref):
    gates = jax.nn.sigmoid(g_ref[...])
    ng, dq = g_ref.shape[1], oc_ref.shape[1]
    row = lax.broadcasted_iota(jnp.int32, (ng, dq), 0)
    head3 = (lax.broadcasted_iota(jnp.int32, (ng, dq), 1) // HEAD_DIM) * N_BRANCH
    acc = None
    for b, ref in enumerate((oc_ref, os_ref, ow_ref)):
        gb = _dot_sel(gates, jnp.where(row == head3 + b, 1.0, 0.0))
        term = gb * ref[...]
        acc = term if acc is None else acc + term
    out_ref[...] = acc.astype(out_ref.dtype)


def _nsa_combine(oc, osel, ow, gates, *, ts=256):
    s, dq = oc.shape
    spec = pl.BlockSpec((ts, dq), lambda i: (i, 0))
    return pl.pallas_call(
        _nsa_combine_kernel,
        grid=(s // ts,),
        in_specs=[spec, spec, spec, pl.BlockSpec((ts, gates.shape[1]), lambda i: (i, 0))],
        out_specs=spec,
        out_shape=jax.ShapeDtypeStruct((s, dq), BF16),
        compiler_params=_params("parallel"),
        name="nsa_combine",
    )(oc, osel, ow, gates)


def _nsa_attention(x, sc, sh, cos2, sin2, w_in, layer, cmp_pe, cmp_w1, cmp_w2):
    s, d = x.shape
    dq = d
    hk = d // HEAD_DIM // NSA_GROUP
    dkv = hk * HEAD_DIM
    n_main = dq + 6 * dkv
    ngate = w_in.shape[2] - n_main
    w_gate = jnp.pad(w_in[layer][:, n_main:], ((0, 0), (0, LANES - ngate)))
    proj, gates = _mod_matmul(x, sc, sh, w_in[layer][:, :n_main], w_gate, name="nsa_in_proj")
    q, kvc, ks, vs, kw, vw = _nsa_prep(proj, cos2, sin2, dq, dkv)
    groups = kvc.reshape(2, hk, s // CMP_STRIDE, CMP_STRIDE * HEAD_DIM)
    cmp = _nsa_compress(groups, cmp_pe, cmp_w1, cmp_w2)

    nc = s // CMP_STRIDE
    n_slc = s // SLC_LEN
    nbp = -(-n_slc // LANES) * LANES
    c_start = CMP_STRIDE * jnp.arange(nc)
    s_start = SLC_LEN * jnp.arange(nbp)
    overlap = jnp.clip(jnp.minimum(c_start[:, None] + CMP_LEN, s_start[None, :] + SLC_LEN)
                       - jnp.maximum(c_start[:, None], s_start[None, :]), 0, None).astype(F32) / CMP_LEN
    o_c, sel = _nsa_cmp_select(q, cmp, overlap.astype(BF16))
    expand = (jnp.arange(nbp)[:, None] == jnp.arange(s)[None, :] // SLC_LEN).astype(BF16)
    o_s = _masked_attn(q, ks, vs, sel, expand, tq=512, tk=1024, sub=512)
    o_w = _masked_attn(q, kw, vw, None, None, tq=256, tk=256, sub=256)
    return _nsa_combine(o_c, o_s, o_w, gates)


def kernel(x, c, positions, mod_w, mod_b, ln_g, ln_b, ffn_w_gu, ffn_w_down, gdn_w_in, gdn_conv_w,
           gdn_a_log, gdn_dt_bias, gdn_norm_w, gdn_w_out, nsa_w_in, nsa_cmp_pe, nsa_cmp_w1,
           nsa_cmp_w2, nsa_w_out):
    bsz, s, d = x.shape
    assert bsz == 1
    depth = mod_w.shape[0]
    alpha = (2 * depth) ** 0.25

    inv = ROPE_THETA ** (-jnp.arange(0, HEAD_DIM, 2, dtype=F32) / HEAD_DIM)
    ang = positions[0].astype(F32)[:, None] * inv
    cos, sin = jnp.cos(ang), jnp.sin(ang)
    cos2 = jnp.concatenate([cos, cos], axis=-1)
    sin2 = jnp.concatenate([-sin, sin], axis=-1)

    mod = _modulation(c, mod_w, mod_b)
    gdn_w_out, nsa_w_out, ffn_w_down = [w.astype(BF16) for w in (gdn_w_out, nsa_w_out, ffn_w_down)]
    xs = x[0]
    for i in range(depth):
        sh1, sc1, ga1, sh2, sc2, ga2 = [mod[i, :, r * d:(r + 1) * d] for r in range(6)]
        j = i // N_MIXERS
        if i % N_MIXERS == 0:
            y = _gated_deltanet(xs, sc1, sh1, gdn_w_in, gdn_conv_w, j, gdn_a_log[j],
                                gdn_dt_bias[j], gdn_norm_w[j])
            w_out = gdn_w_out
        else:
            y = _nsa_attention(xs, sc1, sh1, cos2, sin2, nsa_w_in, j, nsa_cmp_pe[j],
                               nsa_cmp_w1[j], nsa_cmp_w2[j])
            w_out = nsa_w_out
        xs = _matmul_ln(y, w_out, j, xs, ga1, ln_g[i, 0:1], ln_b[i, 0:1], alpha=alpha)
        a = _swiglu_up(xs, sc2, sh2, ffn_w_gu, i)
        xs = _matmul_ln(a, ffn_w_down, i, xs, ga2, ln_g[i, 1:2], ln_b[i, 1:2], alpha=alpha)
    return xs[None]
```

```python
import functools

import jax
import jax.numpy as jnp
from jax import lax
from jax.experimental import pallas as pl
from jax.experimental.pallas import tpu as pltpu

F32 = jnp.float32
BF16 = jnp.bfloat16
HIGHEST = lax.Precision.HIGHEST

HEAD_DIM = 128
N_MIXERS = 2
GDN_CONV = 4
GDN_BLOCK = 128
NSA_GROUP = 4
N_BRANCH = 3
CMP_LEN = 32
CMP_STRIDE = 16
SLC_LEN = 64
SLC_TOP = 16
WINDOW = 512
ROPE_THETA = 10000.0
LN_EPS = 1e-5
NORM_EPS = 1e-6

LOG2E = 1.4426950408889634
MATMUL_ROWS = 1024
LANES = 128
CONV_HALO_ROWS = 16
VMEM_LIMIT_BYTES = 56 * 1024 * 1024
NEG_BIG = -1e30
MASK_NEG = -2.0 ** 100


def _params(*sem):
    return pltpu.CompilerParams(dimension_semantics=sem, vmem_limit_bytes=VMEM_LIMIT_BYTES)


def _silu(v):
    return v * jax.nn.sigmoid(v)


def _dot(a, b):
    return jnp.dot(a.astype(BF16), b.astype(BF16), preferred_element_type=F32)


def _dot_nt(a, b):
    return lax.dot_general(a.astype(BF16), b.astype(BF16), (((1,), (1,)), ((), ())),
                           preferred_element_type=F32)


def _split(a):
    hi = a.astype(BF16)
    lo = (a - hi.astype(F32)).astype(BF16)
    return hi, lo


def _dot3(a, b):
    ah, al = _split(a)
    bh, bl = _split(b)
    d = functools.partial(jnp.dot, preferred_element_type=F32)
    return d(ah, bh) + (d(ah, bl) + d(al, bh))


def _dot_sel(a, onehot):
    ah, al = _split(a)
    e = onehot.astype(BF16)
    d = functools.partial(jnp.dot, preferred_element_type=F32)
    return d(ah, e) + d(al, e)


def _sel_dot(onehot, b):
    bh, bl = _split(b)
    e = onehot.astype(BF16)
    d = functools.partial(jnp.dot, preferred_element_type=F32)
    return d(e, bh) + d(e, bl)


def _mod_kernel(c_ref, w_ref, b_ref, o_ref):
    cond = _silu(c_ref[...])
    o_ref[0] = _dot3(cond, w_ref[0]) + b_ref[0]


def _modulation(c, mod_w, mod_b):
    depth, d, n = mod_w.shape
    tn = 1536
    c8 = jnp.broadcast_to(c[:1], (8, d))
    out = pl.pallas_call(
        _mod_kernel,
        grid=(depth, n // tn),
        in_specs=[pl.BlockSpec((8, d), lambda l, j: (0, 0)),
                  pl.BlockSpec((1, d, tn), lambda l, j: (l, 0, j)),
                  pl.BlockSpec((1, 1, tn), lambda l, j: (l, 0, j))],
        out_specs=pl.BlockSpec((1, 8, tn), lambda l, j: (l, 0, j)),
        out_shape=jax.ShapeDtypeStruct((depth, 8, n), F32),
        compiler_params=_params("parallel", "parallel"),
        name="modulation",
    )(c8, mod_w, mod_b.reshape(depth, 1, n))
    return out[:, 0:1, :]


def _mod_matmul_kernel(x_ref, sc_ref, sh_ref, w_ref, wg_ref, o_ref, og_ref, h_scr):
    @pl.when(pl.program_id(1) == 0)
    def _():
        h_scr[...] = (x_ref[...] * (1.0 + sc_ref[...]) + sh_ref[...]).astype(BF16)
        og_ref[...] = jnp.dot(h_scr[...], wg_ref[...].astype(BF16), preferred_element_type=F32)

    o_ref[...] = jnp.dot(h_scr[...], w_ref[...].astype(BF16),
                         preferred_element_type=F32).astype(o_ref.dtype)


def _mod_matmul(x, sc, sh, w, w_gate, *, out_dtype=F32, tm=MATMUL_ROWS, tn=1024, name="mod_matmul"):
    m, k = x.shape
    n = w.shape[1]
    ng = w_gate.shape[1]
    tm = min(tm, m)
    tn = min(tn, n)
    assert n % tn == 0 and m % tm == 0
    return pl.pallas_call(
        _mod_matmul_kernel,
        grid=(m // tm, n // tn),
        in_specs=[pl.BlockSpec((tm, k), lambda i, j: (i, 0)),
                  pl.BlockSpec((1, k), lambda i, j: (0, 0)),
                  pl.BlockSpec((1, k), lambda i, j: (0, 0)),
                  pl.BlockSpec((k, tn), lambda i, j: (0, j)),
                  pl.BlockSpec((k, ng), lambda i, j: (0, 0))],
        out_specs=[pl.BlockSpec((tm, tn), lambda i, j: (i, j)),
                   pl.BlockSpec((tm, ng), lambda i, j: (i, 0))],
        out_shape=[jax.ShapeDtypeStruct((m, n), out_dtype), jax.ShapeDtypeStruct((m, ng), F32)],
        scratch_shapes=[pltpu.VMEM((tm, k), BF16)],
        compiler_params=_params("parallel", "arbitrary"),
        name=name,
    )(x, sc, sh, w, w_gate)


def _swiglu_up_kernel(x_ref, sc_ref, sh_ref, wg_ref, wu_ref, o_ref, h_scr):
    @pl.when(pl.program_id(1) == 0)
    def _():
        h_scr[...] = (x_ref[...] * (1.0 + sc_ref[...]) + sh_ref[...]).astype(BF16)

    h = h_scr[...]
    g = jnp.dot(h, wg_ref[...].astype(BF16), preferred_element_type=F32)
    u = jnp.dot(h, wu_ref[...].astype(BF16), preferred_element_type=F32)
    o_ref[...] = (_silu(g) * u).astype(o_ref.dtype)


def _swiglu_up(x, sc, sh, w_gu, layer, *, tm=MATMUL_ROWS, tn=512):
    m, k = x.shape
    tm = min(tm, m)
    dff = w_gu.shape[2] // 2
    nb = dff // tn
    assert dff % tn == 0 and m % tm == 0
    return pl.pallas_call(
        _swiglu_up_kernel,
        grid=(m // tm, nb),
        in_specs=[pl.BlockSpec((tm, k), lambda i, j: (i, 0)),
                  pl.BlockSpec((1, k), lambda i, j: (0, 0)),
                  pl.BlockSpec((1, k), lambda i, j: (0, 0)),
                  pl.BlockSpec((None, k, tn), lambda i, j: (layer, 0, j)),
                  pl.BlockSpec((None, k, tn), lambda i, j: (layer, 0, j + nb))],
        out_specs=pl.BlockSpec((tm, tn), lambda i, j: (i, j)),
        out_shape=jax.ShapeDtypeStruct((m, dff), BF16),
        scratch_shapes=[pltpu.VMEM((tm, k), BF16)],
        compiler_params=_params("parallel", "arbitrary"),
        name="swiglu_up",
    )(x, sc, sh, w_gu, w_gu)


LN_ROWS = 256
LN_MAX_TK = 1536


def _matmul_ln_kernel(a_ref, w_ref, x_ref, ga_ref, g_ref, b_ref, o_ref, *, alpha):
    kk = pl.program_id(1)
    part = jnp.dot(a_ref[...].astype(BF16), w_ref[...].astype(BF16), preferred_element_type=F32)

    @pl.when(kk == 0)
    def _():
        o_ref[...] = part

    @pl.when(kk > 0)
    def _():
        o_ref[...] += part

    @pl.when(kk == pl.num_programs(1) - 1)
    def _():
        for r0 in range(0, o_ref.shape[0], LN_ROWS):
            rows = slice(r0, r0 + LN_ROWS)
            r = alpha * x_ref[rows, :] + (1.0 + ga_ref[...]) * o_ref[rows, :]
            mu = jnp.mean(r, axis=-1, keepdims=True)
            d = r - mu
            var = jnp.mean(d * d, axis=-1, keepdims=True)
            o_ref[rows, :] = d * lax.rsqrt(var + LN_EPS) * g_ref[...] + b_ref[...]


def _matmul_ln(a, w, layer, x, ga, g, b, *, alpha, tm=512):
    m, k = a.shape
    n = w.shape[2]
    tm = min(tm, m)
    tk = next(t for t in range(min(k, LN_MAX_TK), 0, -LANES) if k % t == 0)
    assert m % tm == 0 and k % tk == 0 and tm % LN_ROWS == 0
    return pl.pallas_call(
        functools.partial(_matmul_ln_kernel, alpha=alpha),
        grid=(m // tm, k // tk),
        in_specs=[pl.BlockSpec((tm, tk), lambda i, kk: (i, kk)),
                  pl.BlockSpec((None, tk, n), lambda i, kk: (layer, kk, 0)),
                  pl.BlockSpec((tm, n), lambda i, kk: (i, 0)),
                  pl.BlockSpec((1, n), lambda i, kk: (0, 0)),
                  pl.BlockSpec((1, n), lambda i, kk: (0, 0)),
                  pl.BlockSpec((1, n), lambda i, kk: (0, 0))],
        out_specs=pl.BlockSpec((tm, n), lambda i, kk: (i, 0)),
        out_shape=jax.ShapeDtypeStruct((m, n), F32),
        compiler_params=_params("parallel", "arbitrary"),
        name="matmul_ln",
    )(a, w, x, ga, g, b)


def _gdn_conv_kernel(xp_ref, xc_ref, w_ref, o_ref, buf, *, normalize, scale):
    ts, tc = xc_ref.shape
    halo = xp_ref.shape[0]
    buf[0:halo, :] = jnp.where(pl.program_id(0) > 0, xp_ref[...].astype(F32), 0.0)
    buf[halo:, :] = xc_ref[...].astype(F32)
    w = w_ref[...]
    first = halo - (GDN_CONV - 1)
    y = w[0:1, :] * buf[first:first + ts, :]
    for i in range(1, GDN_CONV):
        y = y + w[i:i + 1, :] * buf[first + i:first + i + ts, :]
    y = _silu(y)
    if normalize:
        for hh in range(tc // HEAD_DIM):
            seg = y[:, hh * HEAD_DIM:(hh + 1) * HEAD_DIM]
            ss = jnp.sum(seg * seg, axis=-1, keepdims=True)
            o_ref[:, hh * HEAD_DIM:(hh + 1) * HEAD_DIM] = seg * (lax.rsqrt(ss + NORM_EPS) * scale)
    else:
        o_ref[...] = y


def _gdn_conv(proj, conv_w, layer, col0, ncols, *, normalize, scale, ts=512, tc=512):
    s = proj.shape[0]
    cb0 = col0 // tc
    halo = CONV_HALO_ROWS
    hb = ts // halo
    return pl.pallas_call(
        functools.partial(_gdn_conv_kernel, normalize=normalize, scale=scale),
        grid=(s // ts, ncols // tc),
        in_specs=[pl.BlockSpec((halo, tc), lambda i, j: (jnp.maximum(i * hb - 1, 0), j + cb0)),
                  pl.BlockSpec((ts, tc), lambda i, j: (i, j + cb0)),
                  pl.BlockSpec((None, GDN_CONV, tc), lambda i, j: (layer, 0, j + cb0))],
        out_specs=pl.BlockSpec((ts, tc), lambda i, j: (i, j)),
        out_shape=jax.ShapeDtypeStruct((s, ncols), F32),
        scratch_shapes=[pltpu.VMEM((ts + halo, tc), F32)],
        compiler_params=_params("parallel", "parallel"),
        name="gdn_conv",
    )(proj, proj, conv_w)


def _gdn_gates_kernel(ba_ref, alog_ref, dtb_ref, out_ref):
    ts = ba_ref.shape[0]
    nh = ba_ref.shape[1] // 2
    x = ba_ref[...]
    beta = jax.nn.sigmoid(x[:, :nh])
    z = x[:, nh:] + dtb_ref[...]
    softplus = jnp.maximum(z, 0.0) + jnp.log1p(jnp.exp(-jnp.abs(z)))
    g = -jnp.exp(alog_ref[...]) * softplus
    ri = lax.broadcasted_iota(jnp.int32, (ts, ts), 0)
    ci = lax.broadcasted_iota(jnp.int32, (ts, ts), 1)
    same = (ri // GDN_BLOCK) == (ci // GDN_BLOCK)
    gc = _sel_dot(jnp.where(same & (ci <= ri), 1.0, 0.0), g)
    gl = _sel_dot(jnp.where(same, 1.0, 0.0), g)
    out_ref[...] = jnp.concatenate([beta, gc, gl, jnp.zeros((ts, LANES - 3 * nh), F32)], axis=1)


def _gdn_gates(ba, a_log, dt_bias, *, ts=512):
    s, two_h = ba.shape
    nh = two_h // 2
    assert 3 * nh <= LANES
    return pl.pallas_call(
        _gdn_gates_kernel,
        grid=(s // ts,),
        in_specs=[pl.BlockSpec((ts, two_h), lambda i: (i, 0)),
                  pl.BlockSpec((1, nh), lambda i: (0, 0)),
                  pl.BlockSpec((1, nh), lambda i: (0, 0))],
        out_specs=pl.BlockSpec((ts, LANES), lambda i: (i, 0)),
        out_shape=jax.ShapeDtypeStruct((s, LANES), F32),
        compiler_params=_params("parallel"),
        name="gdn_gates",
    )(ba, a_log.reshape(1, nh), dt_bias.reshape(1, nh))


def _bdot(a, b):
    return jnp.einsum("hij,hjk->hik", a.astype(BF16), b.astype(BF16), preferred_element_type=F32)


def _bdot3(a, b):
    ah, al = _split(a)
    bh, bl = _split(b)
    return jnp.einsum("hij,hjk->hik", jnp.concatenate([ah, ah, al], axis=2),
                      jnp.concatenate([bh, bl, bh], axis=1), preferred_element_type=F32)


def _unit_lower_inverse(low, ri, ci):
    n = low.shape[-1]

    def blk(b):
        return ((ri // b) == (ci // b))[None]

    eye = jnp.where(ri == ci, 1.0, 0.0)[None]
    n1 = jnp.where(blk(16), -low, 0.0)
    n2 = _bdot(n1, n1)
    n4 = _bdot(n2, n2)
    n8 = _bdot(n4, n4)
    t = eye + n1
    t = t + _bdot(n2, t)
    t = t + _bdot(n4, t)
    t = t + _bdot(n8, t)
    b = 32
    while b <= n:
        off = jnp.where(blk(b) & jnp.logical_not(blk(b // 2)), low, 0.0)
        t = t - _bdot(t, _bdot(off, t))
        b *= 2
    return t


def _gdn_chunk_kernel(q_ref, k_ref, v_ref, gates_ref, gct_ref, o_ref, state, *, heads):
    hg = pl.program_id(0)
    rows = q_ref.shape[0]
    nh = gct_ref.shape[0]
    dh = HEAD_DIM
    rep = rows // dh

    @pl.when(pl.program_id(1) == 0)
    def _():
        state[...] = jnp.zeros_like(state)

    ri = lax.broadcasted_iota(jnp.int32, (rows, rows), 0)
    ci = lax.broadcasted_iota(jnp.int32, (rows, rows), 1)
    lower = (ci <= ri)[None]
    strict = (ci < ri)[None]
    dot = functools.partial(jnp.dot, preferred_element_type=F32)

    gates_hi, gates_lo = _split(gates_ref[...])
    width = heads * 3 * dh
    prow = lax.broadcasted_iota(jnp.int32, (LANES, width), 0)
    pcol = lax.broadcasted_iota(jnp.int32, (LANES, width), 1)
    pick = jnp.where(prow == ((pcol // dh) % 3) * nh + hg * heads + pcol // (3 * dh), 1.0, 0.0).astype(BF16)
    cols = dot(jnp.concatenate([gates_hi, gates_lo], axis=1),
               jnp.concatenate([pick, pick], axis=0))

    def per_head(j):
        return jnp.stack([cols[:, (3 * hl + j) * dh:(3 * hl + j + 1) * dh] for hl in range(heads)])

    beta, gcol, glast = per_head(0), per_head(1), per_head(2)
    gct_hi, gct_lo = _split(gct_ref[...])
    trow = lax.broadcasted_iota(jnp.int32, (heads * rows, nh), 0) // rows
    tcol = lax.broadcasted_iota(jnp.int32, (heads * rows, nh), 1)
    pick_t = jnp.where(tcol == hg * heads + trow, 1.0, 0.0).astype(BF16)
    g_j = dot(jnp.concatenate([pick_t, pick_t], axis=1),
              jnp.concatenate([gct_hi, gct_lo], axis=0)).reshape(heads, rows, rows)
    g_i = jnp.concatenate([gcol] * rep, axis=2)
    beta_i = jnp.concatenate([beta] * rep, axis=2)
    decay = jnp.where(lower, jnp.exp(jnp.where(lower, g_i - g_j, 0.0)), 0.0)

    def pairs(ref):
        return [ref[:, p * dh:(p + 1) * dh] for p in range(heads // 2)]

    def nt(a, b):
        return jnp.einsum("pid,pjd->pij", a.astype(BF16), b.astype(BF16), preferred_element_type=F32)

    def both(x):
        return jnp.stack([x[hl // 2] for hl in range(heads)])

    qp, kp = jnp.stack(pairs(q_ref)), jnp.stack(pairs(k_ref))
    q, k = both(qp), both(kp)
    kk, qk = both(nt(kp, kp)), both(nt(qp, kp))
    v = jnp.stack([v_ref[:, hl * dh:(hl + 1) * dh] for hl in range(heads)])

    low = jnp.where(strict, kk * beta_i * decay, 0.0)
    t0 = _unit_lower_inverse(low, ri, ci)
    rhs = jnp.concatenate([v * beta, k * (beta * jnp.exp(gcol))], axis=2)
    x = _bdot(t0, rhs)
    x = x + _bdot(t0, rhs - x - _bdot3(low, x))
    u, w = x[:, :, :dh], x[:, :, dh:]
    a_x = _bdot(jnp.where(lower, qk * decay, 0.0), x)
    q_eff = q * jnp.exp(gcol) - a_x[:, :, dh:]
    k_g = k * jnp.exp(glast - gcol)

    s = state[...]
    o = _bdot(q_eff, s) + a_x[:, :, :dh]
    v_new = u - _bdot(w, s)
    state[...] = s * jnp.exp(glast[:, 0:1, :]) + jnp.einsum(
        "hck,hcv->hkv", k_g.astype(BF16), v_new.astype(BF16), preferred_element_type=F32)
    for hl in range(heads):
        o_ref[:, hl * dh:(hl + 1) * dh] = o[hl]


def _gdn_chunk(q, k, v, gates, gct, *, heads=16):
    rows = GDN_BLOCK
    s, dv_total = v.shape
    nh = dv_total // HEAD_DIM
    assert nh == 2 * (q.shape[1] // HEAD_DIM) and heads % 2 == 0 and nh % heads == 0
    qk_spec = pl.BlockSpec((rows, heads // 2 * HEAD_DIM), lambda h, i: (i, h))
    v_spec = pl.BlockSpec((rows, heads * HEAD_DIM), lambda h, i: (i, h))
    return pl.pallas_call(
        functools.partial(_gdn_chunk_kernel, heads=heads),
        grid=(nh // heads, s // rows),
        in_specs=[qk_spec, qk_spec, v_spec,
                  pl.BlockSpec((rows, LANES), lambda h, i: (i, 0)),
                  pl.BlockSpec((nh, rows), lambda h, i: (0, i))],
        out_specs=v_spec,
        out_shape=jax.ShapeDtypeStruct((s, dv_total), F32),
        scratch_shapes=[pltpu.VMEM((heads, HEAD_DIM, HEAD_DIM), F32)],
        compiler_params=_params("parallel", "arbitrary"),
        name="gdn_chunk",
    )(q, k, v, gates, gct)


def _gdn_post_kernel(o_ref, z_ref, nw_ref, out_ref):
    nw = nw_ref[...]
    for hh in range(o_ref.shape[1] // HEAD_DIM):
        sl = slice(hh * HEAD_DIM, (hh + 1) * HEAD_DIM)
        o = o_ref[:, sl]
        ms = jnp.mean(o * o, axis=-1, keepdims=True)
        out_ref[:, sl] = ((o * lax.rsqrt(ms + NORM_EPS)) * nw * _silu(z_ref[:, sl].astype(F32))).astype(out_ref.dtype)


def _gdn_post(o, proj, z_col0, norm_w, *, ts=256):
    s, dv = o.shape
    zb = z_col0 // dv
    return pl.pallas_call(
        _gdn_post_kernel,
        grid=(s // ts,),
        in_specs=[pl.BlockSpec((ts, dv), lambda i: (i, 0)),
                  pl.BlockSpec((ts, dv), lambda i: (i, zb)),
                  pl.BlockSpec((1, HEAD_DIM), lambda i: (0, 0))],
        out_specs=pl.BlockSpec((ts, dv), lambda i: (i, 0)),
        out_shape=jax.ShapeDtypeStruct((s, dv), BF16),
        compiler_params=_params("parallel"),
        name="gdn_post",
    )(o, proj, norm_w.reshape(1, HEAD_DIM))


def _gated_deltanet(x, sc, sh, w_in, conv_w, layer, a_log, dt_bias, norm_w):
    nvh = a_log.shape[0]
    dv = nvh * HEAD_DIM
    conv_ch = conv_w.shape[2]
    dqk = (conv_ch - dv) // 2
    n_main = conv_ch + dv
    proj, ba = _mod_matmul(x, sc, sh, w_in[layer][:, :n_main], w_in[layer][:, n_main:], out_dtype=BF16,
                           name="gdn_in_proj")
    q = _gdn_conv(proj, conv_w, layer, 0, dqk, normalize=True, scale=HEAD_DIM ** -0.5)
    k = _gdn_conv(proj, conv_w, layer, dqk, dqk, normalize=True, scale=1.0)
    v = _gdn_conv(proj, conv_w, layer, 2 * dqk, dv, normalize=False, scale=1.0)
    gates = _gdn_gates(ba, a_log, dt_bias)
    o = _gdn_chunk(q, k, v, gates, gates[:, nvh:2 * nvh].T)
    return _gdn_post(o, proj, conv_ch, norm_w)


def _rope(x, c, sg):
    return x * c + pltpu.roll(x, HEAD_DIM // 2, 1) * sg


def _nsa_prep_kernel(p_ref, c_ref, s_ref, q_ref, kvc_ref, ks_ref, vs_ref, kw_ref, vw_ref, *, dq, dkv):
    c = c_ref[...]
    sg = s_ref[...]
    dh = HEAD_DIM

    def head(col0, hh, rope, mul):
        x = p_ref[:, col0 + hh * dh:col0 + (hh + 1) * dh]
        if rope:
            x = _rope(x, c, sg)
        return x if mul == 1.0 else x * mul

    for hh in range(dq // dh):
        q_ref[:, hh * dh:(hh + 1) * dh] = head(0, hh, True, dh ** -0.5 * LOG2E).astype(q_ref.dtype)
    for hh in range(dkv // dh):
        sl = slice(hh * dh, (hh + 1) * dh)
        kvc_ref[0, hh] = head(dq, hh, True, 1.0)
        kvc_ref[1, hh] = head(dq + dkv, hh, False, 1.0)
        ks_ref[:, sl] = head(dq + 2 * dkv, hh, True, 1.0).astype(ks_ref.dtype)
        vs_ref[:, sl] = head(dq + 3 * dkv, hh, False, 1.0).astype(vs_ref.dtype)
        kw_ref[:, sl] = head(dq + 4 * dkv, hh, True, 1.0).astype(kw_ref.dtype)
        vw_ref[:, sl] = head(dq + 5 * dkv, hh, False, 1.0).astype(vw_ref.dtype)


def _nsa_prep(proj, cos2, sin2, dq, dkv, *, ts=256):
    s, n = proj.shape
    hk = dkv // HEAD_DIM
    kv_spec = pl.BlockSpec((ts, dkv), lambda i: (i, 0))
    tab = pl.BlockSpec((ts, HEAD_DIM), lambda i: (i, 0))
    kv16 = jax.ShapeDtypeStruct((s, dkv), BF16)
    return pl.pallas_call(
        functools.partial(_nsa_prep_kernel, dq=dq, dkv=dkv),
        grid=(s // ts,),
        in_specs=[pl.BlockSpec((ts, n), lambda i: (i, 0)), tab, tab],
        out_specs=[pl.BlockSpec((ts, dq), lambda i: (i, 0)),
                   pl.BlockSpec((2, hk, ts, HEAD_DIM), lambda i: (0, 0, i, 0)),
                   kv_spec, kv_spec, kv_spec, kv_spec],
        out_shape=[jax.ShapeDtypeStruct((s, dq), BF16),
                   jax.ShapeDtypeStruct((2, hk, s, HEAD_DIM), F32),
                   kv16, kv16, kv16, kv16],
        compiler_params=_params("parallel"),
        name="nsa_prep",
    )(proj, cos2, sin2)


def _nsa_compress_kernel(x_ref, pe_ref, w1_ref, w2_ref, o_ref):
    x = x_ref[0, 0]
    n = x.shape[0]
    a = _dot(x + pe_ref[0, 0:1, :], w1_ref[0, 0])
    b = _dot(x + pe_ref[0, 1:2, :], w1_ref[0, 1])
    y = a + pltpu.roll(b, n - 1, 0)
    o_ref[0, 0] = _dot(_silu(y), w2_ref[0])


def _nsa_compress(x, pe, w1, w2):
    _, hk, ng, gw = x.shape
    half = CMP_LEN // 2
    pe2 = pe.reshape(2, 2, half * HEAD_DIM)
    w1r = w1.reshape(2, 2, half * HEAD_DIM, HEAD_DIM)
    return pl.pallas_call(
        _nsa_compress_kernel,
        grid=(2, hk),
        in_specs=[pl.BlockSpec((1, 1, ng, gw), lambda t, h: (t, h, 0, 0)),
                  pl.BlockSpec((1, 2, gw), lambda t, h: (t, 0, 0)),
                  pl.BlockSpec((1, 2, gw, HEAD_DIM), lambda t, h: (t, 0, 0, 0)),
                  pl.BlockSpec((1, HEAD_DIM, HEAD_DIM), lambda t, h: (t, 0, 0))],
        out_specs=pl.BlockSpec((1, 1, ng, HEAD_DIM), lambda t, h: (t, h, 0, 0)),
        out_shape=jax.ShapeDtypeStruct((2, hk, ng, HEAD_DIM), F32),
        compiler_params=_params("parallel", "parallel"),
        name="nsa_compress",
    )(x, pe2, w1r, w2)


def _nsa_cmp_select_kernel(q_ref, kc_ref, vc_ref, ov_ref, o_ref, sel_ref):
    tq = q_ref.shape[0]
    nc = kc_ref.shape[2]
    nbp = ov_ref.shape[1]
    t0 = pl.program_id(0) * tq
    tpos = t0 + lax.broadcasted_iota(jnp.int32, (tq, nc), 0)
    nidx = lax.broadcasted_iota(jnp.int32, (tq, nc), 1)
    valid = (nidx * CMP_STRIDE + (CMP_LEN - 1)) <= tpos
    kc = kc_ref[0, 0].astype(BF16)
    vc = vc_ref[0, 0].astype(BF16)
    psum = jnp.zeros((tq, nc), F32)
    q4 = jnp.concatenate([q_ref[:, g * HEAD_DIM:(g + 1) * HEAD_DIM] for g in range(NSA_GROUP)], axis=0)
    s_all = _dot_nt(q4, kc)
    probs = []
    for g in range(NSA_GROUP):
        s = jnp.where(valid, s_all[g * tq:(g + 1) * tq, :], NEG_BIG)
        m = jnp.max(s, axis=-1, keepdims=True)
        p = jnp.where(valid, jnp.exp2(s - m), 0.0)
        den = jnp.sum(p, axis=-1, keepdims=True)
        p = p / jnp.where(den > 0.0, den, 1.0)
        probs.append(p.astype(BF16))
        psum = psum + p
    o_all = jnp.dot(jnp.concatenate(probs, axis=0), vc, preferred_element_type=F32)
    for g in range(NSA_GROUP):
        o_ref[:, g * HEAD_DIM:(g + 1) * HEAD_DIM] = o_all[g * tq:(g + 1) * tq, :].astype(o_ref.dtype)
    imp = _dot_sel(psum, ov_ref[...])
    imp_t = imp.T
    j = lax.broadcasted_iota(jnp.int32, (nbp, tq), 0)
    cur = (t0 + lax.broadcasted_iota(jnp.int32, (nbp, tq), 1)) // SLC_LEN
    forced = (j == 0) | (j == cur) | (j == cur - 1)
    work = jnp.where(j <= cur, jnp.where(forced, jnp.inf, imp_t), -jnp.inf)
    jf = j.astype(F32)
    for _ in range(SLC_TOP):
        m = jnp.max(work, axis=0, keepdims=True)
        first = jnp.min(jnp.where(work == m, jf, float(nbp)), axis=0, keepdims=True)
        work = jnp.where(jf == first, -jnp.inf, work)
    sel_ref[0] = jnp.where((j <= cur) & (work == -jnp.inf), 0.0, MASK_NEG).T.astype(sel_ref.dtype)


def _nsa_cmp_select(q, cmp, ov, *, tq=512):
    s, dq = q.shape
    _, hk, nc, _ = cmp.shape
    nbp = ov.shape[1]
    gw = NSA_GROUP * HEAD_DIM
    return pl.pallas_call(
        _nsa_cmp_select_kernel,
        grid=(s // tq, hk),
        in_specs=[pl.BlockSpec((tq, gw), lambda i, h: (i, h)),
                  pl.BlockSpec((1, 1, nc, HEAD_DIM), lambda i, h: (0, h, 0, 0)),
                  pl.BlockSpec((1, 1, nc, HEAD_DIM), lambda i, h: (1, h, 0, 0)),
                  pl.BlockSpec((nc, nbp), lambda i, h: (0, 0))],
        out_specs=[pl.BlockSpec((tq, gw), lambda i, h: (i, h)),
                   pl.BlockSpec((1, tq, nbp), lambda i, h: (h, i, 0))],
        out_shape=[jax.ShapeDtypeStruct((s, dq), BF16),
                   jax.ShapeDtypeStruct((hk, s, nbp), BF16)],
        compiler_params=_params("parallel", "parallel"),
        name="nsa_cmp_select",
    )(q, cmp, cmp, ov)


ATTN_ROWS = 64


def _masked_attn_kernel(*refs, tq, nk, sub, selected):
    qi_ref, kj_ref, first_ref, last_ref, q_ref = refs[:5]
    k_refs, v_refs = refs[5:5 + nk], refs[5 + nk:5 + 2 * nk]
    rest = refs[5 + 2 * nk:]
    if selected:
        sel_ref, expand_ref = rest[:2]
        rest = rest[2:]
    o_ref, q4, bias, s_scr, p_scr, a_scr, m_scr, acc = rest
    t = pl.program_id(1)
    qi = qi_ref[t]
    kj = kj_ref[t]
    dh = HEAD_DIM
    tk = nk * k_refs[0].shape[0]

    @pl.when(first_ref[t] == 1)
    def _():
        for g in range(NSA_GROUP):
            q4[g * tq:(g + 1) * tq, :] = q_ref[:, g * dh:(g + 1) * dh]
        m_scr[...] = jnp.full_like(m_scr, NEG_BIG)
        acc[...] = jnp.zeros_like(acc)

    k0 = kj * k_refs[0].shape[0]

    def causal():
        tpos = qi * tq + lax.broadcasted_iota(jnp.int32, (tq, tk), 0)
        kpos = k0 + lax.broadcasted_iota(jnp.int32, (tq, tk), 1)
        return tpos, kpos

    if selected:
        bias[...] = jnp.dot(sel_ref[0], expand_ref[...], preferred_element_type=F32).astype(BF16)

        @pl.when(k0 + tk - 1 > qi * tq)
        def _():
            tpos, kpos = causal()
            bias[...] = bias[...] + jnp.where(kpos <= tpos, 0.0, MASK_NEG).astype(BF16)
    else:
        tpos, kpos = causal()
        bias[...] = jnp.where((kpos <= tpos) & (kpos > tpos - WINDOW) & (kpos >= 0),
                              0.0, MASK_NEG).astype(BF16)

    tiles = [(r, u * sub) for r in range(nk) for u in range(k_refs[0].shape[0] // sub)]
    for n, (r, off) in enumerate(tiles):
        s_scr[:, n * sub:(n + 1) * sub] = lax.dot_general(
            q4[...], k_refs[r][off:off + sub, :], (((1,), (1,)), ((), ())),
            preferred_element_type=F32).astype(BF16)
    rb = ATTN_ROWS
    for u, (r, off) in enumerate(tiles):
        cols = slice(u * sub, (u + 1) * sub)
        for r0 in range(0, NSA_GROUP * tq, rb):
            rows = slice(r0, r0 + rb)
            brow = r0 % tq
            s = s_scr[rows, cols] + bias[brow:brow + rb, cols]
            m_prev = m_scr[rows, :]
            m_new = jnp.maximum(m_prev, jnp.max(s, axis=-1, keepdims=True).astype(F32))
            a_scr[u, rows, :] = jnp.exp2(m_prev - m_new)
            shift = jnp.concatenate([m_new.astype(BF16)] * (sub // LANES), axis=1)
            p_scr[rows, cols] = jnp.exp2(s - shift)
            m_scr[rows, :] = m_new
        v_one = jnp.concatenate([v_refs[r][off:off + sub, :], jnp.ones((sub, dh), BF16)], axis=1)
        alpha = jnp.concatenate([a_scr[u]] * 2, axis=1)
        acc[...] = alpha * acc[...] + jnp.dot(p_scr[:, cols], v_one, preferred_element_type=F32)

    @pl.when(last_ref[t] == 1)
    def _():
        for g in range(NSA_GROUP):
            rows = slice(g * tq, (g + 1) * tq)
            o_ref[:, g * dh:(g + 1) * dh] = (acc[rows, :dh] / acc[rows, dh:]).astype(o_ref.dtype)


def _masked_attn(q, k, v, sel, expand, *, tq, tk, sub):
    s, dq = q.shape
    hk = k.shape[1] // HEAD_DIM
    gw = NSA_GROUP * HEAD_DIM
    selected = sel is not None
    if selected:
        nk = 1
        pairs = []
        for i in range(s // tq):
            hi = (i * tq + tq - 1) // tk
            pairs += [(i, j, int(j == 0), int(j == hi)) for j in range(hi + 1)]
    else:
        assert tq % tk == 0
        nback = -(-(WINDOW - 1) // tk)
        nk = nback + tq // tk
        pairs = [(i, i * (tq // tk) - nback, 1, 1) for i in range(s // tq)]
    sched = [jnp.asarray([p[c] for p in pairs], jnp.int32) for c in range(4)]

    def kv_spec(r):
        return pl.BlockSpec((tk, HEAD_DIM), lambda h, t, qi, kj, fi, la: (jnp.maximum(kj[t] + r, 0), h))

    in_specs = [pl.BlockSpec((tq, gw), lambda h, t, qi, kj, fi, la: (qi[t], h))]
    in_specs += [kv_spec(r) for r in range(nk)] * 2
    args = [q] + [k] * nk + [v] * nk
    if selected:
        in_specs.append(pl.BlockSpec((1, tq, sel.shape[2]), lambda h, t, qi, kj, fi, la: (h, qi[t], 0)))
        in_specs.append(pl.BlockSpec((expand.shape[0], tk), lambda h, t, qi, kj, fi, la: (0, kj[t])))
        args += [sel, expand]
    rows = NSA_GROUP * tq
    tk = nk * tk
    return pl.pallas_call(
        functools.partial(_masked_attn_kernel, tq=tq, nk=nk, sub=sub, selected=selected),
        grid_spec=pltpu.PrefetchScalarGridSpec(
            num_scalar_prefetch=4,
            grid=(hk, len(pairs)),
            in_specs=in_specs,
            out_specs=pl.BlockSpec((tq, gw), lambda h, t, qi, kj, fi, la: (qi[t], h)),
            scratch_shapes=[pltpu.VMEM((rows, HEAD_DIM), BF16),
                            pltpu.VMEM((tq, tk), BF16),
                            pltpu.VMEM((rows, tk), BF16),
                            pltpu.VMEM((rows, tk), BF16),
                            pltpu.VMEM((tk // sub, rows, LANES), F32),
                            pltpu.VMEM((rows, LANES), F32),
                            pltpu.VMEM((rows, 2 * HEAD_DIM), F32)]),
        out_shape=jax.ShapeDtypeStruct((s, dq), BF16),
        compiler_params=_params("parallel", "arbitrary"),
        name="nsa_selected_attn" if selected else "nsa_window_attn",
    )(*sched, *args)


def _nsa_combine_kernel(oc_ref, os_ref, ow_ref, g_ref, out_ref):
    gates = jax.nn.sigmoid(g_ref[...])
    ng, dq = g_ref.shape[1], oc_ref.shape[1]
    row = lax.broadcasted_iota(jnp.int32, (ng, dq), 0)
    head3 = (lax.broadcasted_iota(jnp.int32, (ng, dq), 1) // HEAD_DIM) * N_BRANCH
    acc = None
    for b, ref in enumerate((oc_ref, os_ref, ow_ref)):
        gb = _dot_sel(gates, jnp.where(row == head3 + b, 1.0, 0.0))
        term = gb * ref[...]
        acc = term if acc is None else acc + term
    out_ref[...] = acc.astype(out_ref.dtype)


def _nsa_combine(oc, osel, ow, gates, *, ts=256):
    s, dq = oc.shape
    spec = pl.BlockSpec((ts, dq), lambda i: (i, 0))
    return pl.pallas_call(
        _nsa_combine_kernel,
        grid=(s // ts,),
        in_specs=[spec, spec, spec, pl.BlockSpec((ts, gates.shape[1]), lambda i: (i, 0))],
        out_specs=spec,
        out_shape=jax.ShapeDtypeStruct((s, dq), BF16),
        compiler_params=_params("parallel"),
        name="nsa_combine",
    )(oc, osel, ow, gates)


def _nsa_attention(x, sc, sh, cos2, sin2, w_in, layer, cmp_pe, cmp_w1, cmp_w2):
    s, d = x.shape
    dq = d
    hk = d // HEAD_DIM // NSA_GROUP
    dkv = hk * HEAD_DIM
    n_main = dq + 6 * dkv
    ngate = w_in.shape[2] - n_main
    w_gate = jnp.pad(w_in[layer][:, n_main:], ((0, 0), (0, LANES - ngate)))
    proj, gates = _mod_matmul(x, sc, sh, w_in[layer][:, :n_main], w_gate, name="nsa_in_proj")
    q, kvc, ks, vs, kw, vw = _nsa_prep(proj, cos2, sin2, dq, dkv)
    groups = kvc.reshape(2, hk, s // CMP_STRIDE, CMP_STRIDE * HEAD_DIM)
    cmp = _nsa_compress(groups, cmp_pe, cmp_w1, cmp_w2)

    nc = s // CMP_STRIDE
    n_slc = s // SLC_LEN
    nbp = -(-n_slc // LANES) * LANES
    c_start = CMP_STRIDE * jnp.arange(nc)
    s_start = SLC_LEN * jnp.arange(nbp)
    overlap = jnp.clip(jnp.minimum(c_start[:, None] + CMP_LEN, s_start[None, :] + SLC_LEN)
                       - jnp.maximum(c_start[:, None], s_start[None, :]), 0, None).astype(F32) / CMP_LEN
    o_c, sel = _nsa_cmp_select(q, cmp, overlap.astype(BF16))
    expand = (jnp.arange(nbp)[:, None] == jnp.arange(s)[None, :] // SLC_LEN).astype(BF16)
    o_s = _masked_attn(q, ks, vs, sel, expand, tq=512, tk=1024, sub=512)
    o_w = _masked_attn(q, kw, vw, None, None, tq=256, tk=256, sub=256)
    return _nsa_combine(o_c, o_s, o_w, gates)


def kernel(x, c, positions, mod_w, mod_b, ln_g, ln_b, ffn_w_gu, ffn_w_down, gdn_w_in, gdn_conv_w,
           gdn_a_log, gdn_dt_bias, gdn_norm_w, gdn_w_out, nsa_w_in, nsa_cmp_pe, nsa_cmp_w1,
           nsa_cmp_w2, nsa_w_out):
    bsz, s, d = x.shape
    assert bsz == 1
    depth = mod_w.shape[0]
    alpha = (2 * depth) ** 0.25

    inv = ROPE_THETA ** (-jnp.arange(0, HEAD_DIM, 2, dtype=F32) / HEAD_DIM)
    ang = positions[0].astype(F32)[:, None] * inv
    cos, sin = jnp.cos(ang), jnp.sin(ang)
    cos2 = jnp.concatenate([cos, cos], axis=-1)
    sin2 = jnp.concatenate([-sin, sin], axis=-1)

    mod = _modulation(c, mod_w, mod_b)
    gdn_w_out, nsa_w_out, ffn_w_down = [w.astype(BF16) for w in (gdn_w_out, nsa_w_out, ffn_w_down)]
    xs = x[0]
    for i in range(depth):
        sh1, sc1, ga1, sh2, sc2, ga2 = [mod[i, :, r * d:(r + 1) * d] for r in range(6)]
        j = i // N_MIXERS
        if i % N_MIXERS == 0:
            y = _gated_deltanet(xs, sc1, sh1, gdn_w_in, gdn_conv_w, j, gdn_a_log[j],
                                gdn_dt_bias[j], gdn_norm_w[j])
            w_out = gdn_w_out
        else:
            y = _nsa_attention(xs, sc1, sh1, cos2, sin2, nsa_w_in, j, nsa_cmp_pe[j],
                               nsa_cmp_w1[j], nsa_cmp_w2[j])
            w_out = nsa_w_out
        xs = _matmul_ln(y, w_out, j, xs, ga1, ln_g[i, 0:1], ln_b[i, 0:1], alpha=alpha)
        a = _swiglu_up(xs, sc2, sh2, ffn_w_gu, i)
        xs = _matmul_ln(a, ffn_w_down, i, xs, ga2, ln_g[i, 1:2], ln_b[i, 1:2], alpha=alpha)
    return xs[None]
```
